```python
import math
import jax, jax.numpy as jnp
from jax import lax
import numpy as np

D_MODEL = 2048
BATCH = 16
SEQ = 2048
DEPTH = 4

N_MIXERS = 2
MEM_LEN = 256
MIX_WIDTH = D_MODEL
MEM_HEADS = 4
MEM_HEAD_DIM = MIX_WIDTH // (4 * MEM_HEADS)
MEM_WIDTH = MEM_HEADS * MEM_HEAD_DIM
SEQ_WIDTH = MIX_WIDTH - MEM_WIDTH

RWKV_HEAD = 64
RWKV_HEADS = SEQ_WIDTH // RWKV_HEAD
LORA_W = 96
LORA_A = 96
LORA_V = 64
LORA_G = 256
RWKV_GN_EPS = 1e-5 * RWKV_HEAD

RET_HEADS = 6
RET_DV = SEQ_WIDTH // RET_HEADS
RET_DK = RET_DV // 2
RET_CHUNK = 128
ROPE_BASE = 10000.0

D_FF = 5632
N_EXPERTS = 8
TOP_K = 2
D_FF_EXPERT = 5632

ALPHA = (2.0 * DEPTH) ** 0.25
BETA = (8.0 * DEPTH) ** -0.25
LN_EPS = 1e-5

kernel_name = "hybrid_rwkv7_retention_memattn_moe_deepnorm"


def layer_norm(x, g, b, eps=LN_EPS):
    xf = x.astype(jnp.float32)
    mu = jnp.mean(xf, -1, keepdims=True)
    var = jnp.mean(jnp.square(xf - mu), -1, keepdims=True)
    return ((xf - mu) * lax.rsqrt(var + eps)).astype(x.dtype) * g + b


def head_norm(y, g, b, eps):
    B, S, H, d = y.shape
    yf = y.astype(jnp.float32)
    mu = jnp.mean(yf, -1, keepdims=True)
    var = jnp.mean(jnp.square(yf - mu), -1, keepdims=True)
    yn = ((yf - mu) * lax.rsqrt(var + eps)).astype(y.dtype)
    return yn.reshape(B, S, H * d) * g + b


def split_cols(p, sizes):
    idx = [int(i) for i in np.cumsum(sizes)[:-1]]
    return jnp.split(p, idx, axis=-1)


def token_shift(p):
    return jnp.pad(p, ((0, 0), (1, 0), (0, 0)))[:, :-1]


def rwkv7_scan(r, w, k, v, kk, a):
    f32 = jnp.float32
    B, S, H, N = r.shape
    xs = tuple(jnp.moveaxis(t.astype(f32), 1, 0) for t in (r, w, k, v, kk, a))

    def step(state, inp):
        r_t, w_t, k_t, v_t, kk_t, a_t = inp
        sa = jnp.einsum('bhij,bhj->bhi', state, -kk_t)
        state = (state * w_t[:, :, None, :]
                 + sa[..., None] * (kk_t * a_t)[:, :, None, :]
                 + v_t[..., None] * k_t[:, :, None, :])
        y_t = jnp.einsum('bhij,bhj->bhi', state, r_t)
        return state, y_t

    s0 = jnp.zeros((B, H, N, N), f32)
    _, y = lax.scan(step, s0, xs)
    return jnp.moveaxis(y, 0, 1).astype(r.dtype)


def rwkv7_mixer(p, mu, up_w, up_a, up_g, up_v, vecs, v_first):
    B, S, _ = p.shape
    p = p + mu * (token_shift(p) - p)
    if up_v is None:
        r, k, v, xw, xa, xg = split_cols(p, [SEQ_WIDTH] * 3 + [LORA_W, LORA_A, LORA_G])
        w0, a0, k_k, k_a, r_k, gn_g, gn_b = vecs
        v_first = v
    else:
        r, k, v, xw, xa, xv, xg = split_cols(p, [SEQ_WIDTH] * 3 + [LORA_W, LORA_A, LORA_V, LORA_G])
        w0, a0, v0, k_k, k_a, r_k, gn_g, gn_b = vecs
        v = v + (v_first - v) * jax.nn.sigmoid(v0 + xv @ up_v)
    w_log = -jax.nn.softplus(-(w0 + jnp.tanh(xw) @ up_w)) - 0.5
    decay = jnp.exp(-jnp.exp(w_log.astype(jnp.float32))).astype(p.dtype)
    a = jax.nn.sigmoid(a0 + xa @ up_a)
    g = jax.nn.sigmoid(xg) @ up_g

    heads = lambda t: t.reshape(B, S, RWKV_HEADS, RWKV_HEAD)
    kk = heads(k * k_k).astype(jnp.float32)
    kk = (kk / jnp.maximum(jnp.sqrt(jnp.sum(jnp.square(kk), -1, keepdims=True)), 1e-12)).astype(p.dtype)
    k = k * (1.0 + (a - 1.0) * k_a)
    r_h, k_h, v_h = heads(r), heads(k), heads(v)
    y = rwkv7_scan(r_h, heads(decay), k_h, v_h, kk, heads(a))
    bonus = jnp.sum(r_h * k_h * r_k.reshape(RWKV_HEADS, RWKV_HEAD), -1, keepdims=True) * v_h
    y = head_norm(y, gn_g, gn_b, RWKV_GN_EPS) + bonus.reshape(B, S, SEQ_WIDTH)
    return y * g, v_first


def rotary(x, pos):
    d = x.shape[-1]
    inv = 1.0 / (ROPE_BASE ** (jnp.arange(0, d, 2, dtype=jnp.float32) / d))
    ang = pos.astype(jnp.float32)[..., None] * inv
    cos = jnp.cos(ang)[:, :, None, :]
    sin = jnp.sin(ang)[:, :, None, :]
    x1, x2 = jnp.split(x.astype(jnp.float32), 2, axis=-1)
    return jnp.concatenate([x1 * cos - x2 * sin, x1 * sin + x2 * cos], -1).astype(x.dtype)


def retention(q, k, v, pos):
    f32 = jnp.float32
    B, S, H, dk = q.shape
    dv = v.shape[-1]
    C = RET_CHUNK
    nC = S // C
    q = rotary(q, pos)
    k = rotary(k, pos) * (dk ** -0.5)
    log_g = jnp.log1p(-jnp.exp2(-5.0 - jnp.arange(H, dtype=f32)))
    idx = jnp.arange(C, dtype=f32)
    diff = idx[:, None] - idx[None, :]
    intra_decay = jnp.where(diff >= 0, jnp.exp(jnp.maximum(diff, 0.0)[None] * log_g[:, None, None]), 0.0)
    q_decay = jnp.exp((idx + 1.0)[:, None] * log_g[None, :])[None, :, :, None]
    k_decay = jnp.exp((C - 1.0 - idx)[:, None] * log_g[None, :])[None, :, :, None]
    chunk_decay = jnp.exp(C * log_g)[None, :, None, None]

    def chunks(t):
        return jnp.moveaxis(t.astype(f32).reshape(B, nC, C, H, t.shape[-1]), 1, 0)

    def step(R, inp):
        qc, kc, vc = inp
        s = jnp.einsum('bnhd,bmhd->bhnm', qc, kc) * intra_decay[None]
        o = jnp.einsum('bhnm,bmhe->bnhe', s, vc)
        o = o + jnp.einsum('bnhd,bhde->bnhe', qc * q_decay, R)
        R = R * chunk_decay + jnp.einsum('bmhd,bmhe->bhde', kc * k_decay, vc)
        return R, o

    R0 = jnp.zeros((B, H, dk, dv), f32)
    _, o = lax.scan(step, R0, (chunks(q), chunks(k), chunks(v)))
    return jnp.moveaxis(o, 0, 1).reshape(B, S, H, dv).astype(v.dtype)


def retention_mixer(p, positions, gn):
    B, S, _ = p.shape
    q, k, v, g = split_cols(p, [RET_HEADS * RET_DK, RET_HEADS * RET_DK, SEQ_WIDTH, SEQ_WIDTH])
    o = retention(q.reshape(B, S, RET_HEADS, RET_DK), k.reshape(B, S, RET_HEADS, RET_DK),
                  v.reshape(B, S, RET_HEADS, RET_DV), positions)
    return jax.nn.silu(g) * head_norm(o, gn[0], gn[1], LN_EPS)


def memory_attention(q, mem_k, mem_v):
    B, S, _ = q.shape
    q = q.reshape(B, S, MEM_HEADS, MEM_HEAD_DIM)
    s = jnp.einsum('bshd,bmhd->bhsm', q, mem_k).astype(jnp.float32) * (MEM_HEAD_DIM ** -0.5)
    p = jax.nn.softmax(s, axis=-1).astype(mem_v.dtype)
    return jnp.einsum('bhsm,bmhd->bshd', p, mem_v).reshape(B, S, MEM_WIDTH)


def swiglu(x, w_gate, w_up, w_down):
    return (jax.nn.silu(x @ w_gate) * (x @ w_up)) @ w_down


def moe_swiglu(x, w_router, w_gate, w_up, w_down):
    logits = (x @ w_router).astype(jnp.float32)
    top_val, top_idx = lax.top_k(logits, TOP_K)
    top_w = jax.nn.softmax(top_val, axis=-1)
    combine = jnp.sum(jax.nn.one_hot(top_idx, N_EXPERTS, dtype=jnp.float32) * top_w[..., None], axis=-2)
    combine = combine.astype(x.dtype)
    y = jnp.zeros_like(x)
    for e in range(N_EXPERTS):
        y = y + combine[..., e:e + 1] * swiglu(x, w_gate[e], w_up[e], w_down[e])
    return y


def setup_inputs(seed: int = 0) -> dict:
    key = jax.random.key(seed)
    keys = iter(jax.random.split(key, 128))
    f32 = jnp.float32

    def normal(shape, scale):
        return jax.random.normal(next(keys), shape, f32) * scale

    def uniform(shape, lo, hi):
        return jax.random.uniform(next(keys), shape, f32, lo, hi)

    def gain(n):
        return 1.0 + normal((n,), 0.02)

    def proj_in(sizes, scales):
        col_scale = jnp.concatenate([jnp.full((n,), s, f32) for n, s in zip(sizes, scales)])
        return normal((D_MODEL, sum(sizes)), D_MODEL ** -0.5) * col_scale

    def norms():
        return jnp.stack([gain(D_MODEL), normal((D_MODEL,), 0.02), gain(D_MODEL), normal((D_MODEL,), 0.02)])

    inp = {}
    inp["x"] = normal((BATCH, SEQ, D_MODEL), 1.0)
    inp["mem"] = normal((BATCH, MEM_LEN, D_MODEL), 1.0)
    offset = jax.random.randint(next(keys), (BATCH, 1), 0, 4096, dtype=jnp.int32)
    inp["positions"] = offset + jnp.arange(SEQ, dtype=jnp.int32)[None, :]
    inp["mem_w_kv"] = proj_in([MEM_WIDTH, MEM_WIDTH], [1.0, BETA])

    def add_rwkv(pre, with_v):
        sizes = [SEQ_WIDTH] * 3 + [LORA_W, LORA_A] + ([LORA_V] if with_v else []) + [LORA_G]
        scales = [1.0, 1.0, BETA, 1.0, 1.0] + ([1.0] if with_v else []) + [1.0]
        inp[pre + "w_in"] = proj_in(sizes + [MEM_WIDTH], scales + [1.0])
        inp[pre + "mu"] = uniform((sum(sizes),), 0.0, 1.0)
        inp[pre + "lora_w"] = normal((LORA_W, SEQ_WIDTH), 0.5 * LORA_W ** -0.5)
        inp[pre + "lora_a"] = normal((LORA_A, SEQ_WIDTH), 0.5 * LORA_A ** -0.5)
        if with_v:
            inp[pre + "lora_v"] = normal((LORA_V, SEQ_WIDTH), 0.5 * LORA_V ** -0.5)
        inp[pre + "lora_g"] = normal((LORA_G, SEQ_WIDTH), LORA_G ** -0.5)
        vecs = [uniform((SEQ_WIDTH,), -6.5, -1.5), normal((SEQ_WIDTH,), 0.5)]
        if with_v:
            vecs.append(normal((SEQ_WIDTH,), 0.5))
        vecs += [0.85 + normal((SEQ_WIDTH,), 0.1), 1.0 + normal((SEQ_WIDTH,), 0.1),
                 normal((SEQ_WIDTH,), 0.1), gain(SEQ_WIDTH), normal((SEQ_WIDTH,), 0.02)]
        inp[pre + "vecs"] = jnp.stack(vecs)
        inp[pre + "w_out"] = normal((MIX_WIDTH, D_MODEL), BETA * MIX_WIDTH ** -0.5)
        inp[pre + "norms"] = norms()
        inp[pre + "ffn_gate"] = normal((D_MODEL, D_FF), BETA * D_MODEL ** -0.5)
        inp[pre + "ffn_up"] = normal((D_MODEL, D_FF), BETA * D_MODEL ** -0.5)
        inp[pre + "ffn_down"] = normal((D_FF, D_MODEL), BETA * D_FF ** -0.5)

    def add_ret(pre):
        qk = RET_HEADS * RET_DK
        inp[pre + "w_in"] = proj_in([qk, qk, SEQ_WIDTH, SEQ_WIDTH, MEM_WIDTH], [1.0, 1.0, BETA, 1.0, 1.0])
        inp[pre + "gn"] = jnp.stack([gain(SEQ_WIDTH), normal((SEQ_WIDTH,), 0.02)])
        inp[pre + "w_out"] = normal((MIX_WIDTH, D_MODEL), BETA * MIX_WIDTH ** -0.5)
        inp[pre + "norms"] = norms()
        inp[pre + "router"] = normal((D_MODEL, N_EXPERTS), D_MODEL ** -0.5)
        inp[pre + "moe_gate"] = normal((N_EXPERTS, D_MODEL, D_FF_EXPERT), BETA * D_MODEL ** -0.5)
        inp[pre + "moe_up"] = normal((N_EXPERTS, D_MODEL, D_FF_EXPERT), BETA * D_MODEL ** -0.5)
        inp[pre + "moe_down"] = normal((N_EXPERTS, D_FF_EXPERT, D_MODEL), BETA * D_FF_EXPERT ** -0.5)

    add_rwkv("l0_", False)
    add_ret("l1_")
    add_rwkv("l2_", True)
    add_ret("l3_")
    return inp


def reference(x, mem, positions, mem_w_kv,
              l0_w_in, l0_mu, l0_lora_w, l0_lora_a, l0_lora_g, l0_vecs, l0_w_out, l0_norms,
              l0_ffn_gate, l0_ffn_up, l0_ffn_down,
              l1_w_in, l1_gn, l1_w_out, l1_norms, l1_router, l1_moe_gate, l1_moe_up, l1_moe_down,
              l2_w_in, l2_mu, l2_lora_w, l2_lora_a, l2_lora_v, l2_lora_g, l2_vecs, l2_w_out, l2_norms,
              l2_ffn_gate, l2_ffn_up, l2_ffn_down,
              l3_w_in, l3_gn, l3_w_out, l3_norms, l3_router, l3_moe_gate, l3_moe_up, l3_moe_down):
    B, M, _ = mem.shape
    mem_kv = mem @ mem_w_kv
    mem_k, mem_v = [t.reshape(B, M, MEM_HEADS, MEM_HEAD_DIM) for t in jnp.split(mem_kv, 2, axis=-1)]

    layers = [
        dict(w_in=l0_w_in, mixer=(l0_mu, l0_lora_w, l0_lora_a, l0_lora_g, None, l0_vecs),
             w_out=l0_w_out, norms=l0_norms, ffn=(l0_ffn_gate, l0_ffn_up, l0_ffn_down)),
        dict(w_in=l1_w_in, mixer=(l1_gn,), w_out=l1_w_out, norms=l1_norms,
             ffn=(l1_router, l1_moe_gate, l1_moe_up, l1_moe_down)),
        dict(w_in=l2_w_in, mixer=(l2_mu, l2_lora_w, l2_lora_a, l2_lora_g, l2_lora_v, l2_vecs),
             w_out=l2_w_out, norms=l2_norms, ffn=(l2_ffn_gate, l2_ffn_up, l2_ffn_down)),
        dict(w_in=l3_w_in, mixer=(l3_gn,), w_out=l3_w_out, norms=l3_norms,
             ffn=(l3_router, l3_moe_gate, l3_moe_up, l3_moe_down)),
    ]

    v_first = None
    for i in range(DEPTH):
        L = layers[i]
        p = x @ L["w_in"]
        p_seq, q_mem = p[..., :-MEM_WIDTH], p[..., -MEM_WIDTH:]
        if i % N_MIXERS == 0:
            y_seq, v_first = rwkv7_mixer(p_seq, *L["mixer"], v_first)
        else:
            y_seq = retention_mixer(p_seq, positions, *L["mixer"])
        y_mem = memory_attention(q_mem, mem_k, mem_v)
        h = jnp.concatenate([y_seq, y_mem], axis=-1) @ L["w_out"]
        nrm = L["norms"]
        x = layer_norm(ALPHA * x + h, nrm[0], nrm[1])
        if i % 2 == 0:
            f = swiglu(x, *L["ffn"])
        else:
            f = moe_swiglu(x, *L["ffn"])
        x = layer_norm(ALPHA * x + f, nrm[2], nrm[3])
    return x
```

```python
import functools

import numpy as np
import jax
import jax.numpy as jnp
from jax import lax
from jax.experimental import pallas as pl
from jax.experimental.pallas import tpu as pltpu

F32 = jnp.float32
BF16 = jnp.bfloat16

D_MODEL = 2048
DEPTH = 4
MEM_HEADS = 4
MEM_HEAD_DIM = 128
MEM_WIDTH = MEM_HEADS * MEM_HEAD_DIM
SEQ_WIDTH = D_MODEL - MEM_WIDTH

RWKV_HEAD = 64
RWKV_HEADS = SEQ_WIDTH // RWKV_HEAD
LORA_W = 96
LORA_A = 96
LORA_V = 64
LORA_G = 256
LORA_PACK = 256
RWKV_GN_EPS = 1e-5 * RWKV_HEAD
RWKV_CHUNK = 64

RET_HEADS = 6
RET_DV = SEQ_WIDTH // RET_HEADS
RET_DK = RET_DV // 2
RET_CHUNK = 128
ROPE_BASE = 10000.0

N_EXPERTS = 8
TOP_K = 2

ALPHA = (2.0 * DEPTH) ** 0.25
LN_EPS = 1e-5

LANES = 128
Q_MEM_OFF = 3 * SEQ_WIDTH
VMEM_LIMIT = 52 * 1024 * 1024


def _cparams(*semantics):
    return pltpu.CompilerParams(dimension_semantics=semantics, vmem_limit_bytes=VMEM_LIMIT)


def _dot(a, b):
    return jnp.dot(a, b, preferred_element_type=F32)


def _dot_nt(a, b):
    return lax.dot_general(a, b, (((1,), (1,)), ((), ())), preferred_element_type=F32)


def _dot_tn(a, b):
    return lax.dot_general(a, b, (((0,), (0,)), ((), ())), preferred_element_type=F32)


def _layer_norm_rows(z, g, b):
    mu = jnp.mean(z, axis=-1, keepdims=True)
    zc = z - mu
    var = jnp.mean(zc * zc, axis=-1, keepdims=True)
    return zc * lax.rsqrt(var + LN_EPS) * g + b


def _mm_kernel(a_ref, w_ref, o_ref, abf_ref):
    @pl.when(pl.program_id(1) == 0)
    def _():
        abf_ref[...] = a_ref[...].astype(BF16)

    o_ref[...] = _dot(abf_ref[...], w_ref[...]).astype(o_ref.dtype)


def _matmul(a, w, *, tm, tn, out_dtype=F32):
    M, K = a.shape
    N = w.shape[1]
    tm = min(tm, M)
    assert M % tm == 0 and N % tn == 0
    return pl.pallas_call(
        _mm_kernel,
        out_shape=jax.ShapeDtypeStruct((M, N), out_dtype),
        grid=(M // tm, N // tn),
        in_specs=[pl.BlockSpec((tm, K), lambda i, j: (i, 0)),
                  pl.BlockSpec((K, tn), lambda i, j: (0, j))],
        out_specs=pl.BlockSpec((tm, tn), lambda i, j: (i, j)),
        scratch_shapes=[pltpu.VMEM((tm, K), BF16)],
        compiler_params=_cparams("parallel", "arbitrary"),
        name="matmul",
    )(a, w)


def _router_kernel(x_ref, w_ref, o_ref):
    o_ref[...] = jnp.dot(x_ref[...], w_ref[...], preferred_element_type=F32,
                         precision=lax.Precision.HIGHEST)


def _router_logits(x, w_router):
    T = x.shape[0]
    tm = min(512, T)
    wp = jnp.pad(w_router, ((0, 0), (0, LANES - N_EXPERTS)))
    out = pl.pallas_call(
        _router_kernel,
        out_shape=jax.ShapeDtypeStruct((T, LANES), F32),
        grid=(T // tm,),
        in_specs=[pl.BlockSpec((tm, D_MODEL), lambda i: (i, 0)),
                  pl.BlockSpec((D_MODEL, LANES), lambda i: (0, 0))],
        out_specs=pl.BlockSpec((tm, LANES), lambda i: (i, 0)),
        compiler_params=_cparams("parallel"),
        name="router",
    )(x, wp)
    return out[:, :N_EXPERTS]


def _proj_ln_kernel(ys_ref, ym_ref, ws_ref, wm_ref, x_ref, g_ref, b_ref, o_ref, ob_ref):
    h = _dot(ys_ref[...], ws_ref[...]) + _dot(ym_ref[...], wm_ref[...])
    out = _layer_norm_rows(ALPHA * x_ref[...] + h, g_ref[...], b_ref[...])
    o_ref[...] = out
    ob_ref[...] = out.astype(BF16)


def _proj_ln(y_seq, y_mem, w_out, x, gamma, beta):
    T = x.shape[0]
    tm = min(256, T)
    ws = w_out[:SEQ_WIDTH].astype(BF16)
    wm = w_out[SEQ_WIDTH:].astype(BF16)
    row = lambda i: (i, 0)
    fixed = lambda i: (0, 0)
    return pl.pallas_call(
        _proj_ln_kernel,
        out_shape=(jax.ShapeDtypeStruct((T, D_MODEL), F32), jax.ShapeDtypeStruct((T, D_MODEL), BF16)),
        grid=(T // tm,),
        in_specs=[pl.BlockSpec((tm, SEQ_WIDTH), row), pl.BlockSpec((tm, MEM_WIDTH), row),
                  pl.BlockSpec((SEQ_WIDTH, D_MODEL), fixed), pl.BlockSpec((MEM_WIDTH, D_MODEL), fixed),
                  pl.BlockSpec((tm, D_MODEL), row),
                  pl.BlockSpec((1, D_MODEL), fixed), pl.BlockSpec((1, D_MODEL), fixed)],
        out_specs=(pl.BlockSpec((tm, D_MODEL), row), pl.BlockSpec((tm, D_MODEL), row)),
        compiler_params=_cparams("parallel"),
        name="proj_ln",
    )(y_seq, y_mem, ws, wm, x, gamma.reshape(1, -1), beta.reshape(1, -1))


def _swiglu_partial(xb, wg, wu, wd):
    gate = _dot(xb, wg)
    up = _dot(xb, wu)
    h = (gate * jax.nn.sigmoid(gate) * up).astype(BF16)
    return _dot(h, wd)


def _ffn_ln_kernel(xb_ref, wg_ref, wu_ref, wd_ref, x_ref, g_ref, b_ref, o_ref, ob_ref, acc_ref):
    f = pl.program_id(1)
    d = _swiglu_partial(xb_ref[...], wg_ref[...], wu_ref[...], wd_ref[...])

    @pl.when(f == 0)
    def _():
        acc_ref[...] = d

    @pl.when(f > 0)
    def _():
        acc_ref[...] += d

    @pl.when(f == pl.num_programs(1) - 1)
    def _():
        out = _layer_norm_rows(ALPHA * x_ref[...] + acc_ref[...], g_ref[...], b_ref[...])
        o_ref[...] = out
        ob_ref[...] = out.astype(BF16)


def _ffn_ln(x, xb, w_gate, w_up, w_down, gamma, beta):
    T = x.shape[0]
    F = w_gate.shape[1]
    tm = min(512, T)
    tf = 512
    row = lambda i, f: (i, 0)
    fixed = lambda i, f: (0, 0)
    return pl.pallas_call(
        _ffn_ln_kernel,
        out_shape=(jax.ShapeDtypeStruct((T, D_MODEL), F32), jax.ShapeDtypeStruct((T, D_MODEL), BF16)),
        grid=(T // tm, F // tf),
        in_specs=[pl.BlockSpec((tm, D_MODEL), row),
                  pl.BlockSpec((D_MODEL, tf), lambda i, f: (0, f)),
                  pl.BlockSpec((D_MODEL, tf), lambda i, f: (0, f)),
                  pl.BlockSpec((tf, D_MODEL), lambda i, f: (f, 0)),
                  pl.BlockSpec((tm, D_MODEL), row),
                  pl.BlockSpec((1, D_MODEL), fixed), pl.BlockSpec((1, D_MODEL), fixed)],
        out_specs=(pl.BlockSpec((tm, D_MODEL), row), pl.BlockSpec((tm, D_MODEL), row)),
        scratch_shapes=[pltpu.VMEM((tm, D_MODEL), F32)],
        compiler_params=_cparams("parallel", "arbitrary"),
        name="ffn_ln",
    )(xb, w_gate.astype(BF16), w_up.astype(BF16), w_down.astype(BF16), x,
      gamma.reshape(1, -1), beta.reshape(1, -1))


MOE_TM = 512
MOE_TF = 512


def _moe_kernel(te_ref, nu_ref, xs_ref, wg_ref, wu_ref, wd_ref, cw_ref, o_ref, acc_ref):
    i = pl.program_id(0)
    f = pl.program_id(1)
    last = pl.num_programs(1) - 1
    used = i < nu_ref[0]

    @pl.when(used)
    def _():
        d = _swiglu_partial(xs_ref[...], wg_ref[...], wu_ref[...], wd_ref[...])

        @pl.when(f == 0)
        def _():
            acc_ref[...] = d

        @pl.when(f > 0)
        def _():
            acc_ref[...] += d

        @pl.when(f == last)
        def _():
            o_ref[...] = acc_ref[...] * cw_ref[...]

    @pl.when(jnp.logical_and(jnp.logical_not(used), f == last))
    def _():
        o_ref[...] = jnp.zeros_like(o_ref)


def _moe_grouped(xs, cw, tile_expert, n_used, w_gate, w_up, w_down):
    Tp = xs.shape[0]
    F = w_gate.shape[2]
    nf = F // MOE_TF

    def fcol(i, f, nu):
        return jnp.where(i < nu[0], f, nf - 1)

    grid_spec = pltpu.PrefetchScalarGridSpec(
        num_scalar_prefetch=2,
        grid=(Tp // MOE_TM, nf),
        in_specs=[pl.BlockSpec((MOE_TM, D_MODEL), lambda i, f, te, nu: (i, 0)),
                  pl.BlockSpec((None, D_MODEL, MOE_TF), lambda i, f, te, nu: (te[i], 0, fcol(i, f, nu))),
                  pl.BlockSpec((None, D_MODEL, MOE_TF), lambda i, f, te, nu: (te[i], 0, fcol(i, f, nu))),
                  pl.BlockSpec((None, MOE_TF, D_MODEL), lambda i, f, te, nu: (te[i], fcol(i, f, nu), 0)),
                  pl.BlockSpec((MOE_TM, 1), lambda i, f, te, nu: (i, 0))],
        out_specs=pl.BlockSpec((MOE_TM, D_MODEL), lambda i, f, te, nu: (i, 0)),
        scratch_shapes=[pltpu.VMEM((MOE_TM, D_MODEL), F32)],
    )
    return pl.pallas_call(
        _moe_kernel,
        out_shape=jax.ShapeDtypeStruct((Tp, D_MODEL), F32),
        grid_spec=grid_spec,
        compiler_params=_cparams("arbitrary", "arbitrary"),
        name="moe_grouped",
    )(tile_expert, n_used, xs, w_gate.astype(BF16), w_up.astype(BF16), w_down.astype(BF16), cw)


def _add_ln_kernel(x_ref, ya_ref, yb_ref, g_ref, b_ref, o_ref, ob_ref):
    out = _layer_norm_rows(ALPHA * x_ref[...] + (ya_ref[...] + yb_ref[...]), g_ref[...], b_ref[...])
    o_ref[...] = out
    ob_ref[...] = out.astype(BF16)


def _add_ln(x, y_pairs, gamma, beta):
    T = x.shape[0]
    tm = min(512, T)
    fixed = lambda i: (0, 0)
    return pl.pallas_call(
        _add_ln_kernel,
        out_shape=(jax.ShapeDtypeStruct((T, D_MODEL), F32), jax.ShapeDtypeStruct((T, D_MODEL), BF16)),
        grid=(T // tm,),
        in_specs=[pl.BlockSpec((tm, D_MODEL), lambda i: (i, 0)),
                  pl.BlockSpec((tm, D_MODEL), lambda i: (i, 0)),
                  pl.BlockSpec((tm, D_MODEL), lambda i: (i, 1)),
                  pl.BlockSpec((1, D_MODEL), fixed), pl.BlockSpec((1, D_MODEL), fixed)],
        out_specs=(pl.BlockSpec((tm, D_MODEL), lambda i: (i, 0)), pl.BlockSpec((tm, D_MODEL), lambda i: (i, 0))),
        compiler_params=_cparams("parallel"),
        name="add_ln",
    )(x, y_pairs, y_pairs, gamma.reshape(1, -1), beta.reshape(1, -1))


def _moe_ln(x, xb, w_router, w_gate, w_up, w_down, gamma, beta):
    T = x.shape[0]
    logits = _router_logits(x, w_router)
    top_val, top_idx = lax.top_k(logits, TOP_K)
    top_w = jax.nn.softmax(top_val, axis=-1)

    e_flat = top_idx.reshape(-1).astype(jnp.int32)
    onehot = (e_flat[:, None] == jnp.arange(N_EXPERTS, dtype=jnp.int32)[None, :]).astype(jnp.int32)
    rank = jnp.sum((jnp.cumsum(onehot, axis=0) - onehot) * onehot, axis=1)
    counts = jnp.sum(onehot, axis=0)
    padded = ((counts + MOE_TM - 1) // MOE_TM) * MOE_TM
    ends = jnp.cumsum(padded)
    starts = ends - padded
    dest = starts[e_flat] + rank

    Tp = TOP_K * T + N_EXPERTS * MOE_TM
    n_tiles = Tp // MOE_TM
    token = jnp.arange(TOP_K * T, dtype=jnp.int32) // TOP_K
    src = jnp.zeros((Tp,), jnp.int32).at[dest].set(token)
    cw = jnp.zeros((Tp,), F32).at[dest].set(top_w.reshape(-1)).reshape(Tp, 1)
    n_used = (ends[-1] // MOE_TM).astype(jnp.int32).reshape(1)
    tile_start = jnp.arange(n_tiles, dtype=jnp.int32) * MOE_TM
    tile_expert = jnp.sum((tile_start[:, None] >= ends[None, :]).astype(jnp.int32), axis=1)
    last_expert = jnp.sum((jnp.maximum(ends[-1] - 1, 0) >= ends).astype(jnp.int32))
    tile_expert = jnp.minimum(tile_expert, last_expert).astype(jnp.int32)

    xs = jnp.take(xb, src, axis=0)
    ys = _moe_grouped(xs, cw, tile_expert, n_used, w_gate, w_up, w_down)
    y_pairs = jnp.take(ys, dest, axis=0).reshape(T, TOP_K * D_MODEL)
    return _add_ln(x, y_pairs, gamma, beta)


def _memattn_kernel(q_ref, k_ref, v_ref, o_ref):
    q = q_ref[0]
    outs = []
    for h in range(MEM_HEADS):
        sl = slice(h * MEM_HEAD_DIM, (h + 1) * MEM_HEAD_DIM)
        s = _dot_nt(q[:, sl].astype(BF16), k_ref[0, :, sl]) * (MEM_HEAD_DIM ** -0.5)
        e = jnp.exp(s - jnp.max(s, axis=-1, keepdims=True))
        p = e / jnp.sum(e, axis=-1, keepdims=True)
        outs.append(_dot(p.astype(BF16), v_ref[0, :, sl]))
    o_ref[0] = jnp.concatenate(outs, axis=-1).astype(o_ref.dtype)


def _memory_attention(p3, mem_kv):
    B, S, _ = p3.shape
    M = mem_kv.shape[1]
    ts = min(512, S)
    return pl.pallas_call(
        _memattn_kernel,
        out_shape=jax.ShapeDtypeStruct((B, S, MEM_WIDTH), BF16),
        grid=(B, S // ts),
        in_specs=[pl.BlockSpec((1, ts, MEM_WIDTH), lambda b, s: (b, s, Q_MEM_OFF // MEM_WIDTH)),
                  pl.BlockSpec((1, M, MEM_WIDTH), lambda b, s: (b, 0, 0)),
                  pl.BlockSpec((1, M, MEM_WIDTH), lambda b, s: (b, 0, 1))],
        out_specs=pl.BlockSpec((1, ts, MEM_WIDTH), lambda b, s: (b, s, 0)),
        compiler_params=_cparams("parallel", "parallel"),
        name="mem_attn",
    )(p3, mem_kv, mem_kv)


def _ret_kernel(q_ref, k_ref, v_ref, g_ref, cos_ref, sin_ref, intra_ref, qd_ref, kd_ref, cd_ref,
                gn_ref, o_ref, state_ref):
    @pl.when(pl.program_id(2) == 0)
    def _():
        state_ref[...] = jnp.zeros_like(state_ref)

    cosf = cos_ref[0]
    sinf = sin_ref[0]
    q = q_ref[0]
    k = k_ref[0]
    half = RET_DK // 2
    qr = q * cosf + pltpu.roll(q, half, 1) * sinf
    kr = (k * cosf + pltpu.roll(k, half, 1) * sinf) * (RET_DK ** -0.5)
    vb = v_ref[0].astype(BF16)
    state = state_ref[...]

    s = _dot_nt(qr.astype(BF16), kr.astype(BF16)) * intra_ref[0]
    o = _dot(s.astype(BF16), vb) + _dot((qr * qd_ref[0]).astype(BF16), state.astype(BF16))
    state_ref[...] = state * cd_ref[0] + _dot_tn((kr * kd_ref[0]).astype(BF16), vb)

    mu = jnp.mean(o, axis=-1, keepdims=True)
    oc = o - mu
    var = jnp.mean(oc * oc, axis=-1, keepdims=True)
    y = oc * lax.rsqrt(var + LN_EPS) * gn_ref[0:1, :] + gn_ref[1:2, :]
    g = g_ref[0]
    o_ref[0] = (y * (g * jax.nn.sigmoid(g))).astype(o_ref.dtype)


def _retention_mixer(p3, positions, gn):
    B, S, _ = p3.shape
    C = RET_CHUNK
    H = RET_HEADS
    inv = 1.0 / (ROPE_BASE ** (jnp.arange(0, RET_DK, 2, dtype=F32) / RET_DK))
    ang = positions.astype(F32)[..., None] * inv
    cos, sin = jnp.cos(ang), jnp.sin(ang)
    cosf = jnp.concatenate([cos, cos], axis=-1)
    sinf = jnp.concatenate([-sin, sin], axis=-1)

    log_g = jnp.log1p(-jnp.exp2(-5.0 - jnp.arange(H, dtype=F32)))
    idx = jnp.arange(C, dtype=F32)
    diff = idx[:, None] - idx[None, :]
    intra = jnp.where(diff >= 0, jnp.exp(jnp.maximum(diff, 0.0)[None] * log_g[:, None, None]), 0.0)
    qd = jnp.broadcast_to(jnp.exp((idx + 1.0)[None, :, None] * log_g[:, None, None]), (H, C, RET_DK))
    kd = jnp.broadcast_to(jnp.exp((C - 1.0 - idx)[None, :, None] * log_g[:, None, None]), (H, C, RET_DK))
    cd = jnp.broadcast_to(jnp.exp(C * log_g)[:, None, None], (H, 1, RET_DV))

    qk_blocks = (H * RET_DK) // RET_DK
    v_block0 = (2 * H * RET_DK) // RET_DV
    g_block0 = v_block0 + H
    return pl.pallas_call(
        _ret_kernel,
        out_shape=jax.ShapeDtypeStruct((B, S, SEQ_WIDTH), BF16),
        grid=(B, H, S // C),
        in_specs=[pl.BlockSpec((1, C, RET_DK), lambda b, h, c: (b, c, h)),
                  pl.BlockSpec((1, C, RET_DK), lambda b, h, c: (b, c, qk_blocks + h)),
                  pl.BlockSpec((1, C, RET_DV), lambda b, h, c: (b, c, v_block0 + h)),
                  pl.BlockSpec((1, C, RET_DV), lambda b, h, c: (b, c, g_block0 + h)),
                  pl.BlockSpec((1, C, RET_DK), lambda b, h, c: (b, c, 0)),
                  pl.BlockSpec((1, C, RET_DK), lambda b, h, c: (b, c, 0)),
                  pl.BlockSpec((1, C, C), lambda b, h, c: (h, 0, 0)),
                  pl.BlockSpec((1, C, RET_DK), lambda b, h, c: (h, 0, 0)),
                  pl.BlockSpec((1, C, RET_DK), lambda b, h, c: (h, 0, 0)),
                  pl.BlockSpec((1, 1, RET_DV), lambda b, h, c: (h, 0, 0)),
                  pl.BlockSpec((2, RET_DV), lambda b, h, c: (0, h))],
        out_specs=pl.BlockSpec((1, C, RET_DV), lambda b, h, c: (b, c, h)),
        scratch_shapes=[pltpu.VMEM((RET_DK, RET_DV), F32)],
        compiler_params=_cparams("parallel", "parallel", "arbitrary"),
        name="retention",
    )(p3, p3, p3, p3, cosf, sinf, intra, qd, kd, cd, gn)


def _split_bf16(x):
    hi = x.astype(BF16)
    lo = (x - hi.astype(F32)).astype(BF16)
    return hi, lo


def _dg3(a, b, dims):
    ah, al = _split_bf16(a)
    bh, bl = _split_bf16(b)
    dg = lambda u, w: lax.dot_general(u, w, (dims, ((), ())), preferred_element_type=F32)
    return dg(ah, bh) + dg(ah, bl) + dg(al, bh)


_NN = ((1,), (0,))
_NT = ((1,), (1,))
_TN = ((0,), (0,))


def _rwkv_chunk(r, c, lw, k, v, kk, a, state):
    L = r.shape[0]
    lane = lax.broadcasted_iota(jnp.int32, (L, 2 * RWKV_HEAD), 1)
    m1 = (lane < RWKV_HEAD).astype(F32)
    m2 = 1.0 - m1
    stack = lambda x: jnp.concatenate([x * m1, x * m2], axis=0)

    p_incl = jnp.exp(c)
    p_excl = jnp.exp(c - lw)
    p_inv = jnp.exp(-c)
    at = -kk * p_excl
    bt = (kk * a) * p_inv
    kt = k * p_inv
    rt = r * p_incl

    lhs = jnp.concatenate([stack(at), stack(rt)], axis=0)
    rhs = jnp.concatenate([stack(bt), stack(kt)], axis=0)
    sc = _dg3(lhs, rhs, _NT)
    n2 = 2 * L
    row = lax.broadcasted_iota(jnp.int32, (n2, n2), 0) % L
    col = lax.broadcasted_iota(jnp.int32, (n2, n2), 1) % L
    strict = (row > col).astype(F32)
    incl = (row >= col).astype(F32)
    nmat = sc[:n2, :n2] * strict
    m_ak = sc[:n2, n2:] * strict
    a_rb = sc[n2:, :n2] * incl
    a_rk = sc[n2:, n2:] * incl

    eye = (lax.broadcasted_iota(jnp.int32, (n2, n2), 0)
           == lax.broadcasted_iota(jnp.int32, (n2, n2), 1)).astype(F32)
    tinv = eye + nmat
    pw = nmat
    steps = int(np.log2(L)) - 1
    for _ in range(steps):
        pw = _dg3(pw, pw, _NN)
        tinv = tinv + _dg3(tinv, pw, _NN)

    vs = stack(v)
    sg = _dg3(jnp.concatenate([at, rt], axis=0), state, _NT)
    u = _dg3(tinv, stack(sg[:L]) + _dg3(m_ak, vs, _NN), _NN)
    uv = jnp.concatenate([u, vs], axis=0)
    ys = _dg3(jnp.concatenate([a_rb, a_rk], axis=1), uv, _NN)
    y = sg[L:] + ys[:L] + ys[L:]
    new_state = (state + _dg3(uv, rhs, _TN)) * p_incl[L - 1:L, :]
    return y, new_state


def _rwkv_kernel(r_ref, c_ref, lw_ref, k_ref, v_ref, kk_ref, a_ref, y_ref, state_ref):
    @pl.when(pl.program_id(2) == 0)
    def _():
        state_ref[...] = jnp.zeros_like(state_ref)

    y, new_state = _rwkv_chunk(r_ref[0], c_ref[0], lw_ref[0], k_ref[0], v_ref[0], kk_ref[0], a_ref[0],
                               state_ref[...])
    y_ref[0] = y
    state_ref[...] = new_state


def _rwkv_scan(r, c, lw, k, v, kk, a):
    B, S, W = r.shape
    L = RWKV_CHUNK
    pw = 2 * RWKV_HEAD
    spec = pl.BlockSpec((1, L, pw), lambda b, h, t: (b, t, h))
    return pl.pallas_call(
        _rwkv_kernel,
        out_shape=jax.ShapeDtypeStruct((B, S, W), F32),
        grid=(B, W // pw, S // L),
        in_specs=[spec] * 7,
        out_specs=spec,
        scratch_shapes=[pltpu.VMEM((pw, pw), F32)],
        compiler_params=_cparams("parallel", "parallel", "arbitrary"),
        name="rwkv_scan",
    )(r, c, lw, k, v, kk, a)


def _rwkv_layout(w_in, mu, with_v):
    sizes = [SEQ_WIDTH] * 3 + [LORA_W, LORA_A] + ([LORA_V] if with_v else []) + [LORA_G]
    offs = np.concatenate([[0], np.cumsum(sizes)])
    seq_cols = int(offs[-1])
    g_idx = len(sizes) - 1
    pad = LORA_PACK - LORA_W - LORA_A - (LORA_V if with_v else 0)
    cols = [w_in[:, :3 * SEQ_WIDTH], w_in[:, seq_cols:], w_in[:, offs[g_idx]:offs[g_idx + 1]],
            w_in[:, offs[3]:offs[g_idx]], jnp.zeros((w_in.shape[0], pad), w_in.dtype)]
    mus = [mu[:3 * SEQ_WIDTH], jnp.zeros((MEM_WIDTH,), mu.dtype), mu[offs[g_idx]:offs[g_idx + 1]],
           mu[offs[3]:offs[g_idx]], jnp.zeros((pad,), mu.dtype)]
    return jnp.concatenate(cols, axis=1), jnp.concatenate(mus)


def _rwkv_mixer(p3, mu, lora_w, lora_a, lora_g, lora_v, vecs, v_first):
    B, S, N = p3.shape
    T = B * S
    prev = jnp.pad(p3, ((0, 0), (1, 0), (0, 0)))[:, :-1]
    pm = p3 + mu * (prev - p3)
    r = pm[..., :SEQ_WIDTH]
    k = pm[..., SEQ_WIDTH:2 * SEQ_WIDTH]
    v = pm[..., 2 * SEQ_WIDTH:3 * SEQ_WIDTH]
    g_off = Q_MEM_OFF + MEM_WIDTH
    xg = pm[..., g_off:g_off + LORA_G]
    l_off = g_off + LORA_G
    xl = pm[..., l_off:l_off + LORA_PACK]

    with_v = lora_v is not None
    if with_v:
        w0, a0, v0, k_k, k_a, r_k, gn_g, gn_b = vecs
    else:
        w0, a0, k_k, k_a, r_k, gn_g, gn_b = vecs

    lane = jnp.arange(LORA_PACK)
    lin = jnp.where(lane < LORA_W, jnp.tanh(xl), xl)
    n_out = 3 if with_v else 2
    wl = jnp.zeros((LORA_PACK, n_out * SEQ_WIDTH), F32)
    wl = wl.at[:LORA_W, :SEQ_WIDTH].set(lora_w)
    wl = wl.at[LORA_W:LORA_W + LORA_A, SEQ_WIDTH:2 * SEQ_WIDTH].set(lora_a)
    if with_v:
        wl = wl.at[LORA_W + LORA_A:, 2 * SEQ_WIDTH:].set(lora_v)
    lo = _matmul(lin.reshape(T, LORA_PACK), wl.astype(BF16), tm=1024, tn=SEQ_WIDTH).reshape(B, S, -1)
    g = _matmul(jax.nn.sigmoid(xg).reshape(T, LORA_G), lora_g.astype(BF16), tm=1024,
                tn=SEQ_WIDTH).reshape(B, S, SEQ_WIDTH)

    if with_v:
        v = v + (v_first - v) * jax.nn.sigmoid(v0 + lo[..., 2 * SEQ_WIDTH:])
    else:
        v_first = v
    w_log = -jax.nn.softplus(-(w0 + lo[..., :SEQ_WIDTH])) - 0.5
    lw = -jnp.exp(w_log)
    a = jax.nn.sigmoid(a0 + lo[..., SEQ_WIDTH:2 * SEQ_WIDTH])

    heads = lambda t: t.reshape(B, S, RWKV_HEADS, RWKV_HEAD)
    kk = heads(k * k_k)
    kk = (kk / jnp.maximum(jnp.sqrt(jnp.sum(jnp.square(kk), -1, keepdims=True)), 1e-12)).reshape(B, S, SEQ_WIDTH)
    k = k * (1.0 + (a - 1.0) * k_a)
    c = jnp.cumsum(lw.reshape(B, S // RWKV_CHUNK, RWKV_CHUNK, SEQ_WIDTH), axis=2).reshape(B, S, SEQ_WIDTH)

    y = _rwkv_scan(r, c, lw, k, v, kk, a)

    r_h, k_h, v_h = heads(r), heads(k), heads(v)
    bonus = jnp.sum(r_h * k_h * r_k.reshape(RWKV_HEADS, RWKV_HEAD), -1, keepdims=True) * v_h
    yh = heads(y)
    m = jnp.mean(yh, -1, keepdims=True)
    var = jnp.mean(jnp.square(yh - m), -1, keepdims=True)
    yn = ((yh - m) * lax.rsqrt(var + RWKV_GN_EPS)).reshape(B, S, SEQ_WIDTH) * gn_g + gn_b
    out = (yn + bonus.reshape(B, S, SEQ_WIDTH)) * g
    return out.astype(BF16), v_first


def kernel(x, mem, positions, mem_w_kv, l0_w_in, l0_mu, l0_lora_w, l0_lora_a, l0_lora_g, l0_vecs, l0_w_out, l0_norms, l0_ffn_gate, l0_ffn_up, l0_ffn_down, l1_w_in, l1_gn, l1_w_out, l1_norms, l1_router, l1_moe_gate, l1_moe_up, l1_moe_down, l2_w_in, l2_mu, l2_lora_w, l2_lora_a, l2_lora_v, l2_lora_g, l2_vecs, l2_w_out, l2_norms, l2_ffn_gate, l2_ffn_up, l2_ffn_down, l3_w_in, l3_gn, l3_w_out, l3_norms, l3_router, l3_moe_gate, l3_moe_up, l3_moe_down):
    B, S, D = x.shape
    T = B * S
    M = mem.shape[1]
    mem_kv = _matmul(mem.reshape(B * M, D), mem_w_kv.astype(BF16), tm=1024, tn=512,
                     out_dtype=BF16).reshape(B, M, 2 * MEM_WIDTH)

    rwkv_layers = {
        0: (l0_w_in, l0_mu, l0_lora_w, l0_lora_a, l0_lora_g, None, l0_vecs),
        2: (l2_w_in, l2_mu, l2_lora_w, l2_lora_a, l2_lora_g, l2_lora_v, l2_vecs),
    }
    ret_layers = {1: (l1_w_in, l1_gn), 3: (l3_w_in, l3_gn)}
    w_outs = (l0_w_out, l1_w_out, l2_w_out, l3_w_out)
    norms = (l0_norms, l1_norms, l2_norms, l3_norms)
    dense = {0: (l0_ffn_gate, l0_ffn_up, l0_ffn_down), 2: (l2_ffn_gate, l2_ffn_up, l2_ffn_down)}
    moe = {1: (l1_router, l1_moe_gate, l1_moe_up, l1_moe_down),
           3: (l3_router, l3_moe_gate, l3_moe_up, l3_moe_down)}

    xf = x.reshape(T, D)
    xin = xf
    v_first = None
    for i in range(DEPTH):
        if i in rwkv_layers:
            w_in, mu, lw_, la_, lg_, lv_, vecs = rwkv_layers[i]
            w_in, mu = _rwkv_layout(w_in, mu, lv_ is not None)
            p3 = _matmul(xin, w_in.astype(BF16), tm=1024, tn=512).reshape(B, S, -1)
            y_seq, v_first = _rwkv_mixer(p3, mu, lw_, la_, lg_, lv_, vecs, v_first)
        else:
            w_in, gn = ret_layers[i]
            p3 = _matmul(xin, w_in.astype(BF16), tm=1024, tn=512).reshape(B, S, -1)
            y_seq = _retention_mixer(p3, positions, gn)
        y_mem = _memory_attention(p3, mem_kv)
        nrm = norms[i]
        xf, xb = _proj_ln(y_seq.reshape(T, SEQ_WIDTH), y_mem.reshape(T, MEM_WIDTH), w_outs[i], xf,
                          nrm[0], nrm[1])
        if i in dense:
            xf, xb = _ffn_ln(xf, xb, *dense[i], nrm[2], nrm[3])
        else:
            xf, xb = _moe_ln(xf, xb, *moe[i], nrm[2], nrm[3])
        xin = xb
    return xf.reshape(B, S, D)
```

```python
import functools

import numpy as np
import jax
import jax.numpy as jnp
from jax import lax
from jax.experimental import pallas as pl
from jax.experimental.pallas import tpu as pltpu

F32 = jnp.float32
BF16 = jnp.bfloat16

D_MODEL = 2048
DEPTH = 4
MEM_HEADS = 4
MEM_HEAD_DIM = 128
MEM_WIDTH = MEM_HEADS * MEM_HEAD_DIM
SEQ_WIDTH = D_MODEL - MEM_WIDTH

RWKV_HEAD = 64
RWKV_HEADS = SEQ_WIDTH // RWKV_HEAD
LORA_W = 96
LORA_A = 96
LORA_V = 64
LORA_G = 256
LORA_PACK = 256
RWKV_GN_EPS = 1e-5 * RWKV_HEAD
RWKV_CHUNK = 64

RET_HEADS = 6
RET_DV = SEQ_WIDTH // RET_HEADS
RET_DK = RET_DV // 2
RET_CHUNK = 128
ROPE_BASE = 10000.0

N_EXPERTS = 8
TOP_K = 2

ALPHA = (2.0 * DEPTH) ** 0.25
LN_EPS = 1e-5

LANES = 128
Q_MEM_OFF = 3 * SEQ_WIDTH
VMEM_LIMIT = 52 * 1024 * 1024


def _cparams(*semantics):
    return pltpu.CompilerParams(dimension_semantics=semantics, vmem_limit_bytes=VMEM_LIMIT)


def _dot(a, b):
    return jnp.dot(a, b, preferred_element_type=F32)


def _dot_nt(a, b):
    return lax.dot_general(a, b, (((1,), (1,)), ((), ())), preferred_element_type=F32)


def _dot_tn(a, b):
    return lax.dot_general(a, b, (((0,), (0,)), ((), ())), preferred_element_type=F32)


def _layer_norm_rows(z, g, b):
    mu = jnp.mean(z, axis=-1, keepdims=True)
    zc = z - mu
    var = jnp.mean(zc * zc, axis=-1, keepdims=True)
    return zc * lax.rsqrt(var + LN_EPS) * g + b


def _mm_kernel(a_ref, w_ref, o_ref, abf_ref):
    @pl.when(pl.program_id(1) == 0)
    def _():
        abf_ref[...] = a_ref[...].astype(BF16)

    o_ref[...] = _dot(abf_ref[...], w_ref[...]).astype(o_ref.dtype)


def _matmul(a, w, *, tm, tn, out_dtype=F32):
    M, K = a.shape
    N = w.shape[1]
    tm = min(tm, M)
    assert M % tm == 0 and N % tn == 0
    return pl.pallas_call(
        _mm_kernel,
        out_shape=jax.ShapeDtypeStruct((M, N), out_dtype),
        grid=(M // tm, N // tn),
        in_specs=[pl.BlockSpec((tm, K), lambda i, j: (i, 0)),
                  pl.BlockSpec((K, tn), lambda i, j: (0, j))],
        out_specs=pl.BlockSpec((tm, tn), lambda i, j: (i, j)),
        scratch_shapes=[pltpu.VMEM((tm, K), BF16)],
        compiler_params=_cparams("parallel", "arbitrary"),
        name="matmul",
    )(a, w)


def _router_kernel(x_ref, w_ref, o_ref):
    o_ref[...] = jnp.dot(x_ref[...], w_ref[...], preferred_element_type=F32,
                         precision=lax.Precision.HIGHEST)


def _router_logits(x, w_router):
    T = x.shape[0]
    tm = min(512, T)
    wp = jnp.pad(w_router, ((0, 0), (0, LANES - N_EXPERTS)))
    out = pl.pallas_call(
        _router_kernel,
        out_shape=jax.ShapeDtypeStruct((T, LANES), F32),
        grid=(T // tm,),
        in_specs=[pl.BlockSpec((tm, D_MODEL), lambda i: (i, 0)),
                  pl.BlockSpec((D_MODEL, LANES), lambda i: (0, 0))],
        out_specs=pl.BlockSpec((tm, LANES), lambda i: (i, 0)),
        compiler_params=_cparams("parallel"),
        name="router",
    )(x, wp)
    return out[:, :N_EXPERTS]


def _proj_ln_kernel(ys_ref, ym_ref, ws_ref, wm_ref, x_ref, g_ref, b_ref, o_ref, ob_ref):
    h = _dot(ys_ref[...], ws_ref[...]) + _dot(ym_ref[...], wm_ref[...])
    out = _layer_norm_rows(ALPHA * x_ref[...] + h, g_ref[...], b_ref[...])
    o_ref[...] = out
    ob_ref[...] = out.astype(BF16)


def _proj_ln(y_seq, y_mem, w_out, x, gamma, beta):
    T = x.shape[0]
    tm = min(256, T)
    ws = w_out[:SEQ_WIDTH].astype(BF16)
    wm = w_out[SEQ_WIDTH:].astype(BF16)
    row = lambda i: (i, 0)
    fixed = lambda i: (0, 0)
    return pl.pallas_call(
        _proj_ln_kernel,
        out_shape=(jax.ShapeDtypeStruct((T, D_MODEL), F32), jax.ShapeDtypeStruct((T, D_MODEL), BF16)),
        grid=(T // tm,),
        in_specs=[pl.BlockSpec((tm, SEQ_WIDTH), row), pl.BlockSpec((tm, MEM_WIDTH), row),
                  pl.BlockSpec((SEQ_WIDTH, D_MODEL), fixed), pl.BlockSpec((MEM_WIDTH, D_MODEL), fixed),
                  pl.BlockSpec((tm, D_MODEL), row),
                  pl.BlockSpec((1, D_MODEL), fixed), pl.BlockSpec((1, D_MODEL), fixed)],
        out_specs=(pl.BlockSpec((tm, D_MODEL), row), pl.BlockSpec((tm, D_MODEL), row)),
        compiler_params=_cparams("parallel"),
        name="proj_ln",
    )(y_seq, y_mem, ws, wm, x, gamma.reshape(1, -1), beta.reshape(1, -1))


def _swiglu_partial(xb, wg, wu, wd):
    gate = _dot(xb, wg)
    up = _dot(xb, wu)
    h = (gate * jax.nn.sigmoid(gate) * up).astype(BF16)
    return _dot(h, wd)


def _ffn_ln_kernel(xb_ref, wg_ref, wu_ref, wd_ref, x_ref, g_ref, b_ref, o_ref, ob_ref, acc_ref):
    f = pl.program_id(1)

    @pl.when(f == 0)
    def _():
        acc_ref[...] = jnp.zeros_like(acc_ref)

    acc_ref[...] += _swiglu_partial(xb_ref[...], wg_ref[...], wu_ref[...], wd_ref[...])

    @pl.when(f == pl.num_programs(1) - 1)
    def _():
        out = _layer_norm_rows(ALPHA * x_ref[...] + acc_ref[...], g_ref[...], b_ref[...])
        o_ref[...] = out
        ob_ref[...] = out.astype(BF16)


def _ffn_ln(x, xb, w_gate, w_up, w_down, gamma, beta):
    T = x.shape[0]
    F = w_gate.shape[1]
    tm = min(512, T)
    tf = 512
    row = lambda i, f: (i, 0)
    fixed = lambda i, f: (0, 0)
    return pl.pallas_call(
        _ffn_ln_kernel,
        out_shape=(jax.ShapeDtypeStruct((T, D_MODEL), F32), jax.ShapeDtypeStruct((T, D_MODEL), BF16)),
        grid=(T // tm, F // tf),
        in_specs=[pl.BlockSpec((tm, D_MODEL), row),
                  pl.BlockSpec((D_MODEL, tf), lambda i, f: (0, f)),
                  pl.BlockSpec((D_MODEL, tf), lambda i, f: (0, f)),
                  pl.BlockSpec((tf, D_MODEL), lambda i, f: (f, 0)),
                  pl.BlockSpec((tm, D_MODEL), row),
                  pl.BlockSpec((1, D_MODEL), fixed), pl.BlockSpec((1, D_MODEL), fixed)],
        out_specs=(pl.BlockSpec((tm, D_MODEL), row), pl.BlockSpec((tm, D_MODEL), row)),
        scratch_shapes=[pltpu.VMEM((tm, D_MODEL), F32)],
        compiler_params=_cparams("parallel", "arbitrary"),
        name="ffn_ln",
    )(xb, w_gate.astype(BF16), w_up.astype(BF16), w_down.astype(BF16), x,
      gamma.reshape(1, -1), beta.reshape(1, -1))


MOE_TM = 512
MOE_TF = 512


def _moe_kernel(te_ref, nu_ref, xs_ref, wg_ref, wu_ref, wd_ref, cw_ref, o_ref, acc_ref):
    i = pl.program_id(0)
    f = pl.program_id(1)
    last = pl.num_programs(1) - 1
    used = i < nu_ref[0]

    @pl.when(jnp.logical_and(used, f == 0))
    def _():
        acc_ref[...] = jnp.zeros_like(acc_ref)

    @pl.when(used)
    def _():
        acc_ref[...] += _swiglu_partial(xs_ref[...], wg_ref[...], wu_ref[...], wd_ref[...])

    @pl.when(jnp.logical_and(used, f == last))
    def _():
        o_ref[...] = acc_ref[...] * cw_ref[...]

    @pl.when(jnp.logical_and(jnp.logical_not(used), f == last))
    def _():
        o_ref[...] = jnp.zeros_like(o_ref)


def _moe_grouped(xs, cw, tile_expert, n_used, w_gate, w_up, w_down):
    Tp = xs.shape[0]
    F = w_gate.shape[2]
    nf = F // MOE_TF

    def fcol(i, f, nu):
        return jnp.where(i < nu[0], f, nf - 1)

    grid_spec = pltpu.PrefetchScalarGridSpec(
        num_scalar_prefetch=2,
        grid=(Tp // MOE_TM, nf),
        in_specs=[pl.BlockSpec((MOE_TM, D_MODEL), lambda i, f, te, nu: (i, 0)),
                  pl.BlockSpec((None, D_MODEL, MOE_TF), lambda i, f, te, nu: (te[i], 0, fcol(i, f, nu))),
                  pl.BlockSpec((None, D_MODEL, MOE_TF), lambda i, f, te, nu: (te[i], 0, fcol(i, f, nu))),
                  pl.BlockSpec((None, MOE_TF, D_MODEL), lambda i, f, te, nu: (te[i], fcol(i, f, nu), 0)),
                  pl.BlockSpec((MOE_TM, 1), lambda i, f, te, nu: (i, 0))],
        out_specs=pl.BlockSpec((MOE_TM, D_MODEL), lambda i, f, te, nu: (i, 0)),
        scratch_shapes=[pltpu.VMEM((MOE_TM, D_MODEL), F32)],
    )
    return pl.pallas_call(
        _moe_kernel,
        out_shape=jax.ShapeDtypeStruct((Tp, D_MODEL), F32),
        grid_spec=grid_spec,
        compiler_params=_cparams("arbitrary", "arbitrary"),
        name="moe_grouped",
    )(tile_expert, n_used, xs, w_gate.astype(BF16), w_up.astype(BF16), w_down.astype(BF16), cw)


def _add_ln_kernel(x_ref, ya_ref, yb_ref, g_ref, b_ref, o_ref, ob_ref):
    out = _layer_norm_rows(ALPHA * x_ref[...] + (ya_ref[...] + yb_ref[...]), g_ref[...], b_ref[...])
    o_ref[...] = out
    ob_ref[...] = out.astype(BF16)


def _add_ln(x, y_pairs, gamma, beta):
    T = x.shape[0]
    tm = min(512, T)
    fixed = lambda i: (0, 0)
    return pl.pallas_call(
        _add_ln_kernel,
        out_shape=(jax.ShapeDtypeStruct((T, D_MODEL), F32), jax.ShapeDtypeStruct((T, D_MODEL), BF16)),
        grid=(T // tm,),
        in_specs=[pl.BlockSpec((tm, D_MODEL), lambda i: (i, 0)),
                  pl.BlockSpec((tm, D_MODEL), lambda i: (i, 0)),
                  pl.BlockSpec((tm, D_MODEL), lambda i: (i, 1)),
                  pl.BlockSpec((1, D_MODEL), fixed), pl.BlockSpec((1, D_MODEL), fixed)],
        out_specs=(pl.BlockSpec((tm, D_MODEL), lambda i: (i, 0)), pl.BlockSpec((tm, D_MODEL), lambda i: (i, 0))),
        compiler_params=_cparams("parallel"),
        name="add_ln",
    )(x, y_pairs, y_pairs, gamma.reshape(1, -1), beta.reshape(1, -1))


def _moe_ln(x, xb, w_router, w_gate, w_up, w_down, gamma, beta):
    T = x.shape[0]
    logits = _router_logits(x, w_router)
    top_val, top_idx = lax.top_k(logits, TOP_K)
    top_w = jax.nn.softmax(top_val, axis=-1)

    e_flat = top_idx.reshape(-1).astype(jnp.int32)
    onehot = (e_flat[:, None] == jnp.arange(N_EXPERTS, dtype=jnp.int32)[None, :]).astype(jnp.int32)
    rank = jnp.sum((jnp.cumsum(onehot, axis=0) - onehot) * onehot, axis=1)
    counts = jnp.sum(onehot, axis=0)
    padded = ((counts + MOE_TM - 1) // MOE_TM) * MOE_TM
    ends = jnp.cumsum(padded)
    starts = ends - padded
    dest = starts[e_flat] + rank

    Tp = TOP_K * T + N_EXPERTS * MOE_TM
    n_tiles = Tp // MOE_TM
    token = jnp.arange(TOP_K * T, dtype=jnp.int32) // TOP_K
    src = jnp.zeros((Tp,), jnp.int32).at[dest].set(token)
    cw = jnp.zeros((Tp,), F32).at[dest].set(top_w.reshape(-1)).reshape(Tp, 1)
    n_used = (ends[-1] // MOE_TM).astype(jnp.int32).reshape(1)
    tile_start = jnp.arange(n_tiles, dtype=jnp.int32) * MOE_TM
    tile_expert = jnp.sum((tile_start[:, None] >= ends[None, :]).astype(jnp.int32), axis=1)
    last_expert = jnp.sum((jnp.maximum(ends[-1] - 1, 0) >= ends).astype(jnp.int32))
    tile_expert = jnp.minimum(tile_expert, last_expert).astype(jnp.int32)

    xs = jnp.take(xb, src, axis=0)
    ys = _moe_grouped(xs, cw, tile_expert, n_used, w_gate, w_up, w_down)
    y_pairs = jnp.take(ys, dest, axis=0).reshape(T, TOP_K * D_MODEL)
    return _add_ln(x, y_pairs, gamma, beta)


def _memattn_kernel(q_ref, k_ref, v_ref, o_ref):
    q = q_ref[0]
    outs = []
    for h in range(MEM_HEADS):
        sl = slice(h * MEM_HEAD_DIM, (h + 1) * MEM_HEAD_DIM)
        s = _dot_nt(q[:, sl].astype(BF16), k_ref[0, :, sl]) * (MEM_HEAD_DIM ** -0.5)
        e = jnp.exp(s - jnp.max(s, axis=-1, keepdims=True))
        p = e / jnp.sum(e, axis=-1, keepdims=True)
        outs.append(_dot(p.astype(BF16), v_ref[0, :, sl]))
    o_ref[0] = jnp.concatenate(outs, axis=-1).astype(o_ref.dtype)


def _memory_attention(p3, mem_kv):
    B, S, _ = p3.shape
    M = mem_kv.shape[1]
    ts = min(512, S)
    return pl.pallas_call(
        _memattn_kernel,
        out_shape=jax.ShapeDtypeStruct((B, S, MEM_WIDTH), BF16),
        grid=(B, S // ts),
        in_specs=[pl.BlockSpec((1, ts, MEM_WIDTH), lambda b, s: (b, s, Q_MEM_OFF // MEM_WIDTH)),
                  pl.BlockSpec((1, M, MEM_WIDTH), lambda b, s: (b, 0, 0)),
                  pl.BlockSpec((1, M, MEM_WIDTH), lambda b, s: (b, 0, 1))],
        out_specs=pl.BlockSpec((1, ts, MEM_WIDTH), lambda b, s: (b, s, 0)),
        compiler_params=_cparams("parallel", "parallel"),
        name="mem_attn",
    )(p3, mem_kv, mem_kv)


def _ret_kernel(q_ref, k_ref, v_ref, g_ref, cos_ref, sin_ref, intra_ref, qd_ref, kd_ref, cd_ref,
                gn_ref, o_ref, state_ref):
    @pl.when(pl.program_id(2) == 0)
    def _():
        state_ref[...] = jnp.zeros_like(state_ref)

    cosf = cos_ref[0]
    sinf = sin_ref[0]
    q = q_ref[0]
    k = k_ref[0]
    half = RET_DK // 2
    qr = q * cosf + pltpu.roll(q, half, 1) * sinf
    kr = (k * cosf + pltpu.roll(k, half, 1) * sinf) * (RET_DK ** -0.5)
    vb = v_ref[0].astype(BF16)
    state = state_ref[...]

    s = _dot_nt(qr.astype(BF16), kr.astype(BF16)) * intra_ref[0]
    o = _dot(s.astype(BF16), vb) + _dot((qr * qd_ref[0]).astype(BF16), state.astype(BF16))
    state_ref[...] = state * cd_ref[0] + _dot_tn((kr * kd_ref[0]).astype(BF16), vb)

    mu = jnp.mean(o, axis=-1, keepdims=True)
    oc = o - mu
    var = jnp.mean(oc * oc, axis=-1, keepdims=True)
    y = oc * lax.rsqrt(var + LN_EPS) * gn_ref[0:1, :] + gn_ref[1:2, :]
    g = g_ref[0]
    o_ref[0] = (y * (g * jax.nn.sigmoid(g))).astype(o_ref.dtype)


def _retention_mixer(p3, positions, gn):
    B, S, _ = p3.shape
    C = RET_CHUNK
    H = RET_HEADS
    inv = 1.0 / (ROPE_BASE ** (jnp.arange(0, RET_DK, 2, dtype=F32) / RET_DK))
    ang = positions.astype(F32)[..., None] * inv
    cos, sin = jnp.cos(ang), jnp.sin(ang)
    cosf = jnp.concatenate([cos, cos], axis=-1)
    sinf = jnp.concatenate([-sin, sin], axis=-1)

    log_g = jnp.log1p(-jnp.exp2(-5.0 - jnp.arange(H, dtype=F32)))
    idx = jnp.arange(C, dtype=F32)
    diff = idx[:, None] - idx[None, :]
    intra = jnp.where(diff >= 0, jnp.exp(jnp.maximum(diff, 0.0)[None] * log_g[:, None, None]), 0.0)
    qd = jnp.broadcast_to(jnp.exp((idx + 1.0)[None, :, None] * log_g[:, None, None]), (H, C, RET_DK))
    kd = jnp.broadcast_to(jnp.exp((C - 1.0 - idx)[None, :, None] * log_g[:, None, None]), (H, C, RET_DK))
    cd = jnp.broadcast_to(jnp.exp(C * log_g)[:, None, None], (H, 1, RET_DV))

    qk_blocks = (H * RET_DK) // RET_DK
    v_block0 = (2 * H * RET_DK) // RET_DV
    g_block0 = v_block0 + H
    return pl.pallas_call(
        _ret_kernel,
        out_shape=jax.ShapeDtypeStruct((B, S, SEQ_WIDTH), BF16),
        grid=(B, H, S // C),
        in_specs=[pl.BlockSpec((1, C, RET_DK), lambda b, h, c: (b, c, h)),
                  pl.BlockSpec((1, C, RET_DK), lambda b, h, c: (b, c, qk_blocks + h)),
                  pl.BlockSpec((1, C, RET_DV), lambda b, h, c: (b, c, v_block0 + h)),
                  pl.BlockSpec((1, C, RET_DV), lambda b, h, c: (b, c, g_block0 + h)),
                  pl.BlockSpec((1, C, RET_DK), lambda b, h, c: (b, c, 0)),
                  pl.BlockSpec((1, C, RET_DK), lambda b, h, c: (b, c, 0)),
                  pl.BlockSpec((1, C, C), lambda b, h, c: (h, 0, 0)),
                  pl.BlockSpec((1, C, RET_DK), lambda b, h, c: (h, 0, 0)),
                  pl.BlockSpec((1, C, RET_DK), lambda b, h, c: (h, 0, 0)),
                  pl.BlockSpec((1, 1, RET_DV), lambda b, h, c: (h, 0, 0)),
                  pl.BlockSpec((2, RET_DV), lambda b, h, c: (0, h))],
        out_specs=pl.BlockSpec((1, C, RET_DV), lambda b, h, c: (b, c, h)),
        scratch_shapes=[pltpu.VMEM((RET_DK, RET_DV), F32)],
        compiler_params=_cparams("parallel", "parallel", "arbitrary"),
        name="retention",
    )(p3, p3, p3, p3, cosf, sinf, intra, qd, kd, cd, gn)


RWKV_PAIRS = 12
PAIR = 2 * RWKV_HEAD

_NN = ((1,), (0,))
_NT = ((1,), (1,))
_TN = ((0,), (0,))


def _dg(a, b, dims):
    return lax.dot_general(a.astype(BF16), b.astype(BF16), (dims, ((), ())), preferred_element_type=F32)


def _rwkv_chunk(r, c, lw, k, v, kk, a, state):
    n = len(r)
    rng = range(n)
    L = r[0].shape[0]
    lane = lax.broadcasted_iota(jnp.int32, (L, PAIR), 1)
    m1 = (lane < RWKV_HEAD).astype(F32)
    m2 = 1.0 - m1
    stack = lambda x: jnp.concatenate([x * m1, x * m2], axis=0)
    n2 = 2 * L
    row = lax.broadcasted_iota(jnp.int32, (n2, n2), 0)
    col = lax.broadcasted_iota(jnp.int32, (n2, n2), 1)
    strict = (row % L > col % L).astype(F32)
    incl = (row % L >= col % L).astype(F32)
    eye = (row == col).astype(F32)

    p_incl = [jnp.exp(c[j]) for j in rng]
    p_inv = [jnp.exp(-c[j]) for j in rng]
    at = [-kk[j] * jnp.exp(c[j] - lw[j]) for j in rng]
    rt = [r[j] * p_incl[j] for j in rng]
    lhs = [jnp.concatenate([stack(at[j]), stack(rt[j])], axis=0).astype(BF16) for j in rng]
    rhs = [jnp.concatenate([stack(kk[j] * a[j] * p_inv[j]), stack(k[j] * p_inv[j])], axis=0).astype(BF16)
           for j in rng]
    sc = [_dg(lhs[j], rhs[j], _NT) for j in rng]
    nmat = [sc[j][:n2, :n2] * strict for j in rng]
    tinv = [eye + nmat[j] for j in rng]
    pw = [nmat[j].astype(BF16) for j in rng]
    for _ in range(int(np.log2(L)) - 1):
        pw = [_dg(pw[j], pw[j], _NN).astype(BF16) for j in rng]
        tinv = [tinv[j] + _dg(tinv[j], pw[j], _NN) for j in rng]

    vs = [stack(v[j]).astype(BF16) for j in rng]
    sg = [_dg(jnp.concatenate([at[j], rt[j]], axis=0), state[j], _NT) for j in rng]
    mv = [_dg(sc[j][:n2, n2:] * strict, vs[j], _NN) for j in rng]
    u = [_dg(tinv[j], stack(sg[j][:L]) + mv[j], _NN) for j in rng]
    uv = [jnp.concatenate([u[j].astype(BF16), vs[j]], axis=0) for j in rng]
    ys = [_dg(sc[j][n2:, :] * jnp.concatenate([incl, incl], axis=1), uv[j], _NN) for j in rng]
    y = [sg[j][L:] + ys[j][:L] + ys[j][L:] for j in rng]
    new_state = [(state[j] + _dg(uv[j], rhs[j], _TN)) * p_incl[j][L - 1:L, :] for j in rng]
    return y, new_state


def _shift_mix(x, prev_block, mu, first):
    prev_row = jnp.where(first, 0.0, prev_block[7:8, :])
    rows = lax.broadcasted_iota(jnp.int32, x.shape, 0)
    prev = jnp.where(rows == 0, prev_row, pltpu.roll(x, 1, 0))
    return x + mu * (prev - x)


def _rwkv_mixer_kernel(*refs, with_v):
    it = iter(refs)
    r_ref, k_ref, v_ref, xg_ref, xl_ref = (next(it) for _ in range(5))
    rp_ref, kp_ref, vp_ref, xgp_ref, xlp_ref = (next(it) for _ in range(5))
    mu3_ref, mug_ref, mul_ref, vec_ref, ww_ref, wa_ref = (next(it) for _ in range(6))
    wv_ref = next(it) if with_v else None
    wg_ref = next(it)
    vf_ref = next(it) if with_v else None
    y_ref = next(it)
    vfo_ref = None if with_v else next(it)
    state_ref = next(it)

    first = pl.program_id(2) == 0

    @pl.when(first)
    def _():
        state_ref[...] = jnp.zeros_like(state_ref)

    L = RWKV_CHUNK
    r = _shift_mix(r_ref[0], rp_ref[0], mu3_ref[0:1, :], first)
    k = _shift_mix(k_ref[0], kp_ref[0], mu3_ref[1:2, :], first)
    v = _shift_mix(v_ref[0], vp_ref[0], mu3_ref[2:3, :], first)
    xg = _shift_mix(xg_ref[0], xgp_ref[0], mug_ref[...], first)
    xl = _shift_mix(xl_ref[0], xlp_ref[0], mul_ref[...], first)

    w0, a0, v0 = vec_ref[0:1, :], vec_ref[1:2, :], vec_ref[2:3, :]
    k_k, k_a, r_k = vec_ref[3:4, :], vec_ref[4:5, :], vec_ref[5:6, :]
    gn_g, gn_b = vec_ref[6:7, :], vec_ref[7:8, :]

    lane = lax.broadcasted_iota(jnp.int32, xl.shape, 1)
    lin = jnp.where(lane < LORA_W, jnp.tanh(xl), xl).astype(BF16)
    w_arg = w0 + _dot(lin, ww_ref[...])
    a = jax.nn.sigmoid(a0 + _dot(lin, wa_ref[...]))
    if with_v:
        v = v + (vf_ref[0] - v) * jax.nn.sigmoid(v0 + _dot(lin, wv_ref[...]))
    else:
        vfo_ref[0] = v
    g = _dot(jax.nn.sigmoid(xg).astype(BF16), wg_ref[...])

    w_log = -(jnp.maximum(-w_arg, 0.0) + jnp.log1p(jnp.exp(-jnp.abs(w_arg)))) - 0.5
    lw = -jnp.exp(w_log)
    tri = (lax.broadcasted_iota(jnp.int32, (L, L), 0) >= lax.broadcasted_iota(jnp.int32, (L, L), 1)).astype(BF16)
    h1 = lw.astype(BF16)
    rem = lw - h1.astype(F32)
    h2 = rem.astype(BF16)
    h3 = (rem - h2.astype(F32)).astype(BF16)
    c = _dot(tri, h1) + _dot(tri, h2) + _dot(tri, h3)

    kk_raw = k * k_k
    k2 = k * (1.0 + (a - 1.0) * k_a)
    ones_bd = (lax.broadcasted_iota(jnp.int32, (PAIR, PAIR), 0) // RWKV_HEAD
               == lax.broadcasted_iota(jnp.int32, (PAIR, PAIR), 1) // RWKV_HEAD).astype(BF16)
    inv_n = 1.0 / RWKV_HEAD

    rng = range(RWKV_PAIRS)
    sls = [slice(j * PAIR, (j + 1) * PAIR) for j in rng]
    part = lambda x: [x[:, sl] for sl in sls]
    kkr, rs, ks, vs = part(kk_raw), part(r), part(k2), part(v)
    red = [_dot(jnp.concatenate([kkr[j] * kkr[j], rs[j] * ks[j] * r_k[:, sls[j]]], axis=0).astype(BF16), ones_bd)
           for j in rng]
    kks = [kkr[j] / jnp.maximum(jnp.sqrt(red[j][:L]), 1e-12) for j in rng]
    ys, new_state = _rwkv_chunk(rs, part(c), part(lw), ks, vs, kks, part(a), [state_ref[j] for j in rng])
    for j in rng:
        state_ref[j] = new_state[j]
    yc = [ys[j] - _dot(ys[j].astype(BF16), ones_bd) * inv_n for j in rng]
    var = [_dot((yc[j] * yc[j]).astype(BF16), ones_bd) * inv_n for j in rng]
    for j in rng:
        yn = yc[j] * lax.rsqrt(var[j] + RWKV_GN_EPS) * gn_g[:, sls[j]] + gn_b[:, sls[j]]
        y_ref[0, :, sls[j]] = ((yn + red[j][L:] * vs[j]) * g[:, sls[j]]).astype(y_ref.dtype)


def _rwkv_layout(w_in, mu, with_v):
    sizes = [SEQ_WIDTH] * 3 + [LORA_W, LORA_A] + ([LORA_V] if with_v else []) + [LORA_G]
    offs = np.concatenate([[0], np.cumsum(sizes)])
    seq_cols = int(offs[-1])
    g_idx = len(sizes) - 1
    pad = LORA_PACK - LORA_W - LORA_A - (LORA_V if with_v else 0)
    cols = [w_in[:, :3 * SEQ_WIDTH], w_in[:, seq_cols:], w_in[:, offs[g_idx]:offs[g_idx + 1]],
            w_in[:, offs[3]:offs[g_idx]], jnp.zeros((w_in.shape[0], pad), w_in.dtype)]
    mus = [mu[:3 * SEQ_WIDTH], jnp.zeros((MEM_WIDTH,), mu.dtype), mu[offs[g_idx]:offs[g_idx + 1]],
           mu[offs[3]:offs[g_idx]], jnp.zeros((pad,), mu.dtype)]
    return jnp.concatenate(cols, axis=1), jnp.concatenate(mus)


def _rwkv_mixer(p3, mu, lora_w, lora_a, lora_g, lora_v, vecs, v_first):
    B, S, _ = p3.shape
    L = RWKV_CHUNK
    W = RWKV_PAIRS * PAIR
    with_v = lora_v is not None
    if with_v:
        w0, a0, v0, k_k, k_a, r_k, gn_g, gn_b = vecs
    else:
        w0, a0, k_k, k_a, r_k, gn_g, gn_b = vecs
        v0 = jnp.zeros_like(w0)
    vec8 = jnp.stack([w0, a0, v0, k_k, k_a, r_k, gn_g, gn_b])
    mu3 = jnp.concatenate([mu[:3 * SEQ_WIDTH].reshape(3, SEQ_WIDTH), jnp.zeros((5, SEQ_WIDTH), F32)])
    g_off = Q_MEM_OFF + MEM_WIDTH
    l_off = g_off + LORA_G
    mug = mu[g_off:g_off + LORA_G].reshape(1, LORA_G)
    mul = mu[l_off:l_off + LORA_PACK].reshape(1, LORA_PACK)
    rows = lambda w, lo: jnp.zeros((LORA_PACK, SEQ_WIDTH), F32).at[lo:lo + w.shape[0]].set(w).astype(BF16)

    def cols(width, off):
        blk = off // width
        return pl.BlockSpec((1, L, width), lambda b, g, t: (b, t, blk))

    def cols_prev(width, off):
        blk = off // width
        return pl.BlockSpec((1, 8, width), lambda b, g, t: (b, jnp.maximum(t * (L // 8) - 1, 0), blk))

    def seq(off):
        blk = off // W
        return pl.BlockSpec((1, L, W), lambda b, g, t: (b, t, blk + g))

    def seq_prev(off):
        blk = off // W
        return pl.BlockSpec((1, 8, W), lambda b, g, t: (b, jnp.maximum(t * (L // 8) - 1, 0), blk + g))

    per_group = lambda nrows: pl.BlockSpec((nrows, W), lambda b, g, t: (0, g))
    fixed = lambda ncols: pl.BlockSpec((1, ncols), lambda b, g, t: (0, 0))

    in_specs = [seq(0), seq(SEQ_WIDTH), seq(2 * SEQ_WIDTH), cols(LORA_G, g_off), cols(LORA_PACK, l_off),
                seq_prev(0), seq_prev(SEQ_WIDTH), seq_prev(2 * SEQ_WIDTH),
                cols_prev(LORA_G, g_off), cols_prev(LORA_PACK, l_off),
                per_group(8), fixed(LORA_G), fixed(LORA_PACK), per_group(8),
                per_group(LORA_PACK), per_group(LORA_PACK)]
    args = [p3] * 10 + [mu3, mug, mul, vec8, rows(lora_w, 0), rows(lora_a, LORA_W)]
    if with_v:
        in_specs.append(per_group(LORA_PACK))
        args.append(rows(lora_v, LORA_W + LORA_A))
    in_specs.append(per_group(LORA_G))
    args.append(lora_g.astype(BF16))
    out_spec = pl.BlockSpec((1, L, W), lambda b, g, t: (b, t, g))
    y_shape = jax.ShapeDtypeStruct((B, S, SEQ_WIDTH), BF16)
    if with_v:
        in_specs.append(out_spec)
        args.append(v_first)
        out_shape, out_specs = y_shape, out_spec
    else:
        out_shape = (y_shape, jax.ShapeDtypeStruct((B, S, SEQ_WIDTH), F32))
        out_specs = (out_spec, out_spec)

    res = pl.pallas_call(
        functools.partial(_rwkv_mixer_kernel, with_v=with_v),
        out_shape=out_shape,
        grid=(B, SEQ_WIDTH // W, S // L),
        in_specs=in_specs,
        out_specs=out_specs,
        scratch_shapes=[pltpu.VMEM((RWKV_PAIRS, PAIR, PAIR), F32)],
        compiler_params=_cparams("parallel", "parallel", "arbitrary"),
        name="rwkv_mixer",
    )(*args)
    return (res, v_first) if with_v else res


def kernel(x, mem, positions, mem_w_kv, l0_w_in, l0_mu, l0_lora_w, l0_lora_a, l0_lora_g, l0_vecs, l0_w_out, l0_norms, l0_ffn_gate, l0_ffn_up, l0_ffn_down, l1_w_in, l1_gn, l1_w_out, l1_norms, l1_router, l1_moe_gate, l1_moe_up, l1_moe_down, l2_w_in, l2_mu, l2_lora_w, l2_lora_a, l2_lora_v, l2_lora_g, l2_vecs, l2_w_out, l2_norms, l2_ffn_gate, l2_ffn_up, l2_ffn_down, l3_w_in, l3_gn, l3_w_out, l3_norms, l3_router, l3_moe_gate, l3_moe_up, l3_moe_down):
    B, S, D = x.shape
    T = B * S
    M = mem.shape[1]
    mem_kv = _matmul(mem.reshape(B * M, D), mem_w_kv.astype(BF16), tm=1024, tn=512,
                     out_dtype=BF16).reshape(B, M, 2 * MEM_WIDTH)

    rwkv_layers = {
        0: (l0_w_in, l0_mu, l0_lora_w, l0_lora_a, l0_lora_g, None, l0_vecs),
        2: (l2_w_in, l2_mu, l2_lora_w, l2_lora_a, l2_lora_g, l2_lora_v, l2_vecs),
    }
    ret_layers = {1: (l1_w_in, l1_gn), 3: (l3_w_in, l3_gn)}
    w_outs = (l0_w_out, l1_w_out, l2_w_out, l3_w_out)
    norms = (l0_norms, l1_norms, l2_norms, l3_norms)
    dense = {0: (l0_ffn_gate, l0_ffn_up, l0_ffn_down), 2: (l2_ffn_gate, l2_ffn_up, l2_ffn_down)}
    moe = {1: (l1_router, l1_moe_gate, l1_moe_up, l1_moe_down),
           3: (l3_router, l3_moe_gate, l3_moe_up, l3_moe_down)}

    xf = x.reshape(T, D)
    xin = xf
    v_first = None
    for i in range(DEPTH):
        if i in rwkv_layers:
            w_in, mu, lw_, la_, lg_, lv_, vecs = rwkv_layers[i]
            w_in, mu = _rwkv_layout(w_in, mu, lv_ is not None)
            p3 = _matmul(xin, w_in.astype(BF16), tm=1024, tn=512).reshape(B, S, -1)
            y_seq, v_first = _rwkv_mixer(p3, mu, lw_, la_, lg_, lv_, vecs, v_first)
        else:
            w_in, gn = ret_layers[i]
            p3 = _matmul(xin, w_in.astype(BF16), tm=1024, tn=512).reshape(B, S, -1)
            y_seq = _retention_mixer(p3, positions, gn)
        y_mem = _memory_attention(p3, mem_kv)
        nrm = norms[i]
        xf, xb = _proj_ln(y_seq.reshape(T, SEQ_WIDTH), y_mem.reshape(T, MEM_WIDTH), w_outs[i], xf,
                          nrm[0], nrm[1])
        if i in dense:
            xf, xb = _ffn_ln(xf, xb, *dense[i], nrm[2], nrm[3])
        else:
            xf, xb = _moe_ln(xf, xb, *moe[i], nrm[2], nrm[3])
        xin = xb
    return xf.reshape(B, S, D)
```

```python
import functools

import numpy as np
import jax
import jax.numpy as jnp
from jax import lax
from jax.experimental import pallas as pl
from jax.experimental.pallas import tpu as pltpu

F32 = jnp.float32
BF16 = jnp.bfloat16

D_MODEL = 2048
DEPTH = 4
MEM_HEADS = 4
MEM_HEAD_DIM = 128
MEM_WIDTH = MEM_HEADS * MEM_HEAD_DIM
SEQ_WIDTH = D_MODEL - MEM_WIDTH

RWKV_HEAD = 64
RWKV_HEADS = SEQ_WIDTH // RWKV_HEAD
LORA_W = 96
LORA_A = 96
LORA_V = 64
LORA_G = 256
LORA_PACK = 256
RWKV_GN_EPS = 1e-5 * RWKV_HEAD
RWKV_CHUNK = 64

RET_HEADS = 6
RET_DV = SEQ_WIDTH // RET_HEADS
RET_DK = RET_DV // 2
RET_CHUNK = 128
ROPE_BASE = 10000.0

N_EXPERTS = 8
TOP_K = 2

ALPHA = (2.0 * DEPTH) ** 0.25
LN_EPS = 1e-5

LANES = 128
Q_MEM_OFF = 3 * SEQ_WIDTH
VMEM_LIMIT = 52 * 1024 * 1024


def _cparams(*semantics):
    return pltpu.CompilerParams(dimension_semantics=semantics, vmem_limit_bytes=VMEM_LIMIT)


def _dot(a, b):
    return jnp.dot(a, b, preferred_element_type=F32)


def _dot_nt(a, b):
    return lax.dot_general(a, b, (((1,), (1,)), ((), ())), preferred_element_type=F32)


def _dot_tn(a, b):
    return lax.dot_general(a, b, (((0,), (0,)), ((), ())), preferred_element_type=F32)


def _layer_norm_rows(z, g, b):
    mu = jnp.mean(z, axis=-1, keepdims=True)
    zc = z - mu
    var = jnp.mean(zc * zc, axis=-1, keepdims=True)
    return zc * lax.rsqrt(var + LN_EPS) * g + b


def _mm_kernel(a_ref, w_ref, o_ref, abf_ref):
    @pl.when(pl.program_id(1) == 0)
    def _():
        abf_ref[...] = a_ref[...].astype(BF16)

    o_ref[...] = _dot(abf_ref[...], w_ref[...]).astype(o_ref.dtype)


def _matmul(a, w, *, tm, tn, out_dtype=F32):
    M, K = a.shape
    N = w.shape[1]
    tm = min(tm, M)
    assert M % tm == 0 and N % tn == 0
    return pl.pallas_call(
        _mm_kernel,
        out_shape=jax.ShapeDtypeStruct((M, N), out_dtype),
        grid=(M // tm, N // tn),
        in_specs=[pl.BlockSpec((tm, K), lambda i, j: (i, 0)),
                  pl.BlockSpec((K, tn), lambda i, j: (0, j))],
        out_specs=pl.BlockSpec((tm, tn), lambda i, j: (i, j)),
        scratch_shapes=[pltpu.VMEM((tm, K), BF16)],
        compiler_params=_cparams("parallel", "arbitrary"),
        name="matmul",
    )(a, w)


def _router_kernel(x_ref, w_ref, o_ref):
    o_ref[...] = jnp.dot(x_ref[...], w_ref[...], preferred_element_type=F32,
                         precision=lax.Precision.HIGHEST)


def _router_logits(x, w_router):
    T = x.shape[0]
    tm = min(512, T)
    wp = jnp.pad(w_router, ((0, 0), (0, LANES - N_EXPERTS)))
    out = pl.pallas_call(
        _router_kernel,
        out_shape=jax.ShapeDtypeStruct((T, LANES), F32),
        grid=(T // tm,),
        in_specs=[pl.BlockSpec((tm, D_MODEL), lambda i: (i, 0)),
                  pl.BlockSpec((D_MODEL, LANES), lambda i: (0, 0))],
        out_specs=pl.BlockSpec((tm, LANES), lambda i: (i, 0)),
        compiler_params=_cparams("parallel"),
        name="router",
    )(x, wp)
    return out[:, :N_EXPERTS]


def _proj_ln_kernel(ys_ref, ym_ref, ws_ref, wm_ref, x_ref, g_ref, b_ref, o_ref, ob_ref):
    h = _dot(ys_ref[...], ws_ref[...]) + _dot(ym_ref[...], wm_ref[...])
    out = _layer_norm_rows(ALPHA * x_ref[...] + h, g_ref[...], b_ref[...])
    o_ref[...] = out
    ob_ref[...] = out.astype(BF16)


def _proj_ln(y_seq, y_mem, w_out, x, gamma, beta):
    T = x.shape[0]
    tm = min(512, T)
    ws = w_out[:SEQ_WIDTH].astype(BF16)
    wm = w_out[SEQ_WIDTH:].astype(BF16)
    row = lambda i: (i, 0)
    fixed = lambda i: (0, 0)
    once = pl.Buffered(1)
    return pl.pallas_call(
        _proj_ln_kernel,
        out_shape=(jax.ShapeDtypeStruct((T, D_MODEL), F32), jax.ShapeDtypeStruct((T, D_MODEL), BF16)),
        grid=(T // tm,),
        in_specs=[pl.BlockSpec((tm, SEQ_WIDTH), row), pl.BlockSpec((tm, MEM_WIDTH), row),
                  pl.BlockSpec((SEQ_WIDTH, D_MODEL), fixed, pipeline_mode=once),
                  pl.BlockSpec((MEM_WIDTH, D_MODEL), fixed, pipeline_mode=once),
                  pl.BlockSpec((tm, D_MODEL), row),
                  pl.BlockSpec((1, D_MODEL), fixed), pl.BlockSpec((1, D_MODEL), fixed)],
        out_specs=(pl.BlockSpec((tm, D_MODEL), row), pl.BlockSpec((tm, D_MODEL), row)),
        compiler_params=_cparams("parallel"),
        name="proj_ln",
    )(y_seq, y_mem, ws, wm, x, gamma.reshape(1, -1), beta.reshape(1, -1))


def _swiglu_partial(xb, wg, wu, wd):
    gate = _dot(xb, wg)
    up = _dot(xb, wu)
    h = (gate * jax.nn.sigmoid(gate) * up).astype(BF16)
    return _dot(h, wd)


def _ffn_ln_kernel(xb_ref, wg_ref, wu_ref, wd_ref, x_ref, g_ref, b_ref, o_ref, ob_ref, acc_ref):
    f = pl.program_id(1)

    @pl.when(f == 0)
    def _():
        acc_ref[...] = jnp.zeros_like(acc_ref)

    acc_ref[...] += _swiglu_partial(xb_ref[...], wg_ref[...], wu_ref[...], wd_ref[...])

    @pl.when(f == pl.num_programs(1) - 1)
    def _():
        out = _layer_norm_rows(ALPHA * x_ref[...] + acc_ref[...], g_ref[...], b_ref[...])
        o_ref[...] = out
        ob_ref[...] = out.astype(BF16)


def _ffn_ln(x, xb, w_gate, w_up, w_down, gamma, beta):
    T = x.shape[0]
    F = w_gate.shape[1]
    tm = min(512, T)
    tf = 512
    row = lambda i, f: (i, 0)
    fixed = lambda i, f: (0, 0)
    return pl.pallas_call(
        _ffn_ln_kernel,
        out_shape=(jax.ShapeDtypeStruct((T, D_MODEL), F32), jax.ShapeDtypeStruct((T, D_MODEL), BF16)),
        grid=(T // tm, F // tf),
        in_specs=[pl.BlockSpec((tm, D_MODEL), row),
                  pl.BlockSpec((D_MODEL, tf), lambda i, f: (0, f)),
                  pl.BlockSpec((D_MODEL, tf), lambda i, f: (0, f)),
                  pl.BlockSpec((tf, D_MODEL), lambda i, f: (f, 0)),
                  pl.BlockSpec((tm, D_MODEL), row),
                  pl.BlockSpec((1, D_MODEL), fixed), pl.BlockSpec((1, D_MODEL), fixed)],
        out_specs=(pl.BlockSpec((tm, D_MODEL), row), pl.BlockSpec((tm, D_MODEL), row)),
        scratch_shapes=[pltpu.VMEM((tm, D_MODEL), F32)],
        compiler_params=_cparams("parallel", "arbitrary"),
        name="ffn_ln",
    )(xb, w_gate.astype(BF16), w_up.astype(BF16), w_down.astype(BF16), x,
      gamma.reshape(1, -1), beta.reshape(1, -1))


MOE_TM = 512
MOE_TF = 512


def _cast_kernel(w_ref, o_ref):
    o_ref[...] = w_ref[...].astype(o_ref.dtype)


def _to_bf16(w):
    E, R, C = w.shape
    tr = 256
    spec = pl.BlockSpec((1, tr, C), lambda e, r: (e, r, 0))
    return pl.pallas_call(
        _cast_kernel,
        out_shape=jax.ShapeDtypeStruct(w.shape, BF16),
        grid=(E, R // tr),
        in_specs=[spec],
        out_specs=spec,
        compiler_params=_cparams("parallel", "parallel"),
        name="to_bf16",
    )(w)


def _moe_kernel(te_ref, nu_ref, xs_ref, wg_ref, wu_ref, wd_ref, o_ref, acc_ref):
    i = pl.program_id(0)
    f = pl.program_id(1)
    last = pl.num_programs(1) - 1
    used = i < nu_ref[0]

    @pl.when(jnp.logical_and(used, f == 0))
    def _():
        acc_ref[...] = jnp.zeros_like(acc_ref)

    @pl.when(used)
    def _():
        acc_ref[...] += _swiglu_partial(xs_ref[...], wg_ref[...], wu_ref[...], wd_ref[...])

    @pl.when(jnp.logical_and(used, f == last))
    def _():
        o_ref[...] = acc_ref[...].astype(o_ref.dtype)

    @pl.when(jnp.logical_and(jnp.logical_not(used), f == last))
    def _():
        o_ref[...] = jnp.zeros_like(o_ref)


def _moe_grouped(xs, tile_expert, n_used, w_gate, w_up, w_down):
    Tp = xs.shape[0]
    F = w_gate.shape[2]
    nf = F // MOE_TF

    def fcol(i, f, nu):
        return jnp.where(i < nu[0], f, nf - 1)

    grid_spec = pltpu.PrefetchScalarGridSpec(
        num_scalar_prefetch=2,
        grid=(Tp // MOE_TM, nf),
        in_specs=[pl.BlockSpec((MOE_TM, D_MODEL), lambda i, f, te, nu: (i, 0)),
                  pl.BlockSpec((None, D_MODEL, MOE_TF), lambda i, f, te, nu: (te[i], 0, fcol(i, f, nu))),
                  pl.BlockSpec((None, D_MODEL, MOE_TF), lambda i, f, te, nu: (te[i], 0, fcol(i, f, nu))),
                  pl.BlockSpec((None, MOE_TF, D_MODEL), lambda i, f, te, nu: (te[i], fcol(i, f, nu), 0))],
        out_specs=pl.BlockSpec((MOE_TM, D_MODEL), lambda i, f, te, nu: (i, 0)),
        scratch_shapes=[pltpu.VMEM((MOE_TM, D_MODEL), F32)],
    )
    return pl.pallas_call(
        _moe_kernel,
        out_shape=jax.ShapeDtypeStruct((Tp, D_MODEL), BF16),
        grid_spec=grid_spec,
        compiler_params=_cparams("arbitrary", "arbitrary"),
        name="moe_grouped",
    )(tile_expert, n_used, xs, _to_bf16(w_gate), _to_bf16(w_up), _to_bf16(w_down))


def _combine_ln_kernel(x_ref, ya_ref, yb_ref, w_ref, g_ref, b_ref, o_ref, ob_ref):
    w = w_ref[...]
    f = w[:, 0:1] * ya_ref[...].astype(F32) + w[:, 1:2] * yb_ref[...].astype(F32)
    out = _layer_norm_rows(ALPHA * x_ref[...] + f, g_ref[...], b_ref[...])
    o_ref[...] = out
    ob_ref[...] = out.astype(BF16)


def _combine_ln(x, y_rows, top_w, gamma, beta):
    T = x.shape[0]
    tm = min(512, T)
    nt = T // tm
    wpad = jnp.pad(top_w, ((0, 0), (0, LANES - TOP_K)))
    fixed = lambda i: (0, 0)
    row = lambda i: (i, 0)
    return pl.pallas_call(
        _combine_ln_kernel,
        out_shape=(jax.ShapeDtypeStruct((T, D_MODEL), F32), jax.ShapeDtypeStruct((T, D_MODEL), BF16)),
        grid=(nt,),
        in_specs=[pl.BlockSpec((tm, D_MODEL), row),
                  pl.BlockSpec((tm, D_MODEL), row),
                  pl.BlockSpec((tm, D_MODEL), lambda i: (i + nt, 0)),
                  pl.BlockSpec((tm, LANES), row),
                  pl.BlockSpec((1, D_MODEL), fixed), pl.BlockSpec((1, D_MODEL), fixed)],
        out_specs=(pl.BlockSpec((tm, D_MODEL), row), pl.BlockSpec((tm, D_MODEL), row)),
        compiler_params=_cparams("parallel"),
        name="combine_ln",
    )(x, y_rows, y_rows, wpad, gamma.reshape(1, -1), beta.reshape(1, -1))


def _moe_ln(x, xb, w_router, w_gate, w_up, w_down, gamma, beta):
    T = x.shape[0]
    logits = _router_logits(x, w_router)
    top_val, top_idx = lax.top_k(logits, TOP_K)
    top_w = jax.nn.softmax(top_val, axis=-1)

    e_flat = top_idx.reshape(-1).astype(jnp.int32)
    onehot = (e_flat[:, None] == jnp.arange(N_EXPERTS, dtype=jnp.int32)[None, :]).astype(jnp.int32)
    rank = jnp.sum((jnp.cumsum(onehot, axis=0) - onehot) * onehot, axis=1)
    counts = jnp.sum(onehot, axis=0)
    padded = ((counts + MOE_TM - 1) // MOE_TM) * MOE_TM
    ends = jnp.cumsum(padded)
    starts = ends - padded
    dest = starts[e_flat] + rank

    Tp = TOP_K * T + N_EXPERTS * MOE_TM
    n_tiles = Tp // MOE_TM
    token = jnp.arange(TOP_K * T, dtype=jnp.int32) // TOP_K
    src = jnp.zeros((Tp,), jnp.int32).at[dest].set(token)
    n_used = (ends[-1] // MOE_TM).astype(jnp.int32).reshape(1)
    tile_start = jnp.arange(n_tiles, dtype=jnp.int32) * MOE_TM
    tile_expert = jnp.sum((tile_start[:, None] >= ends[None, :]).astype(jnp.int32), axis=1)
    last_expert = jnp.sum((jnp.maximum(ends[-1] - 1, 0) >= ends).astype(jnp.int32))
    tile_expert = jnp.minimum(tile_expert, last_expert).astype(jnp.int32)

    xs = xb.at[src].get(mode="promise_in_bounds")
    ys = _moe_grouped(xs, tile_expert, n_used, w_gate, w_up, w_down)
    slot_major = dest.reshape(T, TOP_K).T.reshape(-1)
    y_rows = ys.at[slot_major].get(mode="promise_in_bounds")
    return _combine_ln(x, y_rows, top_w, gamma, beta)


def _memattn_kernel(q_ref, k_ref, v_ref, o_ref):
    q = q_ref[0]
    outs = []
    for h in range(MEM_HEADS):
        sl = slice(h * MEM_HEAD_DIM, (h + 1) * MEM_HEAD_DIM)
        s = _dot_nt(q[:, sl].astype(BF16), k_ref[0, :, sl]) * (MEM_HEAD_DIM ** -0.5)
        e = jnp.exp(s - jnp.max(s, axis=-1, keepdims=True))
        p = e / jnp.sum(e, axis=-1, keepdims=True)
        outs.append(_dot(p.astype(BF16), v_ref[0, :, sl]))
    o_ref[0] = jnp.concatenate(outs, axis=-1).astype(o_ref.dtype)


def _memory_attention(p3, mem_kv):
    B, S, _ = p3.shape
    M = mem_kv.shape[1]
    ts = min(512, S)
    return pl.pallas_call(
        _memattn_kernel,
        out_shape=jax.ShapeDtypeStruct((B, S, MEM_WIDTH), BF16),
        grid=(B, S // ts),
        in_specs=[pl.BlockSpec((1, ts, MEM_WIDTH), lambda b, s: (b, s, Q_MEM_OFF // MEM_WIDTH)),
                  pl.BlockSpec((1, M, MEM_WIDTH), lambda b, s: (b, 0, 0)),
                  pl.BlockSpec((1, M, MEM_WIDTH), lambda b, s: (b, 0, 1))],
        out_specs=pl.BlockSpec((1, ts, MEM_WIDTH), lambda b, s: (b, s, 0)),
        compiler_params=_cparams("parallel", "parallel"),
        name="mem_attn",
    )(p3, mem_kv, mem_kv)


def _ret_kernel(q_ref, k_ref, v_ref, g_ref, cos_ref, sin_ref, intra_ref, qd_ref, kd_ref, cd_ref,
                gn_ref, o_ref, state_ref):
    @pl.when(pl.program_id(1) == 0)
    def _():
        state_ref[...] = jnp.zeros_like(state_ref)

    cosf = cos_ref[0]
    sinf = sin_ref[0]
    half = RET_DK // 2
    rot = lambda x: x * cosf + pltpu.roll(x, half, 1) * sinf
    hs = range(RET_HEADS)
    ksl = [slice(h * RET_DK, (h + 1) * RET_DK) for h in hs]
    vsl = [slice(h * RET_DV, (h + 1) * RET_DV) for h in hs]

    qr = [rot(q_ref[0, :, ksl[h]]) for h in hs]
    kr = [rot(k_ref[0, :, ksl[h]]) * (RET_DK ** -0.5) for h in hs]
    vb = [v_ref[0, :, vsl[h]].astype(BF16) for h in hs]
    state = [state_ref[h] for h in hs]
    s = [_dot_nt(qr[h].astype(BF16), kr[h].astype(BF16)) * intra_ref[h] for h in hs]
    o = [_dot(s[h].astype(BF16), vb[h]) + _dot((qr[h] * qd_ref[h]).astype(BF16), state[h].astype(BF16))
         for h in hs]
    for h in hs:
        state_ref[h] = state[h] * cd_ref[h] + _dot_tn((kr[h] * kd_ref[h]).astype(BF16), vb[h])
    for h in hs:
        mu = jnp.mean(o[h], axis=-1, keepdims=True)
        oc = o[h] - mu
        var = jnp.mean(oc * oc, axis=-1, keepdims=True)
        y = oc * lax.rsqrt(var + LN_EPS) * gn_ref[0:1, vsl[h]] + gn_ref[1:2, vsl[h]]
        g = g_ref[0, :, vsl[h]]
        o_ref[0, :, vsl[h]] = (y * (g * jax.nn.sigmoid(g))).astype(o_ref.dtype)


def _retention_mixer(p3, positions, gn):
    B, S, _ = p3.shape
    C = RET_CHUNK
    H = RET_HEADS
    inv = 1.0 / (ROPE_BASE ** (jnp.arange(0, RET_DK, 2, dtype=F32) / RET_DK))
    ang = positions.astype(F32)[..., None] * inv
    cos, sin = jnp.cos(ang), jnp.sin(ang)
    cosf = jnp.concatenate([cos, cos], axis=-1)
    sinf = jnp.concatenate([-sin, sin], axis=-1)

    log_g = jnp.log1p(-jnp.exp2(-5.0 - jnp.arange(H, dtype=F32)))
    idx = jnp.arange(C, dtype=F32)
    diff = idx[:, None] - idx[None, :]
    intra = jnp.where(diff >= 0, jnp.exp(jnp.maximum(diff, 0.0)[None] * log_g[:, None, None]), 0.0)
    qd = jnp.broadcast_to(jnp.exp((idx + 1.0)[None, :, None] * log_g[:, None, None]), (H, C, RET_DK))
    kd = jnp.broadcast_to(jnp.exp((C - 1.0 - idx)[None, :, None] * log_g[:, None, None]), (H, C, RET_DK))
    cd = jnp.broadcast_to(jnp.exp(C * log_g)[:, None, None], (H, 1, RET_DV))

    QK = H * RET_DK
    assert 2 * QK == SEQ_WIDTH
    whole = lambda b, c: (0, 0, 0)
    return pl.pallas_call(
        _ret_kernel,
        out_shape=jax.ShapeDtypeStruct((B, S, SEQ_WIDTH), BF16),
        grid=(B, S // C),
        in_specs=[pl.BlockSpec((1, C, QK), lambda b, c: (b, c, 0)),
                  pl.BlockSpec((1, C, QK), lambda b, c: (b, c, 1)),
                  pl.BlockSpec((1, C, SEQ_WIDTH), lambda b, c: (b, c, 1)),
                  pl.BlockSpec((1, C, SEQ_WIDTH), lambda b, c: (b, c, 2)),
                  pl.BlockSpec((1, C, RET_DK), lambda b, c: (b, c, 0)),
                  pl.BlockSpec((1, C, RET_DK), lambda b, c: (b, c, 0)),
                  pl.BlockSpec((H, C, C), whole),
                  pl.BlockSpec((H, C, RET_DK), whole),
                  pl.BlockSpec((H, C, RET_DK), whole),
                  pl.BlockSpec((H, 1, RET_DV), whole),
                  pl.BlockSpec((2, SEQ_WIDTH), lambda b, c: (0, 0))],
        out_specs=pl.BlockSpec((1, C, SEQ_WIDTH), lambda b, c: (b, c, 0)),
        scratch_shapes=[pltpu.VMEM((H, RET_DK, RET_DV), F32)],
        compiler_params=_cparams("parallel", "arbitrary"),
        name="retention",
    )(p3, p3, p3, p3, cosf, sinf, intra, qd, kd, cd, gn)


RWKV_PAIRS = 12
PAIR = 2 * RWKV_HEAD

_NN = ((1,), (0,))
_NT = ((1,), (1,))
_TN = ((0,), (0,))


def _dg(a, b, dims):
    return lax.dot_general(a.astype(BF16), b.astype(BF16), (dims, ((), ())), preferred_element_type=F32)


def _rwkv_chunk(r, c, lw, k, v, kk, a, state):
    n = len(r)
    rng = range(n)
    L = r[0].shape[0]
    lane = lax.broadcasted_iota(jnp.int32, (L, PAIR), 1)
    m1 = (lane < RWKV_HEAD).astype(F32)
    m2 = 1.0 - m1
    stack = lambda x: jnp.concatenate([x * m1, x * m2], axis=0)
    n2 = 2 * L
    row = lax.broadcasted_iota(jnp.int32, (n2, n2), 0)
    col = lax.broadcasted_iota(jnp.int32, (n2, n2), 1)
    strict = (row % L > col % L).astype(F32)
    incl = (row % L >= col % L).astype(F32)
    eye = (row == col).astype(F32)

    p_incl = [jnp.exp(c[j]) for j in rng]
    p_inv = [jnp.exp(-c[j]) for j in rng]
    at = [-kk[j] * jnp.exp(c[j] - lw[j]) for j in rng]
    rt = [r[j] * p_incl[j] for j in rng]
    lhs = [jnp.concatenate([stack(at[j]), stack(rt[j])], axis=0).astype(BF16) for j in rng]
    rhs = [jnp.concatenate([stack(kk[j] * a[j] * p_inv[j]), stack(k[j] * p_inv[j])], axis=0).astype(BF16)
           for j in rng]
    sc = [_dg(lhs[j], rhs[j], _NT) for j in rng]
    nmat = [sc[j][:n2, :n2] * strict for j in rng]
    tinv = [eye + nmat[j] for j in rng]
    pw = [nmat[j].astype(BF16) for j in rng]
    for _ in range(int(np.log2(L)) - 1):
        pw = [_dg(pw[j], pw[j], _NN).astype(BF16) for j in rng]
        tinv = [tinv[j] + _dg(tinv[j], pw[j], _NN) for j in rng]

    vs = [stack(v[j]).astype(BF16) for j in rng]
    sg = [_dg(jnp.concatenate([at[j], rt[j]], axis=0), state[j], _NT) for j in rng]
    mv = [_dg(sc[j][:n2, n2:] * strict, vs[j], _NN) for j in rng]
    u = [_dg(tinv[j], stack(sg[j][:L]) + mv[j], _NN) for j in rng]
    uv = [jnp.concatenate([u[j].astype(BF16), vs[j]], axis=0) for j in rng]
    ys = [_dg(sc[j][n2:, :] * jnp.concatenate([incl, incl], axis=1), uv[j], _NN) for j in rng]
    y = [sg[j][L:] + ys[j][:L] + ys[j][L:] for j in rng]
    new_state = [(state[j] + _dg(uv[j], rhs[j], _TN)) * p_incl[j][L - 1:L, :] for j in rng]
    return y, new_state


def _shift_mix(x, prev_block, mu, first):
    prev_row = jnp.where(first, 0.0, prev_block[7:8, :])
    rows = lax.broadcasted_iota(jnp.int32, x.shape, 0)
    prev = jnp.where(rows == 0, prev_row, pltpu.roll(x, 1, 0))
    return x + mu * (prev - x)


def _rwkv_mixer_kernel(*refs, with_v):
    it = iter(refs)
    r_ref, k_ref, v_ref, xg_ref, xl_ref = (next(it) for _ in range(5))
    rp_ref, kp_ref, vp_ref, xgp_ref, xlp_ref = (next(it) for _ in range(5))
    mu3_ref, mug_ref, mul_ref, vec_ref, ww_ref, wa_ref = (next(it) for _ in range(6))
    wv_ref = next(it) if with_v else None
    wg_ref = next(it)
    vf_ref = next(it) if with_v else None
    y_ref = next(it)
    vfo_ref = None if with_v else next(it)
    state_ref = next(it)

    first = pl.program_id(2) == 0

    @pl.when(first)
    def _():
        state_ref[...] = jnp.zeros_like(state_ref)

    L = RWKV_CHUNK
    r = _shift_mix(r_ref[0], rp_ref[0], mu3_ref[0:1, :], first)
    k = _shift_mix(k_ref[0], kp_ref[0], mu3_ref[1:2, :], first)
    v = _shift_mix(v_ref[0], vp_ref[0], mu3_ref[2:3, :], first)
    xg = _shift_mix(xg_ref[0], xgp_ref[0], mug_ref[...], first)
    xl = _shift_mix(xl_ref[0], xlp_ref[0], mul_ref[...], first)

    w0, a0, v0 = vec_ref[0:1, :], vec_ref[1:2, :], vec_ref[2:3, :]
    k_k, k_a, r_k = vec_ref[3:4, :], vec_ref[4:5, :], vec_ref[5:6, :]
    gn_g, gn_b = vec_ref[6:7, :], vec_ref[7:8, :]

    lane = lax.broadcasted_iota(jnp.int32, xl.shape, 1)
    lin = jnp.where(lane < LORA_W, jnp.tanh(xl), xl).astype(BF16)
    w_arg = w0 + _dot(lin, ww_ref[...])
    a = jax.nn.sigmoid(a0 + _dot(lin, wa_ref[...]))
    if with_v:
        v = v + (vf_ref[0] - v) * jax.nn.sigmoid(v0 + _dot(lin, wv_ref[...]))
    else:
        vfo_ref[0] = v
    g = _dot(jax.nn.sigmoid(xg).astype(BF16), wg_ref[...])

    w_log = -(jnp.maximum(-w_arg, 0.0) + jnp.log1p(jnp.exp(-jnp.abs(w_arg)))) - 0.5
    lw = -jnp.exp(w_log)
    tri = (lax.broadcasted_iota(jnp.int32, (L, L), 0) >= lax.broadcasted_iota(jnp.int32, (L, L), 1)).astype(BF16)
    h1 = lw.astype(BF16)
    rem = lw - h1.astype(F32)
    h2 = rem.astype(BF16)
    h3 = (rem - h2.astype(F32)).astype(BF16)
    c = _dot(tri, h1) + _dot(tri, h2) + _dot(tri, h3)

    kk_raw = k * k_k
    k2 = k * (1.0 + (a - 1.0) * k_a)
    ones_bd = (lax.broadcasted_iota(jnp.int32, (PAIR, PAIR), 0) // RWKV_HEAD
               == lax.broadcasted_iota(jnp.int32, (PAIR, PAIR), 1) // RWKV_HEAD).astype(BF16)
    inv_n = 1.0 / RWKV_HEAD

    rng = range(RWKV_PAIRS)
    sls = [slice(j * PAIR, (j + 1) * PAIR) for j in rng]
    part = lambda x: [x[:, sl] for sl in sls]
    kkr, rs, ks, vs = part(kk_raw), part(r), part(k2), part(v)
    red = [_dot(jnp.concatenate([kkr[j] * kkr[j], rs[j] * ks[j] * r_k[:, sls[j]]], axis=0).astype(BF16), ones_bd)
           for j in rng]
    kks = [kkr[j] / jnp.maximum(jnp.sqrt(red[j][:L]), 1e-12) for j in rng]
    ys, new_state = _rwkv_chunk(rs, part(c), part(lw), ks, vs, kks, part(a), [state_ref[j] for j in rng])
    for j in rng:
        state_ref[j] = new_state[j]
    yc = [ys[j] - _dot(ys[j].astype(BF16), ones_bd) * inv_n for j in rng]
    var = [_dot((yc[j] * yc[j]).astype(BF16), ones_bd) * inv_n for j in rng]
    for j in rng:
        yn = yc[j] * lax.rsqrt(var[j] + RWKV_GN_EPS) * gn_g[:, sls[j]] + gn_b[:, sls[j]]
        y_ref[0, :, sls[j]] = ((yn + red[j][L:] * vs[j]) * g[:, sls[j]]).astype(y_ref.dtype)


def _rwkv_layout(w_in, mu, with_v):
    sizes = [SEQ_WIDTH] * 3 + [LORA_W, LORA_A] + ([LORA_V] if with_v else []) + [LORA_G]
    offs = np.concatenate([[0], np.cumsum(sizes)])
    seq_cols = int(offs[-1])
    g_idx = len(sizes) - 1
    pad = LORA_PACK - LORA_W - LORA_A - (LORA_V if with_v else 0)
    cols = [w_in[:, :3 * SEQ_WIDTH], w_in[:, seq_cols:], w_in[:, offs[g_idx]:offs[g_idx + 1]],
            w_in[:, offs[3]:offs[g_idx]], jnp.zeros((w_in.shape[0], pad), w_in.dtype)]
    mus = [mu[:3 * SEQ_WIDTH], jnp.zeros((MEM_WIDTH,), mu.dtype), mu[offs[g_idx]:offs[g_idx + 1]],
           mu[offs[3]:offs[g_idx]], jnp.zeros((pad,), mu.dtype)]
    return jnp.concatenate(cols, axis=1), jnp.concatenate(mus)


def _rwkv_mixer(p3, mu, lora_w, lora_a, lora_g, lora_v, vecs, v_first):
    B, S, _ = p3.shape
    L = RWKV_CHUNK
    W = RWKV_PAIRS * PAIR
    with_v = lora_v is not None
    if with_v:
        w0, a0, v0, k_k, k_a, r_k, gn_g, gn_b = vecs
    else:
        w0, a0, k_k, k_a, r_k, gn_g, gn_b = vecs
        v0 = jnp.zeros_like(w0)
    vec8 = jnp.stack([w0, a0, v0, k_k, k_a, r_k, gn_g, gn_b])
    mu3 = jnp.concatenate([mu[:3 * SEQ_WIDTH].reshape(3, SEQ_WIDTH), jnp.zeros((5, SEQ_WIDTH), F32)])
    g_off = Q_MEM_OFF + MEM_WIDTH
    l_off = g_off + LORA_G
    mug = mu[g_off:g_off + LORA_G].reshape(1, LORA_G)
    mul = mu[l_off:l_off + LORA_PACK].reshape(1, LORA_PACK)
    rows = lambda w, lo: jnp.zeros((LORA_PACK, SEQ_WIDTH), F32).at[lo:lo + w.shape[0]].set(w).astype(BF16)

    def cols(width, off):
        blk = off // width
        return pl.BlockSpec((1, L, width), lambda b, g, t: (b, t, blk))

    def cols_prev(width, off):
        blk = off // width
        return pl.BlockSpec((1, 8, width), lambda b, g, t: (b, jnp.maximum(t * (L // 8) - 1, 0), blk))

    def seq(off):
        blk = off // W
        return pl.BlockSpec((1, L, W), lambda b, g, t: (b, t, blk + g))

    def seq_prev(off):
        blk = off // W
        return pl.BlockSpec((1, 8, W), lambda b, g, t: (b, jnp.maximum(t * (L // 8) - 1, 0), blk + g))

    per_group = lambda nrows: pl.BlockSpec((nrows, W), lambda b, g, t: (0, g))
    fixed = lambda ncols: pl.BlockSpec((1, ncols), lambda b, g, t: (0, 0))

    in_specs = [seq(0), seq(SEQ_WIDTH), seq(2 * SEQ_WIDTH), cols(LORA_G, g_off), cols(LORA_PACK, l_off),
                seq_prev(0), seq_prev(SEQ_WIDTH), seq_prev(2 * SEQ_WIDTH),
                cols_prev(LORA_G, g_off), cols_prev(LORA_PACK, l_off),
                per_group(8), fixed(LORA_G), fixed(LORA_PACK), per_group(8),
                per_group(LORA_PACK), per_group(LORA_PACK)]
    args = [p3] * 10 + [mu3, mug, mul, vec8, rows(lora_w, 0), rows(lora_a, LORA_W)]
    if with_v:
        in_specs.append(per_group(LORA_PACK))
        args.append(rows(lora_v, LORA_W + LORA_A))
    in_specs.append(per_group(LORA_G))
    args.append(lora_g.astype(BF16))
    out_spec = pl.BlockSpec((1, L, W), lambda b, g, t: (b, t, g))
    y_shape = jax.ShapeDtypeStruct((B, S, SEQ_WIDTH), BF16)
    if with_v:
        in_specs.append(out_spec)
        args.append(v_first)
        out_shape, out_specs = y_shape, out_spec
    else:
        out_shape = (y_shape, jax.ShapeDtypeStruct((B, S, SEQ_WIDTH), F32))
        out_specs = (out_spec, out_spec)

    res = pl.pallas_call(
        functools.partial(_rwkv_mixer_kernel, with_v=with_v),
        out_shape=out_shape,
        grid=(B, SEQ_WIDTH // W, S // L),
        in_specs=in_specs,
        out_specs=out_specs,
        scratch_shapes=[pltpu.VMEM((RWKV_PAIRS, PAIR, PAIR), F32)],
        compiler_params=_cparams("parallel", "parallel", "arbitrary"),
        name="rwkv_mixer",
    )(*args)
    return (res, v_first) if with_v else res


def kernel(x, mem, positions, mem_w_kv, l0_w_in, l0_mu, l0_lora_w, l0_lora_a, l0_lora_g, l0_vecs, l0_w_out, l0_norms, l0_ffn_gate, l0_ffn_up, l0_ffn_down, l1_w_in, l1_gn, l1_w_out, l1_norms, l1_router, l1_moe_gate, l1_moe_up, l1_moe_down, l2_w_in, l2_mu, l2_lora_w, l2_lora_a, l2_lora_v, l2_lora_g, l2_vecs, l2_w_out, l2_norms, l2_ffn_gate, l2_ffn_up, l2_ffn_down, l3_w_in, l3_gn, l3_w_out, l3_norms, l3_router, l3_moe_gate, l3_moe_up, l3_moe_down):
    B, S, D = x.shape
    T = B * S
    M = mem.shape[1]
    mem_kv = _matmul(mem.reshape(B * M, D), mem_w_kv.astype(BF16), tm=1024, tn=512,
                     out_dtype=BF16).reshape(B, M, 2 * MEM_WIDTH)

    rwkv_layers = {
        0: (l0_w_in, l0_mu, l0_lora_w, l0_lora_a, l0_lora_g, None, l0_vecs),
        2: (l2_w_in, l2_mu, l2_lora_w, l2_lora_a, l2_lora_g, l2_lora_v, l2_vecs),
    }
    ret_layers = {1: (l1_w_in, l1_gn), 3: (l3_w_in, l3_gn)}
    w_outs = (l0_w_out, l1_w_out, l2_w_out, l3_w_out)
    norms = (l0_norms, l1_norms, l2_norms, l3_norms)
    dense = {0: (l0_ffn_gate, l0_ffn_up, l0_ffn_down), 2: (l2_ffn_gate, l2_ffn_up, l2_ffn_down)}
    moe = {1: (l1_router, l1_moe_gate, l1_moe_up, l1_moe_down),
           3: (l3_router, l3_moe_gate, l3_moe_up, l3_moe_down)}

    xf = x.reshape(T, D)
    xin = xf
    v_first = None
    for i in range(DEPTH):
        if i in rwkv_layers:
            w_in, mu, lw_, la_, lg_, lv_, vecs = rwkv_layers[i]
            w_in, mu = _rwkv_layout(w_in, mu, lv_ is not None)
            p3 = _matmul(xin, w_in.astype(BF16), tm=1024, tn=512).reshape(B, S, -1)
            y_seq, v_first = _rwkv_mixer(p3, mu, lw_, la_, lg_, lv_, vecs, v_first)
        else:
            w_in, gn = ret_layers[i]
            p3 = _matmul(xin, w_in.astype(BF16), tm=1024, tn=512).reshape(B, S, -1)
            y_seq = _retention_mixer(p3, positions, gn)
        y_mem = _memory_attention(p3, mem_kv)
        nrm = norms[i]
        xf, xb = _proj_ln(y_seq.reshape(T, SEQ_WIDTH), y_mem.reshape(T, MEM_WIDTH), w_outs[i], xf,
                          nrm[0], nrm[1])
        if i in dense:
            xf, xb = _ffn_ln(xf, xb, *dense[i], nrm[2], nrm[3])
        else:
            xf, xb = _moe_ln(xf, xb, *moe[i], nrm[2], nrm[3])
        xin = xb
    return xf.reshape(B, S, D)
```

```python
import functools

import numpy as np
import jax
import jax.numpy as jnp
from jax import lax
from jax.experimental import pallas as pl
from jax.experimental.pallas import tpu as pltpu

F32 = jnp.float32
BF16 = jnp.bfloat16

D_MODEL = 2048
DEPTH = 4
MEM_HEADS = 4
MEM_HEAD_DIM = 128
MEM_WIDTH = MEM_HEADS * MEM_HEAD_DIM
SEQ_WIDTH = D_MODEL - MEM_WIDTH

RWKV_HEAD = 64
RWKV_HEADS = SEQ_WIDTH // RWKV_HEAD
LORA_W = 96
LORA_A = 96
LORA_V = 64
LORA_G = 256
LORA_PACK = 256
RWKV_GN_EPS = 1e-5 * RWKV_HEAD
RWKV_CHUNK = 64

RET_HEADS = 6
RET_DV = SEQ_WIDTH // RET_HEADS
RET_DK = RET_DV // 2
RET_CHUNK = 128
ROPE_BASE = 10000.0

N_EXPERTS = 8
TOP_K = 2

ALPHA = (2.0 * DEPTH) ** 0.25
LN_EPS = 1e-5

LANES = 128
Q_MEM_OFF = 3 * SEQ_WIDTH
VMEM_LIMIT = 52 * 1024 * 1024
MOE_VMEM_LIMIT = 58 * 1024 * 1024


def _cparams(*semantics):
    return pltpu.CompilerParams(dimension_semantics=semantics, vmem_limit_bytes=VMEM_LIMIT)


def _dot(a, b):
    return jnp.dot(a, b, preferred_element_type=F32)


def _dot_nt(a, b):
    return lax.dot_general(a, b, (((1,), (1,)), ((), ())), preferred_element_type=F32)


def _dot_tn(a, b):
    return lax.dot_general(a, b, (((0,), (0,)), ((), ())), preferred_element_type=F32)


def _layer_norm_rows(z, g, b):
    mu = jnp.mean(z, axis=-1, keepdims=True)
    zc = z - mu
    var = jnp.mean(zc * zc, axis=-1, keepdims=True)
    return zc * lax.rsqrt(var + LN_EPS) * g + b


def _mm_kernel(a_ref, w_ref, o_ref, abf_ref):
    @pl.when(pl.program_id(1) == 0)
    def _():
        abf_ref[...] = a_ref[...].astype(BF16)

    o_ref[...] = _dot(abf_ref[...], w_ref[...]).astype(o_ref.dtype)


def _matmul(a, w, *, tm, tn, out_dtype=F32):
    M, K = a.shape
    N = w.shape[1]
    tm = min(tm, M)
    assert M % tm == 0 and N % tn == 0
    return pl.pallas_call(
        _mm_kernel,
        out_shape=jax.ShapeDtypeStruct((M, N), out_dtype),
        grid=(M // tm, N // tn),
        in_specs=[pl.BlockSpec((tm, K), lambda i, j: (i, 0)),
                  pl.BlockSpec((K, tn), lambda i, j: (0, j))],
        out_specs=pl.BlockSpec((tm, tn), lambda i, j: (i, j)),
        scratch_shapes=[pltpu.VMEM((tm, K), BF16)],
        compiler_params=_cparams("parallel", "arbitrary"),
        name="matmul",
    )(a, w)


def _router_kernel(x_ref, w_ref, o_ref):
    o_ref[...] = jnp.dot(x_ref[...], w_ref[...], preferred_element_type=F32,
                         precision=lax.Precision.HIGHEST)


def _router_logits(x, w_router):
    T = x.shape[0]
    tm = min(512, T)
    wp = jnp.pad(w_router, ((0, 0), (0, LANES - N_EXPERTS)))
    out = pl.pallas_call(
        _router_kernel,
        out_shape=jax.ShapeDtypeStruct((T, LANES), F32),
        grid=(T // tm,),
        in_specs=[pl.BlockSpec((tm, D_MODEL), lambda i: (i, 0)),
                  pl.BlockSpec((D_MODEL, LANES), lambda i: (0, 0))],
        out_specs=pl.BlockSpec((tm, LANES), lambda i: (i, 0)),
        compiler_params=_cparams("parallel"),
        name="router",
    )(x, wp)
    return out[:, :N_EXPERTS]


def _proj_ln_kernel(ys_ref, ym_ref, ws_ref, wm_ref, x_ref, g_ref, b_ref, o_ref, ob_ref):
    half = x_ref.shape[0] // 2
    for s in range(2):
        rs = slice(s * half, (s + 1) * half)
        h = _dot(ys_ref[rs, :], ws_ref[...]) + _dot(ym_ref[rs, :], wm_ref[...])
        out = _layer_norm_rows(ALPHA * x_ref[rs, :] + h, g_ref[...], b_ref[...])
        o_ref[rs, :] = out
        ob_ref[rs, :] = out.astype(BF16)


def _proj_ln(y_seq, y_mem, w_out, x, gamma, beta):
    T = x.shape[0]
    tm = min(512, T)
    ws = w_out[:SEQ_WIDTH].astype(BF16)
    wm = w_out[SEQ_WIDTH:].astype(BF16)
    row = lambda i: (i, 0)
    fixed = lambda i: (0, 0)
    once = pl.Buffered(1)
    return pl.pallas_call(
        _proj_ln_kernel,
        out_shape=(jax.ShapeDtypeStruct((T, D_MODEL), F32), jax.ShapeDtypeStruct((T, D_MODEL), BF16)),
        grid=(T // tm,),
        in_specs=[pl.BlockSpec((tm, SEQ_WIDTH), row), pl.BlockSpec((tm, MEM_WIDTH), row),
                  pl.BlockSpec((SEQ_WIDTH, D_MODEL), fixed, pipeline_mode=once),
                  pl.BlockSpec((MEM_WIDTH, D_MODEL), fixed, pipeline_mode=once),
                  pl.BlockSpec((tm, D_MODEL), row),
                  pl.BlockSpec((1, D_MODEL), fixed), pl.BlockSpec((1, D_MODEL), fixed)],
        out_specs=(pl.BlockSpec((tm, D_MODEL), row), pl.BlockSpec((tm, D_MODEL), row)),
        compiler_params=_cparams("parallel"),
        name="proj_ln",
    )(y_seq, y_mem, ws, wm, x, gamma.reshape(1, -1), beta.reshape(1, -1))


def _swiglu_partial(xb, wg, wu, wd):
    gate = _dot(xb, wg)
    up = _dot(xb, wu)
    h = (gate * jax.nn.sigmoid(gate) * up).astype(BF16)
    return _dot(h, wd)


def _ffn_ln_kernel(xb_ref, wg_ref, wu_ref, wd_ref, x_ref, g_ref, b_ref, o_ref, ob_ref, acc_ref):
    f = pl.program_id(1)

    @pl.when(f == 0)
    def _():
        acc_ref[...] = jnp.zeros_like(acc_ref)

    acc_ref[...] += _swiglu_partial(xb_ref[...], wg_ref[...], wu_ref[...], wd_ref[...])

    @pl.when(f == pl.num_programs(1) - 1)
    def _():
        out = _layer_norm_rows(ALPHA * x_ref[...] + acc_ref[...], g_ref[...], b_ref[...])
        o_ref[...] = out
        ob_ref[...] = out.astype(BF16)


def _ffn_ln(x, xb, w_gate, w_up, w_down, gamma, beta):
    T = x.shape[0]
    F = w_gate.shape[1]
    tm = min(512, T)
    tf = 512
    row = lambda i, f: (i, 0)
    fixed = lambda i, f: (0, 0)
    return pl.pallas_call(
        _ffn_ln_kernel,
        out_shape=(jax.ShapeDtypeStruct((T, D_MODEL), F32), jax.ShapeDtypeStruct((T, D_MODEL), BF16)),
        grid=(T // tm, F // tf),
        in_specs=[pl.BlockSpec((tm, D_MODEL), row),
                  pl.BlockSpec((D_MODEL, tf), lambda i, f: (0, f)),
                  pl.BlockSpec((D_MODEL, tf), lambda i, f: (0, f)),
                  pl.BlockSpec((tf, D_MODEL), lambda i, f: (f, 0)),
                  pl.BlockSpec((tm, D_MODEL), row),
                  pl.BlockSpec((1, D_MODEL), fixed), pl.BlockSpec((1, D_MODEL), fixed)],
        out_specs=(pl.BlockSpec((tm, D_MODEL), row), pl.BlockSpec((tm, D_MODEL), row)),
        scratch_shapes=[pltpu.VMEM((tm, D_MODEL), F32)],
        compiler_params=_cparams("parallel", "arbitrary"),
        name="ffn_ln",
    )(xb, w_gate.astype(BF16), w_up.astype(BF16), w_down.astype(BF16), x,
      gamma.reshape(1, -1), beta.reshape(1, -1))


MOE_TM = 1024
MOE_TF = 256


def _moe_kernel(te_ref, nu_ref, xs_ref, wg_ref, wu_ref, wd_ref, o_ref, acc_ref):
    i = pl.program_id(0)
    f = pl.program_id(1)
    last = pl.num_programs(1) - 1
    used = i < nu_ref[0]

    @pl.when(jnp.logical_and(used, f == 0))
    def _():
        acc_ref[...] = jnp.zeros_like(acc_ref)

    @pl.when(used)
    def _():
        acc_ref[...] += _swiglu_partial(xs_ref[...], wg_ref[...].astype(BF16), wu_ref[...].astype(BF16),
                                        wd_ref[...].astype(BF16))

    @pl.when(jnp.logical_and(used, f == last))
    def _():
        o_ref[...] = acc_ref[...].astype(o_ref.dtype)

    @pl.when(jnp.logical_and(jnp.logical_not(used), f == last))
    def _():
        o_ref[...] = jnp.zeros_like(o_ref)


def _moe_grouped(xs, tile_expert, n_used, w_gate, w_up, w_down):
    Tp = xs.shape[0]
    F = w_gate.shape[2]
    nf = F // MOE_TF

    def fcol(i, f, nu):
        return jnp.where(i < nu[0], f, nf - 1)

    grid_spec = pltpu.PrefetchScalarGridSpec(
        num_scalar_prefetch=2,
        grid=(Tp // MOE_TM, nf),
        in_specs=[pl.BlockSpec((MOE_TM, D_MODEL), lambda i, f, te, nu: (i, 0)),
                  pl.BlockSpec((None, D_MODEL, MOE_TF), lambda i, f, te, nu: (te[i], 0, fcol(i, f, nu))),
                  pl.BlockSpec((None, D_MODEL, MOE_TF), lambda i, f, te, nu: (te[i], 0, fcol(i, f, nu))),
                  pl.BlockSpec((None, MOE_TF, D_MODEL), lambda i, f, te, nu: (te[i], fcol(i, f, nu), 0))],
        out_specs=pl.BlockSpec((MOE_TM, D_MODEL), lambda i, f, te, nu: (i, 0)),
        scratch_shapes=[pltpu.VMEM((MOE_TM, D_MODEL), F32)],
    )
    return pl.pallas_call(
        _moe_kernel,
        out_shape=jax.ShapeDtypeStruct((Tp, D_MODEL), BF16),
        grid_spec=grid_spec,
        compiler_params=pltpu.CompilerParams(dimension_semantics=("arbitrary", "arbitrary"),
                                             vmem_limit_bytes=MOE_VMEM_LIMIT),
        name="moe_grouped",
    )(tile_expert, n_used, xs, w_gate, w_up, w_down)


def _combine_ln_kernel(x_ref, ya_ref, yb_ref, w_ref, g_ref, b_ref, o_ref, ob_ref):
    w = w_ref[...]
    f = w[:, 0:1] * ya_ref[...].astype(F32) + w[:, 1:2] * yb_ref[...].astype(F32)
    out = _layer_norm_rows(ALPHA * x_ref[...] + f, g_ref[...], b_ref[...])
    o_ref[...] = out
    ob_ref[...] = out.astype(BF16)


def _combine_ln(x, y_rows, top_w, gamma, beta):
    T = x.shape[0]
    tm = min(512, T)
    nt = T // tm
    wpad = jnp.pad(top_w, ((0, 0), (0, LANES - TOP_K)))
    fixed = lambda i: (0, 0)
    row = lambda i: (i, 0)
    return pl.pallas_call(
        _combine_ln_kernel,
        out_shape=(jax.ShapeDtypeStruct((T, D_MODEL), F32), jax.ShapeDtypeStruct((T, D_MODEL), BF16)),
        grid=(nt,),
        in_specs=[pl.BlockSpec((tm, D_MODEL), row),
                  pl.BlockSpec((tm, D_MODEL), row),
                  pl.BlockSpec((tm, D_MODEL), lambda i: (i + nt, 0)),
                  pl.BlockSpec((tm, LANES), row),
                  pl.BlockSpec((1, D_MODEL), fixed), pl.BlockSpec((1, D_MODEL), fixed)],
        out_specs=(pl.BlockSpec((tm, D_MODEL), row), pl.BlockSpec((tm, D_MODEL), row)),
        compiler_params=_cparams("parallel"),
        name="combine_ln",
    )(x, y_rows, y_rows, wpad, gamma.reshape(1, -1), beta.reshape(1, -1))


def _moe_ln(x, xb, w_router, w_gate, w_up, w_down, gamma, beta):
    T = x.shape[0]
    logits = _router_logits(x, w_router)
    top_val, top_idx = lax.top_k(logits, TOP_K)
    top_w = jax.nn.softmax(top_val, axis=-1)

    e_flat = top_idx.reshape(-1).astype(jnp.int32)
    onehot = (e_flat[:, None] == jnp.arange(N_EXPERTS, dtype=jnp.int32)[None, :]).astype(jnp.int32)
    rank = jnp.sum((jnp.cumsum(onehot, axis=0) - onehot) * onehot, axis=1)
    counts = jnp.sum(onehot, axis=0)
    padded = ((counts + MOE_TM - 1) // MOE_TM) * MOE_TM
    ends = jnp.cumsum(padded)
    starts = ends - padded
    dest = starts[e_flat] + rank

    Tp = TOP_K * T + N_EXPERTS * MOE_TM
    n_tiles = Tp // MOE_TM
    token = jnp.arange(TOP_K * T, dtype=jnp.int32) // TOP_K
    src = jnp.zeros((Tp,), jnp.int32).at[dest].set(token)
    n_used = (ends[-1] // MOE_TM).astype(jnp.int32).reshape(1)
    tile_start = jnp.arange(n_tiles, dtype=jnp.int32) * MOE_TM
    tile_expert = jnp.sum((tile_start[:, None] >= ends[None, :]).astype(jnp.int32), axis=1)
    last_expert = jnp.sum((jnp.maximum(ends[-1] - 1, 0) >= ends).astype(jnp.int32))
    tile_expert = jnp.minimum(tile_expert, last_expert).astype(jnp.int32)

    xs = xb.at[src].get(mode="promise_in_bounds")
    ys = _moe_grouped(xs, tile_expert, n_used, w_gate, w_up, w_down)
    slot_major = dest.reshape(T, TOP_K).T.reshape(-1)
    y_rows = ys.at[slot_major].get(mode="promise_in_bounds")
    return _combine_ln(x, y_rows, top_w, gamma, beta)


def _memattn_kernel(q_ref, k_ref, v_ref, o_ref):
    q = q_ref[0]
    outs = []
    for h in range(MEM_HEADS):
        sl = slice(h * MEM_HEAD_DIM, (h + 1) * MEM_HEAD_DIM)
        s = _dot_nt(q[:, sl].astype(BF16), k_ref[0, :, sl]) * (MEM_HEAD_DIM ** -0.5)
        e = jnp.exp(s - jnp.max(s, axis=-1, keepdims=True))
        p = e / jnp.sum(e, axis=-1, keepdims=True)
        outs.append(_dot(p.astype(BF16), v_ref[0, :, sl]))
    o_ref[0] = jnp.concatenate(outs, axis=-1).astype(o_ref.dtype)


def _memory_attention(p3, mem_kv):
    B, S, _ = p3.shape
    M = mem_kv.shape[1]
    ts = min(512, S)
    return pl.pallas_call(
        _memattn_kernel,
        out_shape=jax.ShapeDtypeStruct((B, S, MEM_WIDTH), BF16),
        grid=(B, S // ts),
        in_specs=[pl.BlockSpec((1, ts, MEM_WIDTH), lambda b, s: (b, s, Q_MEM_OFF // MEM_WIDTH)),
                  pl.BlockSpec((1, M, MEM_WIDTH), lambda b, s: (b, 0, 0)),
                  pl.BlockSpec((1, M, MEM_WIDTH), lambda b, s: (b, 0, 1))],
        out_specs=pl.BlockSpec((1, ts, MEM_WIDTH), lambda b, s: (b, s, 0)),
        compiler_params=_cparams("parallel", "parallel"),
        name="mem_attn",
    )(p3, mem_kv, mem_kv)


def _ret_kernel(q_ref, k_ref, v_ref, g_ref, cos_ref, sin_ref, intra_ref, qd_ref, kd_ref, cd_ref,
                gn_ref, o_ref, state_ref):
    @pl.when(pl.program_id(1) == 0)
    def _():
        state_ref[...] = jnp.zeros_like(state_ref)

    cosf = cos_ref[0]
    sinf = sin_ref[0]
    half = RET_DK // 2
    rot = lambda x: x * cosf + pltpu.roll(x, half, 1) * sinf
    hs = range(RET_HEADS)
    ksl = [slice(h * RET_DK, (h + 1) * RET_DK) for h in hs]
    vsl = [slice(h * RET_DV, (h + 1) * RET_DV) for h in hs]

    qr = [rot(q_ref[0, :, ksl[h]].astype(F32)) for h in hs]
    kr = [rot(k_ref[0, :, ksl[h]].astype(F32)) * (RET_DK ** -0.5) for h in hs]
    vb = [v_ref[0, :, vsl[h]].astype(BF16) for h in hs]
    state = [state_ref[h] for h in hs]
    s = [_dot_nt(qr[h].astype(BF16), kr[h].astype(BF16)) * intra_ref[h] for h in hs]
    o = [_dot(s[h].astype(BF16), vb[h]) + _dot((qr[h] * qd_ref[h]).astype(BF16), state[h].astype(BF16))
         for h in hs]
    for h in hs:
        state_ref[h] = state[h] * cd_ref[h] + _dot_tn((kr[h] * kd_ref[h]).astype(BF16), vb[h])
    for h in hs:
        mu = jnp.mean(o[h], axis=-1, keepdims=True)
        oc = o[h] - mu
        var = jnp.mean(oc * oc, axis=-1, keepdims=True)
        y = oc * lax.rsqrt(var + LN_EPS) * gn_ref[0:1, vsl[h]] + gn_ref[1:2, vsl[h]]
        g = g_ref[0, :, vsl[h]].astype(F32)
        o_ref[0, :, vsl[h]] = (y * (g * jax.nn.sigmoid(g))).astype(o_ref.dtype)


def _retention_mixer(p3, positions, gn):
    B, S, _ = p3.shape
    C = RET_CHUNK
    H = RET_HEADS
    inv = 1.0 / (ROPE_BASE ** (jnp.arange(0, RET_DK, 2, dtype=F32) / RET_DK))
    ang = positions.astype(F32)[..., None] * inv
    cos, sin = jnp.cos(ang), jnp.sin(ang)
    cosf = jnp.concatenate([cos, cos], axis=-1)
    sinf = jnp.concatenate([-sin, sin], axis=-1)

    log_g = jnp.log1p(-jnp.exp2(-5.0 - jnp.arange(H, dtype=F32)))
    idx = jnp.arange(C, dtype=F32)
    diff = idx[:, None] - idx[None, :]
    intra = jnp.where(diff >= 0, jnp.exp(jnp.maximum(diff, 0.0)[None] * log_g[:, None, None]), 0.0)
    qd = jnp.broadcast_to(jnp.exp((idx + 1.0)[None, :, None] * log_g[:, None, None]), (H, C, RET_DK))
    kd = jnp.broadcast_to(jnp.exp((C - 1.0 - idx)[None, :, None] * log_g[:, None, None]), (H, C, RET_DK))
    cd = jnp.broadcast_to(jnp.exp(C * log_g)[:, None, None], (H, 1, RET_DV))

    QK = H * RET_DK
    assert 2 * QK == SEQ_WIDTH
    whole = lambda b, c: (0, 0, 0)
    return pl.pallas_call(
        _ret_kernel,
        out_shape=jax.ShapeDtypeStruct((B, S, SEQ_WIDTH), BF16),
        grid=(B, S // C),
        in_specs=[pl.BlockSpec((1, C, QK), lambda b, c: (b, c, 0)),
                  pl.BlockSpec((1, C, QK), lambda b, c: (b, c, 1)),
                  pl.BlockSpec((1, C, SEQ_WIDTH), lambda b, c: (b, c, 1)),
                  pl.BlockSpec((1, C, SEQ_WIDTH), lambda b, c: (b, c, 2)),
                  pl.BlockSpec((1, C, RET_DK), lambda b, c: (b, c, 0)),
                  pl.BlockSpec((1, C, RET_DK), lambda b, c: (b, c, 0)),
                  pl.BlockSpec((H, C, C), whole),
                  pl.BlockSpec((H, C, RET_DK), whole),
                  pl.BlockSpec((H, C, RET_DK), whole),
                  pl.BlockSpec((H, 1, RET_DV), whole),
                  pl.BlockSpec((2, SEQ_WIDTH), lambda b, c: (0, 0))],
        out_specs=pl.BlockSpec((1, C, SEQ_WIDTH), lambda b, c: (b, c, 0)),
        scratch_shapes=[pltpu.VMEM((H, RET_DK, RET_DV), F32)],
        compiler_params=_cparams("parallel", "arbitrary"),
        name="retention",
    )(p3, p3, p3, p3, cosf, sinf, intra, qd, kd, cd, gn)


RWKV_PAIRS = 12
PAIR = 2 * RWKV_HEAD

_NN = ((1,), (0,))
_NT = ((1,), (1,))
_TN = ((0,), (0,))


def _dg(a, b, dims):
    return lax.dot_general(a.astype(BF16), b.astype(BF16), (dims, ((), ())), preferred_element_type=F32)


def _rwkv_chunk(r, c, lw, k, v, kk, a, state):
    n = len(r)
    rng = range(n)
    L = r[0].shape[0]
    lane = lax.broadcasted_iota(jnp.int32, (L, PAIR), 1)
    m1 = (lane < RWKV_HEAD).astype(F32)
    m2 = 1.0 - m1
    stack = lambda x: jnp.concatenate([x * m1, x * m2], axis=0)
    n2 = 2 * L
    row = lax.broadcasted_iota(jnp.int32, (n2, n2), 0)
    col = lax.broadcasted_iota(jnp.int32, (n2, n2), 1)
    strict = (row % L > col % L).astype(F32)
    incl = (row % L >= col % L).astype(F32)
    eye = (row == col).astype(F32)

    p_incl = [jnp.exp(c[j]) for j in rng]
    p_inv = [jnp.exp(-c[j]) for j in rng]
    at = [-kk[j] * jnp.exp(c[j] - lw[j]) for j in rng]
    rt = [r[j] * p_incl[j] for j in rng]
    lhs = [jnp.concatenate([stack(at[j]), stack(rt[j])], axis=0).astype(BF16) for j in rng]
    rhs = [jnp.concatenate([stack(kk[j] * a[j] * p_inv[j]), stack(k[j] * p_inv[j])], axis=0).astype(BF16)
           for j in rng]
    sc = [_dg(lhs[j], rhs[j], _NT) for j in rng]
    nmat = [sc[j][:n2, :n2] * strict for j in rng]
    tinv = [eye + nmat[j] for j in rng]
    nb = [nmat[j].astype(BF16) for j in rng]
    pw = [_dg(nb[j], nb[j], _NN).astype(BF16) for j in rng]
    rounds = int(np.log2(L)) - 1
    for i in range(rounds):
        if i < rounds - 1:
            both = [_dg(jnp.concatenate([pw[j], tinv[j].astype(BF16)], axis=0), pw[j], _NN) for j in rng]
            pw = [both[j][:n2].astype(BF16) for j in rng]
            tinv = [tinv[j] + both[j][n2:] for j in rng]
        else:
            tinv = [tinv[j] + _dg(tinv[j], pw[j], _NN) for j in rng]

    vs = [stack(v[j]).astype(BF16) for j in rng]
    sg = [_dg(jnp.concatenate([at[j], rt[j]], axis=0), state[j], _NT) for j in rng]
    mv = [_dg(sc[j][:n2, n2:] * strict, vs[j], _NN) for j in rng]
    u = [_dg(tinv[j], stack(sg[j][:L]) + mv[j], _NN) for j in rng]
    uv = [jnp.concatenate([u[j].astype(BF16), vs[j]], axis=0) for j in rng]
    ys = [_dg(sc[j][n2:, :] * jnp.concatenate([incl, incl], axis=1), uv[j], _NN) for j in rng]
    y = [sg[j][L:] + ys[j][:L] + ys[j][L:] for j in rng]
    new_state = [(state[j] + _dg(uv[j], rhs[j], _TN)) * p_incl[j][L - 1:L, :] for j in rng]
    return y, new_state


def _shift_mix(x, prev_block, mu, first):
    prev_row = jnp.where(first, 0.0, prev_block[7:8, :])
    rows = lax.broadcasted_iota(jnp.int32, x.shape, 0)
    prev = jnp.where(rows == 0, prev_row, pltpu.roll(x, 1, 0))
    return x + mu * (prev - x)


def _rwkv_mixer_kernel(*refs, with_v):
    it = iter(refs)
    r_ref, k_ref, v_ref, xg_ref, xl_ref = (next(it) for _ in range(5))
    rp_ref, kp_ref, vp_ref, xgp_ref, xlp_ref = (next(it) for _ in range(5))
    mu3_ref, mug_ref, mul_ref, vec_ref, ww_ref, wa_ref = (next(it) for _ in range(6))
    wv_ref = next(it) if with_v else None
    wg_ref = next(it)
    vf_ref = next(it) if with_v else None
    y_ref = next(it)
    vfo_ref = None if with_v else next(it)
    state_ref = next(it)

    first = pl.program_id(2) == 0

    @pl.when(first)
    def _():
        state_ref[...] = jnp.zeros_like(state_ref)

    L = RWKV_CHUNK
    r = _shift_mix(r_ref[0], rp_ref[0], mu3_ref[0:1, :], first)
    k = _shift_mix(k_ref[0], kp_ref[0], mu3_ref[1:2, :], first)
    v = _shift_mix(v_ref[0], vp_ref[0], mu3_ref[2:3, :], first)
    xg = _shift_mix(xg_ref[0], xgp_ref[0], mug_ref[...], first)
    xl = _shift_mix(xl_ref[0], xlp_ref[0], mul_ref[...], first)

    w0, a0, v0 = vec_ref[0:1, :], vec_ref[1:2, :], vec_ref[2:3, :]
    k_k, k_a, r_k = vec_ref[3:4, :], vec_ref[4:5, :], vec_ref[5:6, :]
    gn_g, gn_b = vec_ref[6:7, :], vec_ref[7:8, :]

    lane = lax.broadcasted_iota(jnp.int32, xl.shape, 1)
    lin = jnp.where(lane < LORA_W, jnp.tanh(xl), xl).astype(BF16)
    w_arg = w0 + _dot(lin, ww_ref[...])
    a = jax.nn.sigmoid(a0 + _dot(lin, wa_ref[...]))
    if with_v:
        v = v + (vf_ref[0] - v) * jax.nn.sigmoid(v0 + _dot(lin, wv_ref[...]))
    else:
        vfo_ref[0] = v
    g = _dot(jax.nn.sigmoid(xg).astype(BF16), wg_ref[...])

    w_log = -(jnp.maximum(-w_arg, 0.0) + jnp.log1p(jnp.exp(-jnp.abs(w_arg)))) - 0.5
    lw = -jnp.exp(w_log)
    tri = (lax.broadcasted_iota(jnp.int32, (L, L), 0) >= lax.broadcasted_iota(jnp.int32, (L, L), 1)).astype(BF16)
    h1 = lw.astype(BF16)
    rem = lw - h1.astype(F32)
    h2 = rem.astype(BF16)
    h3 = (rem - h2.astype(F32)).astype(BF16)
    c = _dot(tri, h1) + _dot(tri, h2) + _dot(tri, h3)

    kk_raw = k * k_k
    k2 = k * (1.0 + (a - 1.0) * k_a)
    ones_bd = (lax.broadcasted_iota(jnp.int32, (PAIR, PAIR), 0) // RWKV_HEAD
               == lax.broadcasted_iota(jnp.int32, (PAIR, PAIR), 1) // RWKV_HEAD).astype(BF16)
    inv_n = 1.0 / RWKV_HEAD

    rng = range(RWKV_PAIRS)
    sls = [slice(j * PAIR, (j + 1) * PAIR) for j in rng]
    part = lambda x: [x[:, sl] for sl in sls]
    kkr, rs, ks, vs = part(kk_raw), part(r), part(k2), part(v)
    red = [_dot(jnp.concatenate([kkr[j] * kkr[j], rs[j] * ks[j] * r_k[:, sls[j]]], axis=0).astype(BF16), ones_bd)
           for j in rng]
    kks = [kkr[j] / jnp.maximum(jnp.sqrt(red[j][:L]), 1e-12) for j in rng]
    ys, new_state = _rwkv_chunk(rs, part(c), part(lw), ks, vs, kks, part(a), [state_ref[j] for j in rng])
    for j in rng:
        state_ref[j] = new_state[j]
    yc = [ys[j] - _dot(ys[j].astype(BF16), ones_bd) * inv_n for j in rng]
    var = [_dot((yc[j] * yc[j]).astype(BF16), ones_bd) * inv_n for j in rng]
    for j in rng:
        yn = yc[j] * lax.rsqrt(var[j] + RWKV_GN_EPS) * gn_g[:, sls[j]] + gn_b[:, sls[j]]
        y_ref[0, :, sls[j]] = ((yn + red[j][L:] * vs[j]) * g[:, sls[j]]).astype(y_ref.dtype)


def _rwkv_layout(w_in, mu, with_v):
    sizes = [SEQ_WIDTH] * 3 + [LORA_W, LORA_A] + ([LORA_V] if with_v else []) + [LORA_G]
    offs = np.concatenate([[0], np.cumsum(sizes)])
    seq_cols = int(offs[-1])
    g_idx = len(sizes) - 1
    pad = LORA_PACK - LORA_W - LORA_A - (LORA_V if with_v else 0)
    cols = [w_in[:, :3 * SEQ_WIDTH], w_in[:, seq_cols:], w_in[:, offs[g_idx]:offs[g_idx + 1]],
            w_in[:, offs[3]:offs[g_idx]], jnp.zeros((w_in.shape[0], pad), w_in.dtype)]
    mus = [mu[:3 * SEQ_WIDTH], jnp.zeros((MEM_WIDTH,), mu.dtype), mu[offs[g_idx]:offs[g_idx + 1]],
           mu[offs[3]:offs[g_idx]], jnp.zeros((pad,), mu.dtype)]
    return jnp.concatenate(cols, axis=1), jnp.concatenate(mus)


def _rwkv_mixer(p3, mu, lora_w, lora_a, lora_g, lora_v, vecs, v_first):
    B, S, _ = p3.shape
    L = RWKV_CHUNK
    W = RWKV_PAIRS * PAIR
    with_v = lora_v is not None
    if with_v:
        w0, a0, v0, k_k, k_a, r_k, gn_g, gn_b = vecs
    else:
        w0, a0, k_k, k_a, r_k, gn_g, gn_b = vecs
        v0 = jnp.zeros_like(w0)
    vec8 = jnp.stack([w0, a0, v0, k_k, k_a, r_k, gn_g, gn_b])
    mu3 = jnp.concatenate([mu[:3 * SEQ_WIDTH].reshape(3, SEQ_WIDTH), jnp.zeros((5, SEQ_WIDTH), F32)])
    g_off = Q_MEM_OFF + MEM_WIDTH
    l_off = g_off + LORA_G
    mug = mu[g_off:g_off + LORA_G].reshape(1, LORA_G)
    mul = mu[l_off:l_off + LORA_PACK].reshape(1, LORA_PACK)
    rows = lambda w, lo: jnp.zeros((LORA_PACK, SEQ_WIDTH), F32).at[lo:lo + w.shape[0]].set(w).astype(BF16)

    def cols(width, off):
        blk = off // width
        return pl.BlockSpec((1, L, width), lambda b, g, t: (b, t, blk))

    def cols_prev(width, off):
        blk = off // width
        return pl.BlockSpec((1, 8, width), lambda b, g, t: (b, jnp.maximum(t * (L // 8) - 1, 0), blk))

    def seq(off):
        blk = off // W
        return pl.BlockSpec((1, L, W), lambda b, g, t: (b, t, blk + g))

    def seq_prev(off):
        blk = off // W
        return pl.BlockSpec((1, 8, W), lambda b, g, t: (b, jnp.maximum(t * (L // 8) - 1, 0), blk + g))

    per_group = lambda nrows: pl.BlockSpec((nrows, W), lambda b, g, t: (0, g))
    fixed = lambda ncols: pl.BlockSpec((1, ncols), lambda b, g, t: (0, 0))

    in_specs = [seq(0), seq(SEQ_WIDTH), seq(2 * SEQ_WIDTH), cols(LORA_G, g_off), cols(LORA_PACK, l_off),
                seq_prev(0), seq_prev(SEQ_WIDTH), seq_prev(2 * SEQ_WIDTH),
                cols_prev(LORA_G, g_off), cols_prev(LORA_PACK, l_off),
                per_group(8), fixed(LORA_G), fixed(LORA_PACK), per_group(8),
                per_group(LORA_PACK), per_group(LORA_PACK)]
    args = [p3] * 10 + [mu3, mug, mul, vec8, rows(lora_w, 0), rows(lora_a, LORA_W)]
    if with_v:
        in_specs.append(per_group(LORA_PACK))
        args.append(rows(lora_v, LORA_W + LORA_A))
    in_specs.append(per_group(LORA_G))
    args.append(lora_g.astype(BF16))
    out_spec = pl.BlockSpec((1, L, W), lambda b, g, t: (b, t, g))
    y_shape = jax.ShapeDtypeStruct((B, S, SEQ_WIDTH), BF16)
    if with_v:
        in_specs.append(out_spec)
        args.append(v_first)
        out_shape, out_specs = y_shape, out_spec
    else:
        out_shape = (y_shape, jax.ShapeDtypeStruct((B, S, SEQ_WIDTH), F32))
        out_specs = (out_spec, out_spec)

    res = pl.pallas_call(
        functools.partial(_rwkv_mixer_kernel, with_v=with_v),
        out_shape=out_shape,
        grid=(B, SEQ_WIDTH // W, S // L),
        in_specs=in_specs,
        out_specs=out_specs,
        scratch_shapes=[pltpu.VMEM((RWKV_PAIRS, PAIR, PAIR), F32)],
        compiler_params=_cparams("parallel", "parallel", "arbitrary"),
        name="rwkv_mixer",
    )(*args)
    return (res, v_first) if with_v else res


def kernel(x, mem, positions, mem_w_kv, l0_w_in, l0_mu, l0_lora_w, l0_lora_a, l0_lora_g, l0_vecs, l0_w_out, l0_norms, l0_ffn_gate, l0_ffn_up, l0_ffn_down, l1_w_in, l1_gn, l1_w_out, l1_norms, l1_router, l1_moe_gate, l1_moe_up, l1_moe_down, l2_w_in, l2_mu, l2_lora_w, l2_lora_a, l2_lora_v, l2_lora_g, l2_vecs, l2_w_out, l2_norms, l2_ffn_gate, l2_ffn_up, l2_ffn_down, l3_w_in, l3_gn, l3_w_out, l3_norms, l3_router, l3_moe_gate, l3_moe_up, l3_moe_down):
    B, S, D = x.shape
    T = B * S
    M = mem.shape[1]
    mem_kv = _matmul(mem.reshape(B * M, D), mem_w_kv.astype(BF16), tm=1024, tn=512,
                     out_dtype=BF16).reshape(B, M, 2 * MEM_WIDTH)

    rwkv_layers = {
        0: (l0_w_in, l0_mu, l0_lora_w, l0_lora_a, l0_lora_g, None, l0_vecs),
        2: (l2_w_in, l2_mu, l2_lora_w, l2_lora_a, l2_lora_g, l2_lora_v, l2_vecs),
    }
    ret_layers = {1: (l1_w_in, l1_gn), 3: (l3_w_in, l3_gn)}
    w_outs = (l0_w_out, l1_w_out, l2_w_out, l3_w_out)
    norms = (l0_norms, l1_norms, l2_norms, l3_norms)
    dense = {0: (l0_ffn_gate, l0_ffn_up, l0_ffn_down), 2: (l2_ffn_gate, l2_ffn_up, l2_ffn_down)}
    moe = {1: (l1_router, l1_moe_gate, l1_moe_up, l1_moe_down),
           3: (l3_router, l3_moe_gate, l3_moe_up, l3_moe_down)}

    xf = x.reshape(T, D)
    xin = xf
    v_first = None
    for i in range(DEPTH):
        if i in rwkv_layers:
            w_in, mu, lw_, la_, lg_, lv_, vecs = rwkv_layers[i]
            w_in, mu = _rwkv_layout(w_in, mu, lv_ is not None)
            p3 = _matmul(xin, w_in.astype(BF16), tm=1024, tn=512).reshape(B, S, -1)
            y_seq, v_first = _rwkv_mixer(p3, mu, lw_, la_, lg_, lv_, vecs, v_first)
        else:
            w_in, gn = ret_layers[i]
            p3 = _matmul(xin, w_in.astype(BF16), tm=1024, tn=512, out_dtype=BF16).reshape(B, S, -1)
            y_seq = _retention_mixer(p3, positions, gn)
        y_mem = _memory_attention(p3, mem_kv)
        nrm = norms[i]
        xf, xb = _proj_ln(y_seq.reshape(T, SEQ_WIDTH), y_mem.reshape(T, MEM_WIDTH), w_outs[i], xf,
                          nrm[0], nrm[1])
        if i in dense:
            xf, xb = _ffn_ln(xf, xb, *dense[i], nrm[2], nrm[3])
        else:
            xf, xb = _moe_ln(xf, xb, *moe[i], nrm[2], nrm[3])
        xin = xb
    return xf.reshape(B, S, D)
```

```python
import functools

import numpy as np
import jax
import jax.numpy as jnp
from jax import lax
from jax.experimental import pallas as pl
from jax.experimental.pallas import tpu as pltpu

F32 = jnp.float32
BF16 = jnp.bfloat16

D_MODEL = 2048
DEPTH = 4
MEM_HEADS = 4
MEM_HEAD_DIM = 128
MEM_WIDTH = MEM_HEADS * MEM_HEAD_DIM
SEQ_WIDTH = D_MODEL - MEM_WIDTH

RWKV_HEAD = 64
RWKV_HEADS = SEQ_WIDTH // RWKV_HEAD
LORA_W = 96
LORA_A = 96
LORA_V = 64
LORA_G = 256
LORA_PACK = 256
RWKV_GN_EPS = 1e-5 * RWKV_HEAD
RWKV_CHUNK = 64

RET_HEADS = 6
RET_DV = SEQ_WIDTH // RET_HEADS
RET_DK = RET_DV // 2
RET_CHUNK = 128
ROPE_BASE = 10000.0

N_EXPERTS = 8
TOP_K = 2

ALPHA = (2.0 * DEPTH) ** 0.25
LN_EPS = 1e-5

LANES = 128
Q_MEM_OFF = 3 * SEQ_WIDTH
VMEM_LIMIT = 52 * 1024 * 1024
MOE_VMEM_LIMIT = 58 * 1024 * 1024


def _cparams(*semantics):
    return pltpu.CompilerParams(dimension_semantics=semantics, vmem_limit_bytes=VMEM_LIMIT)


def _dot(a, b):
    return jnp.dot(a, b, preferred_element_type=F32)


def _dot_nt(a, b):
    return lax.dot_general(a, b, (((1,), (1,)), ((), ())), preferred_element_type=F32)


def _dot_tn(a, b):
    return lax.dot_general(a, b, (((0,), (0,)), ((), ())), preferred_element_type=F32)


def _layer_norm_rows(z, g, b):
    mu = jnp.mean(z, axis=-1, keepdims=True)
    zc = z - mu
    var = jnp.mean(zc * zc, axis=-1, keepdims=True)
    return zc * lax.rsqrt(var + LN_EPS) * g + b


def _mm_kernel(a_ref, w_ref, o_ref, abf_ref):
    @pl.when(pl.program_id(1) == 0)
    def _():
        abf_ref[...] = a_ref[...].astype(BF16)

    o_ref[...] = _dot(abf_ref[...], w_ref[...]).astype(o_ref.dtype)


def _matmul(a, w, *, tm, tn, out_dtype=F32):
    M, K = a.shape
    N = w.shape[1]
    tm = min(tm, M)
    assert M % tm == 0 and N % tn == 0
    return pl.pallas_call(
        _mm_kernel,
        out_shape=jax.ShapeDtypeStruct((M, N), out_dtype),
        grid=(M // tm, N // tn),
        in_specs=[pl.BlockSpec((tm, K), lambda i, j: (i, 0)),
                  pl.BlockSpec((K, tn), lambda i, j: (0, j))],
        out_specs=pl.BlockSpec((tm, tn), lambda i, j: (i, j)),
        scratch_shapes=[pltpu.VMEM((tm, K), BF16)],
        compiler_params=_cparams("parallel", "arbitrary"),
        name="matmul",
    )(a, w)


def _router_kernel(x_ref, w_ref, o_ref):
    o_ref[...] = jnp.dot(x_ref[...], w_ref[...], preferred_element_type=F32,
                         precision=lax.Precision.HIGHEST)


def _router_logits(x, w_router):
    T = x.shape[0]
    tm = min(512, T)
    wp = jnp.pad(w_router, ((0, 0), (0, LANES - N_EXPERTS)))
    out = pl.pallas_call(
        _router_kernel,
        out_shape=jax.ShapeDtypeStruct((T, LANES), F32),
        grid=(T // tm,),
        in_specs=[pl.BlockSpec((tm, D_MODEL), lambda i: (i, 0)),
                  pl.BlockSpec((D_MODEL, LANES), lambda i: (0, 0))],
        out_specs=pl.BlockSpec((tm, LANES), lambda i: (i, 0)),
        compiler_params=_cparams("parallel"),
        name="router",
    )(x, wp)
    return out[:, :N_EXPERTS]


def _proj_ln_kernel(ys_ref, ym_ref, ws_ref, wm_ref, x_ref, g_ref, b_ref, o_ref, ob_ref):
    half = x_ref.shape[0] // 2
    for s in range(2):
        rs = slice(s * half, (s + 1) * half)
        h = _dot(ys_ref[rs, :], ws_ref[...]) + _dot(ym_ref[rs, :], wm_ref[...])
        out = _layer_norm_rows(ALPHA * x_ref[rs, :] + h, g_ref[...], b_ref[...])
        o_ref[rs, :] = out
        ob_ref[rs, :] = out.astype(BF16)


def _proj_ln(y_seq, y_mem, w_out, x, gamma, beta):
    T = x.shape[0]
    tm = min(512, T)
    ws = w_out[:SEQ_WIDTH].astype(BF16)
    wm = w_out[SEQ_WIDTH:].astype(BF16)
    row = lambda i: (i, 0)
    fixed = lambda i: (0, 0)
    once = pl.Buffered(1)
    return pl.pallas_call(
        _proj_ln_kernel,
        out_shape=(jax.ShapeDtypeStruct((T, D_MODEL), F32), jax.ShapeDtypeStruct((T, D_MODEL), BF16)),
        grid=(T // tm,),
        in_specs=[pl.BlockSpec((tm, SEQ_WIDTH), row), pl.BlockSpec((tm, MEM_WIDTH), row),
                  pl.BlockSpec((SEQ_WIDTH, D_MODEL), fixed, pipeline_mode=once),
                  pl.BlockSpec((MEM_WIDTH, D_MODEL), fixed, pipeline_mode=once),
                  pl.BlockSpec((tm, D_MODEL), row),
                  pl.BlockSpec((1, D_MODEL), fixed), pl.BlockSpec((1, D_MODEL), fixed)],
        out_specs=(pl.BlockSpec((tm, D_MODEL), row), pl.BlockSpec((tm, D_MODEL), row)),
        compiler_params=_cparams("parallel"),
        name="proj_ln",
    )(y_seq, y_mem, ws, wm, x, gamma.reshape(1, -1), beta.reshape(1, -1))


def _swiglu_partial(xb, wg, wu, wd):
    gate = _dot(xb, wg)
    up = _dot(xb, wu)
    h = (gate * jax.nn.sigmoid(gate) * up).astype(BF16)
    return _dot(h, wd)


def _ffn_ln_kernel(xb_ref, wg_ref, wu_ref, wd_ref, x_ref, g_ref, b_ref, o_ref, ob_ref, acc_ref):
    f = pl.program_id(1)

    @pl.when(f == 0)
    def _():
        acc_ref[...] = jnp.zeros_like(acc_ref)

    acc_ref[...] += _swiglu_partial(xb_ref[...], wg_ref[...], wu_ref[...], wd_ref[...])

    @pl.when(f == pl.num_programs(1) - 1)
    def _():
        out = _layer_norm_rows(ALPHA * x_ref[...] + acc_ref[...], g_ref[...], b_ref[...])
        o_ref[...] = out
        ob_ref[...] = out.astype(BF16)


def _ffn_ln(x, xb, w_gate, w_up, w_down, gamma, beta):
    T = x.shape[0]
    F = w_gate.shape[1]
    tm = min(512, T)
    tf = 512
    row = lambda i, f: (i, 0)
    fixed = lambda i, f: (0, 0)
    return pl.pallas_call(
        _ffn_ln_kernel,
        out_shape=(jax.ShapeDtypeStruct((T, D_MODEL), F32), jax.ShapeDtypeStruct((T, D_MODEL), BF16)),
        grid=(T // tm, F // tf),
        in_specs=[pl.BlockSpec((tm, D_MODEL), row),
                  pl.BlockSpec((D_MODEL, tf), lambda i, f: (0, f)),
                  pl.BlockSpec((D_MODEL, tf), lambda i, f: (0, f)),
                  pl.BlockSpec((tf, D_MODEL), lambda i, f: (f, 0)),
                  pl.BlockSpec((tm, D_MODEL), row),
                  pl.BlockSpec((1, D_MODEL), fixed), pl.BlockSpec((1, D_MODEL), fixed)],
        out_specs=(pl.BlockSpec((tm, D_MODEL), row), pl.BlockSpec((tm, D_MODEL), row)),
        scratch_shapes=[pltpu.VMEM((tm, D_MODEL), F32)],
        compiler_params=_cparams("parallel", "arbitrary"),
        name="ffn_ln",
    )(xb, w_gate.astype(BF16), w_up.astype(BF16), w_down.astype(BF16), x,
      gamma.reshape(1, -1), beta.reshape(1, -1))


MOE_TM = 1024
MOE_TF = 256
MOE_RUNS = 4


def _moe_kernel(te_ref, nu_ref, xs_ref, wg_ref, wu_ref, wd_ref, *rest):
    o_ref, acc_ref = rest[-2:]
    i = pl.program_id(0)
    f = pl.program_id(1)
    last = pl.num_programs(1) - 1
    used = i < nu_ref[0]

    @pl.when(jnp.logical_and(used, f == 0))
    def _():
        acc_ref[...] = jnp.zeros_like(acc_ref)

    @pl.when(used)
    def _():
        acc_ref[...] += _swiglu_partial(xs_ref[...], wg_ref[...].astype(BF16), wu_ref[...].astype(BF16),
                                        wd_ref[...].astype(BF16))

    @pl.when(jnp.logical_and(used, f == last))
    def _():
        o_ref[...] = acc_ref[...].astype(o_ref.dtype)

    @pl.when(jnp.logical_and(jnp.logical_not(used), f == last))
    def _():
        o_ref[...] = jnp.zeros_like(o_ref)


def _moe_grouped(xs, tile_expert, n_used, w_gate, w_up, w_down, ys_prev, tile0, total_tiles):
    tiles = xs.shape[0] // MOE_TM
    F = w_gate.shape[2]
    nf = F // MOE_TF

    def fcol(i, f, nu):
        return jnp.where(i < nu[0], f, nf - 1)

    in_specs = [pl.BlockSpec((MOE_TM, D_MODEL), lambda i, f, te, nu: (i, 0)),
                pl.BlockSpec((None, D_MODEL, MOE_TF), lambda i, f, te, nu: (te[i], 0, fcol(i, f, nu))),
                pl.BlockSpec((None, D_MODEL, MOE_TF), lambda i, f, te, nu: (te[i], 0, fcol(i, f, nu))),
                pl.BlockSpec((None, MOE_TF, D_MODEL), lambda i, f, te, nu: (te[i], fcol(i, f, nu), 0))]
    args = [tile_expert, n_used, xs, w_gate, w_up, w_down]
    aliases = {}
    if ys_prev is not None:
        in_specs.append(pl.BlockSpec(memory_space=pl.ANY))
        aliases = {len(args): 0}
        args.append(ys_prev)
    grid_spec = pltpu.PrefetchScalarGridSpec(
        num_scalar_prefetch=2,
        grid=(tiles, nf),
        in_specs=in_specs,
        out_specs=pl.BlockSpec((MOE_TM, D_MODEL), lambda i, f, te, nu: (tile0 + i, 0)),
        scratch_shapes=[pltpu.VMEM((MOE_TM, D_MODEL), F32)],
    )
    return pl.pallas_call(
        _moe_kernel,
        out_shape=jax.ShapeDtypeStruct((total_tiles * MOE_TM, D_MODEL), BF16),
        grid_spec=grid_spec,
        input_output_aliases=aliases,
        compiler_params=pltpu.CompilerParams(dimension_semantics=("arbitrary", "arbitrary"),
                                             vmem_limit_bytes=MOE_VMEM_LIMIT),
        name="moe_grouped",
    )(*args)


def _combine_ln_kernel(x_ref, ya_ref, yb_ref, w_ref, g_ref, b_ref, o_ref, ob_ref):
    w = w_ref[...]
    f = w[:, 0:1] * ya_ref[...].astype(F32) + w[:, 1:2] * yb_ref[...].astype(F32)
    out = _layer_norm_rows(ALPHA * x_ref[...] + f, g_ref[...], b_ref[...])
    o_ref[...] = out
    ob_ref[...] = out.astype(BF16)


def _combine_ln(x, y_rows, top_w, gamma, beta):
    T = x.shape[0]
    tm = min(512, T)
    nt = T // tm
    wpad = jnp.pad(top_w, ((0, 0), (0, LANES - TOP_K)))
    fixed = lambda i: (0, 0)
    row = lambda i: (i, 0)
    return pl.pallas_call(
        _combine_ln_kernel,
        out_shape=(jax.ShapeDtypeStruct((T, D_MODEL), F32), jax.ShapeDtypeStruct((T, D_MODEL), BF16)),
        grid=(nt,),
        in_specs=[pl.BlockSpec((tm, D_MODEL), row),
                  pl.BlockSpec((tm, D_MODEL), row),
                  pl.BlockSpec((tm, D_MODEL), lambda i: (i + nt, 0)),
                  pl.BlockSpec((tm, LANES), row),
                  pl.BlockSpec((1, D_MODEL), fixed), pl.BlockSpec((1, D_MODEL), fixed)],
        out_specs=(pl.BlockSpec((tm, D_MODEL), row), pl.BlockSpec((tm, D_MODEL), row)),
        compiler_params=_cparams("parallel"),
        name="combine_ln",
    )(x, y_rows, y_rows, wpad, gamma.reshape(1, -1), beta.reshape(1, -1))


def _moe_ln(x, xb, w_router, w_gate, w_up, w_down, gamma, beta):
    T = x.shape[0]
    logits = _router_logits(x, w_router)
    top_val, top_idx = lax.top_k(logits, TOP_K)
    top_w = jax.nn.softmax(top_val, axis=-1)

    e_flat = top_idx.reshape(-1).astype(jnp.int32)
    onehot = (e_flat[:, None] == jnp.arange(N_EXPERTS, dtype=jnp.int32)[None, :]).astype(jnp.int32)
    rank = jnp.sum((jnp.cumsum(onehot, axis=0) - onehot) * onehot, axis=1)
    counts = jnp.sum(onehot, axis=0)
    padded = ((counts + MOE_TM - 1) // MOE_TM) * MOE_TM
    ends = jnp.cumsum(padded)
    starts = ends - padded
    dest = starts[e_flat] + rank

    Tp = TOP_K * T + N_EXPERTS * MOE_TM
    n_tiles = Tp // MOE_TM
    token = jnp.arange(TOP_K * T, dtype=jnp.int32) // TOP_K
    src = jnp.zeros((Tp,), jnp.int32).at[dest].set(token)
    n_used = (ends[-1] // MOE_TM).astype(jnp.int32).reshape(1)
    tile_start = jnp.arange(n_tiles, dtype=jnp.int32) * MOE_TM
    tile_expert = jnp.sum((tile_start[:, None] >= ends[None, :]).astype(jnp.int32), axis=1)
    last_expert = jnp.sum((jnp.maximum(ends[-1] - 1, 0) >= ends).astype(jnp.int32))
    tile_expert = jnp.minimum(tile_expert, last_expert).astype(jnp.int32)

    runs = max(d for d in range(1, MOE_RUNS + 1) if n_tiles % d == 0)
    tiles = n_tiles // runs
    ys = None
    for c in range(runs):
        xs = xb.at[src[c * tiles * MOE_TM:(c + 1) * tiles * MOE_TM]].get(mode="promise_in_bounds")
        ys = _moe_grouped(xs, tile_expert[c * tiles:(c + 1) * tiles], jnp.clip(n_used - c * tiles, 0, tiles),
                          w_gate, w_up, w_down, ys, c * tiles, n_tiles)
    slot_major = dest.reshape(T, TOP_K).T.reshape(-1)
    y_rows = ys.at[slot_major].get(mode="promise_in_bounds")
    return _combine_ln(x, y_rows, top_w, gamma, beta)


def _memattn_kernel(q_ref, k_ref, v_ref, o_ref):
    q = q_ref[0]
    outs = []
    for h in range(MEM_HEADS):
        sl = slice(h * MEM_HEAD_DIM, (h + 1) * MEM_HEAD_DIM)
        s = _dot_nt(q[:, sl].astype(BF16), k_ref[0, :, sl]) * (MEM_HEAD_DIM ** -0.5)
        e = jnp.exp(s - jnp.max(s, axis=-1, keepdims=True))
        p = e / jnp.sum(e, axis=-1, keepdims=True)
        outs.append(_dot(p.astype(BF16), v_ref[0, :, sl]))
    o_ref[0] = jnp.concatenate(outs, axis=-1).astype(o_ref.dtype)


def _memory_attention(p3, mem_kv):
    B, S, _ = p3.shape
    M = mem_kv.shape[1]
    ts = min(512, S)
    return pl.pallas_call(
        _memattn_kernel,
        out_shape=jax.ShapeDtypeStruct((B, S, MEM_WIDTH), BF16),
        grid=(B, S // ts),
        in_specs=[pl.BlockSpec((1, ts, MEM_WIDTH), lambda b, s: (b, s, Q_MEM_OFF // MEM_WIDTH)),
                  pl.BlockSpec((1, M, MEM_WIDTH), lambda b, s: (b, 0, 0)),
                  pl.BlockSpec((1, M, MEM_WIDTH), lambda b, s: (b, 0, 1))],
        out_specs=pl.BlockSpec((1, ts, MEM_WIDTH), lambda b, s: (b, s, 0)),
        compiler_params=_cparams("parallel", "parallel"),
        name="mem_attn",
    )(p3, mem_kv, mem_kv)


def _ret_kernel(q_ref, k_ref, v_ref, g_ref, cos_ref, sin_ref, intra_ref, qd_ref, kd_ref, cd_ref,
                gn_ref, o_ref, state_ref):
    @pl.when(pl.program_id(1) == 0)
    def _():
        state_ref[...] = jnp.zeros_like(state_ref)

    cosf = cos_ref[0]
    sinf = sin_ref[0]
    half = RET_DK // 2
    rot = lambda x: x * cosf + pltpu.roll(x, half, 1) * sinf
    hs = range(RET_HEADS)
    ksl = [slice(h * RET_DK, (h + 1) * RET_DK) for h in hs]
    vsl = [slice(h * RET_DV, (h + 1) * RET_DV) for h in hs]

    qr = [rot(q_ref[0, :, ksl[h]].astype(F32)) for h in hs]
    kr = [rot(k_ref[0, :, ksl[h]].astype(F32)) * (RET_DK ** -0.5) for h in hs]
    vb = [v_ref[0, :, vsl[h]].astype(BF16) for h in hs]
    state = [state_ref[h] for h in hs]
    s = [_dot_nt(qr[h].astype(BF16), kr[h].astype(BF16)) * intra_ref[h] for h in hs]
    o = [_dot(s[h].astype(BF16), vb[h]) + _dot((qr[h] * qd_ref[h]).astype(BF16), state[h].astype(BF16))
         for h in hs]
    for h in hs:
        state_ref[h] = state[h] * cd_ref[h] + _dot_tn((kr[h] * kd_ref[h]).astype(BF16), vb[h])
    for h in hs:
        mu = jnp.mean(o[h], axis=-1, keepdims=True)
        oc = o[h] - mu
        var = jnp.mean(oc * oc, axis=-1, keepdims=True)
        y = oc * lax.rsqrt(var + LN_EPS) * gn_ref[0:1, vsl[h]] + gn_ref[1:2, vsl[h]]
        g = g_ref[0, :, vsl[h]].astype(F32)
        o_ref[0, :, vsl[h]] = (y * (g * jax.nn.sigmoid(g))).astype(o_ref.dtype)


def _retention_mixer(p3, positions, gn):
    B, S, _ = p3.shape
    C = RET_CHUNK
    H = RET_HEADS
    inv = 1.0 / (ROPE_BASE ** (jnp.arange(0, RET_DK, 2, dtype=F32) / RET_DK))
    ang = positions.astype(F32)[..., None] * inv
    cos, sin = jnp.cos(ang), jnp.sin(ang)
    cosf = jnp.concatenate([cos, cos], axis=-1)
    sinf = jnp.concatenate([-sin, sin], axis=-1)

    log_g = jnp.log1p(-jnp.exp2(-5.0 - jnp.arange(H, dtype=F32)))
    idx = jnp.arange(C, dtype=F32)
    diff = idx[:, None] - idx[None, :]
    intra = jnp.where(diff >= 0, jnp.exp(jnp.maximum(diff, 0.0)[None] * log_g[:, None, None]), 0.0)
    qd = jnp.broadcast_to(jnp.exp((idx + 1.0)[None, :, None] * log_g[:, None, None]), (H, C, RET_DK))
    kd = jnp.broadcast_to(jnp.exp((C - 1.0 - idx)[None, :, None] * log_g[:, None, None]), (H, C, RET_DK))
    cd = jnp.broadcast_to(jnp.exp(C * log_g)[:, None, None], (H, 1, RET_DV))

    QK = H * RET_DK
    assert 2 * QK == SEQ_WIDTH
    whole = lambda b, c: (0, 0, 0)
    return pl.pallas_call(
        _ret_kernel,
        out_shape=jax.ShapeDtypeStruct((B, S, SEQ_WIDTH), BF16),
        grid=(B, S // C),
        in_specs=[pl.BlockSpec((1, C, QK), lambda b, c: (b, c, 0)),
                  pl.BlockSpec((1, C, QK), lambda b, c: (b, c, 1)),
                  pl.BlockSpec((1, C, SEQ_WIDTH), lambda b, c: (b, c, 1)),
                  pl.BlockSpec((1, C, SEQ_WIDTH), lambda b, c: (b, c, 2)),
                  pl.BlockSpec((1, C, RET_DK), lambda b, c: (b, c, 0)),
                  pl.BlockSpec((1, C, RET_DK), lambda b, c: (b, c, 0)),
                  pl.BlockSpec((H, C, C), whole),
                  pl.BlockSpec((H, C, RET_DK), whole),
                  pl.BlockSpec((H, C, RET_DK), whole),
                  pl.BlockSpec((H, 1, RET_DV), whole),
                  pl.BlockSpec((2, SEQ_WIDTH), lambda b, c: (0, 0))],
        out_specs=pl.BlockSpec((1, C, SEQ_WIDTH), lambda b, c: (b, c, 0)),
        scratch_shapes=[pltpu.VMEM((H, RET_DK, RET_DV), F32)],
        compiler_params=_cparams("parallel", "arbitrary"),
        name="retention",
    )(p3, p3, p3, p3, cosf, sinf, intra, qd, kd, cd, gn)


RWKV_PAIRS = 12
PAIR = 2 * RWKV_HEAD

_NN = ((1,), (0,))
_NT = ((1,), (1,))
_TN = ((0,), (0,))


def _dg(a, b, dims):
    return lax.dot_general(a.astype(BF16), b.astype(BF16), (dims, ((), ())), preferred_element_type=F32)


def _rwkv_masks():
    L = RWKV_CHUNK
    row = np.arange(L)[:, None]
    lane = np.arange(PAIR)[None, :]
    m1 = np.broadcast_to(lane < RWKV_HEAD, (L, PAIR))
    small = np.stack([m1, ~m1, row > lane % L, row >= lane % L, row == lane % L]).astype(np.float32)
    same_head = (np.arange(2 * L)[:, None] // L == lane // L).astype(np.float32)
    tri = (row >= np.arange(L)[None, :]).astype(np.float32)
    return (jnp.asarray(small), jnp.asarray(same_head), jnp.asarray(same_head, dtype=BF16),
            jnp.asarray(tri, dtype=BF16))


def _rwkv_chunk(r, c, lw, k, v, kk, a, state, masks):
    n = len(r)
    rng = range(n)
    L = r[0].shape[0]
    assert L == RWKV_HEAD
    m1, m2, strict, incl, eye, same_head = masks
    stack = lambda x: jnp.concatenate([x * m1, x * m2], axis=0)
    blockdiag = lambda x: (jnp.concatenate([x, x], axis=0) * same_head).astype(BF16)

    p_incl = [jnp.exp(c[j]) for j in rng]
    p_inv = [jnp.exp(-c[j]) for j in rng]
    at = [-kk[j] * jnp.exp(c[j] - lw[j]) for j in rng]
    rt = [r[j] * p_incl[j] for j in rng]
    lhs = [jnp.concatenate([at[j], rt[j]], axis=0).astype(BF16) for j in rng]
    rhs = [jnp.concatenate([stack(kk[j] * a[j] * p_inv[j]), stack(k[j] * p_inv[j])], axis=0).astype(BF16)
           for j in rng]
    sc = [_dg(lhs[j], rhs[j], _NT) for j in rng]
    nmat = [sc[j][:L, :PAIR] * strict for j in rng]
    tinv = [eye + nmat[j] for j in rng]
    pw = [_dg(nmat[j], blockdiag(nmat[j]), _NN) for j in rng]
    rounds = int(np.log2(L)) - 1
    for i in range(rounds):
        if i < rounds - 1:
            both = [_dg(jnp.concatenate([pw[j], tinv[j]], axis=0), blockdiag(pw[j]), _NN) for j in rng]
            pw = [both[j][:L] for j in rng]
            tinv = [tinv[j] + both[j][L:] for j in rng]
        else:
            tinv = [tinv[j] + _dg(tinv[j], blockdiag(pw[j]), _NN) for j in rng]

    vs = [stack(v[j]).astype(BF16) for j in rng]
    sg = [_dg(lhs[j], state[j], _NT) for j in rng]
    mv = [_dg(sc[j][:L, PAIR:] * strict, vs[j], _NN) for j in rng]
    u = [_dg(tinv[j], stack(sg[j][:L] + mv[j]), _NN) for j in rng]
    uv = [jnp.concatenate([stack(u[j]).astype(BF16), vs[j]], axis=0) for j in rng]
    ys = [_dg(sc[j][L:, :] * jnp.concatenate([incl, incl], axis=1), uv[j], _NN) for j in rng]
    y = [sg[j][L:] + ys[j] for j in rng]
    new_state = [(state[j] + _dg(uv[j], rhs[j], _TN)) * p_incl[j][L - 1:L, :] for j in rng]
    return y, new_state


def _shift_mix(x, prev_block, mu, first):
    prev_row = jnp.where(first, 0.0, prev_block[7:8, :])
    rows = lax.broadcasted_iota(jnp.int32, x.shape, 0)
    prev = jnp.where(rows == 0, prev_row, pltpu.roll(x, 1, 0))
    return x + mu * (prev - x)


def _rwkv_mixer_kernel(*refs, with_v):
    it = iter(refs)
    r_ref, k_ref, v_ref, xg_ref, xl_ref = (next(it) for _ in range(5))
    rp_ref, kp_ref, vp_ref, xgp_ref, xlp_ref = (next(it) for _ in range(5))
    mu3_ref, mug_ref, mul_ref, vec_ref = (next(it) for _ in range(4))
    small_ref, same_head_ref, ones_ref, tri_ref = (next(it) for _ in range(4))
    ww_ref, wa_ref = next(it), next(it)
    wv_ref = next(it) if with_v else None
    wg_ref = next(it)
    vf_ref = next(it) if with_v else None
    y_ref = next(it)
    vfo_ref = None if with_v else next(it)
    state_ref = next(it)

    first = pl.program_id(2) == 0

    @pl.when(first)
    def _():
        state_ref[...] = jnp.zeros_like(state_ref)

    L = RWKV_CHUNK
    r = _shift_mix(r_ref[0], rp_ref[0], mu3_ref[0:1, :], first)
    k = _shift_mix(k_ref[0], kp_ref[0], mu3_ref[1:2, :], first)
    v = _shift_mix(v_ref[0], vp_ref[0], mu3_ref[2:3, :], first)
    xg = _shift_mix(xg_ref[0], xgp_ref[0], mug_ref[...], first)
    xl = _shift_mix(xl_ref[0], xlp_ref[0], mul_ref[...], first)

    w0, a0, v0 = vec_ref[0:1, :], vec_ref[1:2, :], vec_ref[2:3, :]
    k_k, k_a, r_k = vec_ref[3:4, :], vec_ref[4:5, :], vec_ref[5:6, :]
    gn_g, gn_b = vec_ref[6:7, :], vec_ref[7:8, :]

    lane = lax.broadcasted_iota(jnp.int32, xl.shape, 1)
    lin = jnp.where(lane < LORA_W, jnp.tanh(xl), xl).astype(BF16)
    w_arg = w0 + _dot(lin, ww_ref[...])
    a = jax.nn.sigmoid(a0 + _dot(lin, wa_ref[...]))
    if with_v:
        v = v + (vf_ref[0] - v) * jax.nn.sigmoid(v0 + _dot(lin, wv_ref[...]))
    else:
        vfo_ref[0] = v
    g = _dot(jax.nn.sigmoid(xg).astype(BF16), wg_ref[...])

    w_log = -(jnp.maximum(-w_arg, 0.0) + jnp.log1p(jnp.exp(-jnp.abs(w_arg)))) - 0.5
    lw = -jnp.exp(w_log)
    tri = tri_ref[...]
    h1 = lw.astype(BF16)
    rem = lw - h1.astype(F32)
    h2 = rem.astype(BF16)
    h3 = (rem - h2.astype(F32)).astype(BF16)
    c = _dot(tri, h1) + _dot(tri, h2) + _dot(tri, h3)

    kk_raw = k * k_k
    k2 = k * (1.0 + (a - 1.0) * k_a)
    ones_bd = ones_ref[...]
    masks = tuple(small_ref[i] for i in range(5)) + (same_head_ref[...],)
    inv_n = 1.0 / RWKV_HEAD

    rng = range(RWKV_PAIRS)
    sls = [slice(j * PAIR, (j + 1) * PAIR) for j in rng]
    part = lambda x: [x[:, sl] for sl in sls]
    kkr, rs, ks, vs = part(kk_raw), part(r), part(k2), part(v)
    red = [_dot(jnp.concatenate([kkr[j] * kkr[j], rs[j] * ks[j] * r_k[:, sls[j]]], axis=0).astype(BF16), ones_bd)
           for j in rng]
    kks = [kkr[j] / jnp.maximum(jnp.sqrt(red[j][:L]), 1e-12) for j in rng]
    ys, new_state = _rwkv_chunk(rs, part(c), part(lw), ks, vs, kks, part(a), [state_ref[j] for j in rng], masks)
    for j in rng:
        state_ref[j] = new_state[j]
    yc = [ys[j] - _dot(ys[j].astype(BF16), ones_bd) * inv_n for j in rng]
    var = [_dot((yc[j] * yc[j]).astype(BF16), ones_bd) * inv_n for j in rng]
    for j in rng:
        yn = yc[j] * lax.rsqrt(var[j] + RWKV_GN_EPS) * gn_g[:, sls[j]] + gn_b[:, sls[j]]
        y_ref[0, :, sls[j]] = ((yn + red[j][L:] * vs[j]) * g[:, sls[j]]).astype(y_ref.dtype)


def _rwkv_layout(w_in, mu, with_v):
    sizes = [SEQ_WIDTH] * 3 + [LORA_W, LORA_A] + ([LORA_V] if with_v else []) + [LORA_G]
    offs = np.concatenate([[0], np.cumsum(sizes)])
    seq_cols = int(offs[-1])
    g_idx = len(sizes) - 1
    pad = LORA_PACK - LORA_W - LORA_A - (LORA_V if with_v else 0)
    cols = [w_in[:, :3 * SEQ_WIDTH], w_in[:, seq_cols:], w_in[:, offs[g_idx]:offs[g_idx + 1]],
            w_in[:, offs[3]:offs[g_idx]], jnp.zeros((w_in.shape[0], pad), w_in.dtype)]
    mus = [mu[:3 * SEQ_WIDTH], jnp.zeros((MEM_WIDTH,), mu.dtype), mu[offs[g_idx]:offs[g_idx + 1]],
           mu[offs[3]:offs[g_idx]], jnp.zeros((pad,), mu.dtype)]
    return jnp.concatenate(cols, axis=1), jnp.concatenate(mus)


def _rwkv_mixer(p3, mu, lora_w, lora_a, lora_g, lora_v, vecs, v_first):
    B, S, _ = p3.shape
    L = RWKV_CHUNK
    W = RWKV_PAIRS * PAIR
    with_v = lora_v is not None
    if with_v:
        w0, a0, v0, k_k, k_a, r_k, gn_g, gn_b = vecs
    else:
        w0, a0, k_k, k_a, r_k, gn_g, gn_b = vecs
        v0 = jnp.zeros_like(w0)
    vec8 = jnp.stack([w0, a0, v0, k_k, k_a, r_k, gn_g, gn_b])
    mu3 = jnp.concatenate([mu[:3 * SEQ_WIDTH].reshape(3, SEQ_WIDTH), jnp.zeros((5, SEQ_WIDTH), F32)])
    g_off = Q_MEM_OFF + MEM_WIDTH
    l_off = g_off + LORA_G
    mug = mu[g_off:g_off + LORA_G].reshape(1, LORA_G)
    mul = mu[l_off:l_off + LORA_PACK].reshape(1, LORA_PACK)
    rows = lambda w, lo: jnp.zeros((LORA_PACK, SEQ_WIDTH), F32).at[lo:lo + w.shape[0]].set(w).astype(BF16)

    def cols(width, off):
        blk = off // width
        return pl.BlockSpec((1, L, width), lambda b, g, t: (b, t, blk))

    def cols_prev(width, off):
        blk = off // width
        return pl.BlockSpec((1, 8, width), lambda b, g, t: (b, jnp.maximum(t * (L // 8) - 1, 0), blk))

    def seq(off):
        blk = off // W
        return pl.BlockSpec((1, L, W), lambda b, g, t: (b, t, blk + g))

    def seq_prev(off):
        blk = off // W
        return pl.BlockSpec((1, 8, W), lambda b, g, t: (b, jnp.maximum(t * (L // 8) - 1, 0), blk + g))

    per_group = lambda nrows: pl.BlockSpec((nrows, W), lambda b, g, t: (0, g))
    fixed = lambda ncols: pl.BlockSpec((1, ncols), lambda b, g, t: (0, 0))

    in_specs = [seq(0), seq(SEQ_WIDTH), seq(2 * SEQ_WIDTH), cols(LORA_G, g_off), cols(LORA_PACK, l_off),
                seq_prev(0), seq_prev(SEQ_WIDTH), seq_prev(2 * SEQ_WIDTH),
                cols_prev(LORA_G, g_off), cols_prev(LORA_PACK, l_off),
                per_group(8), fixed(LORA_G), fixed(LORA_PACK), per_group(8)]
    masks = _rwkv_masks()
    in_specs += [pl.BlockSpec(m.shape, lambda b, g, t, nd=m.ndim: (0,) * nd) for m in masks]
    in_specs += [per_group(LORA_PACK), per_group(LORA_PACK)]
    args = [p3] * 10 + [mu3, mug, mul, vec8, *masks, rows(lora_w, 0), rows(lora_a, LORA_W)]
    if with_v:
        in_specs.append(per_group(LORA_PACK))
        args.append(rows(lora_v, LORA_W + LORA_A))
    in_specs.append(per_group(LORA_G))
    args.append(lora_g.astype(BF16))
    out_spec = pl.BlockSpec((1, L, W), lambda b, g, t: (b, t, g))
    y_shape = jax.ShapeDtypeStruct((B, S, SEQ_WIDTH), BF16)
    if with_v:
        in_specs.append(out_spec)
        args.append(v_first)
        out_shape, out_specs = y_shape, out_spec
    else:
        out_shape = (y_shape, jax.ShapeDtypeStruct((B, S, SEQ_WIDTH), F32))
        out_specs = (out_spec, out_spec)

    res = pl.pallas_call(
        functools.partial(_rwkv_mixer_kernel, with_v=with_v),
        out_shape=out_shape,
        grid=(B, SEQ_WIDTH // W, S // L),
        in_specs=in_specs,
        out_specs=out_specs,
        scratch_shapes=[pltpu.VMEM((RWKV_PAIRS, PAIR, PAIR), F32)],
        compiler_params=_cparams("parallel", "parallel", "arbitrary"),
        name="rwkv_mixer",
    )(*args)
    return (res, v_first) if with_v else res


def kernel(x, mem, positions, mem_w_kv, l0_w_in, l0_mu, l0_lora_w, l0_lora_a, l0_lora_g, l0_vecs, l0_w_out, l0_norms, l0_ffn_gate, l0_ffn_up, l0_ffn_down, l1_w_in, l1_gn, l1_w_out, l1_norms, l1_router, l1_moe_gate, l1_moe_up, l1_moe_down, l2_w_in, l2_mu, l2_lora_w, l2_lora_a, l2_lora_v, l2_lora_g, l2_vecs, l2_w_out, l2_norms, l2_ffn_gate, l2_ffn_up, l2_ffn_down, l3_w_in, l3_gn, l3_w_out, l3_norms, l3_router, l3_moe_gate, l3_moe_up, l3_moe_down):
    B, S, D = x.shape
    T = B * S
    M = mem.shape[1]
    mem_kv = _matmul(mem.reshape(B * M, D), mem_w_kv.astype(BF16), tm=1024, tn=512,
                     out_dtype=BF16).reshape(B, M, 2 * MEM_WIDTH)

    rwkv_layers = {
        0: (l0_w_in, l0_mu, l0_lora_w, l0_lora_a, l0_lora_g, None, l0_vecs),
        2: (l2_w_in, l2_mu, l2_lora_w, l2_lora_a, l2_lora_g, l2_lora_v, l2_vecs),
    }
    ret_layers = {1: (l1_w_in, l1_gn), 3: (l3_w_in, l3_gn)}
    w_outs = (l0_w_out, l1_w_out, l2_w_out, l3_w_out)
    norms = (l0_norms, l1_norms, l2_norms, l3_norms)
    dense = {0: (l0_ffn_gate, l0_ffn_up, l0_ffn_down), 2: (l2_ffn_gate, l2_ffn_up, l2_ffn_down)}
    moe = {1: (l1_router, l1_moe_gate, l1_moe_up, l1_moe_down),
           3: (l3_router, l3_moe_gate, l3_moe_up, l3_moe_down)}

    xf = x.reshape(T, D)
    xin = xf
    v_first = None
    for i in range(DEPTH):
        if i in rwkv_layers:
            w_in, mu, lw_, la_, lg_, lv_, vecs = rwkv_layers[i]
            w_in, mu = _rwkv_layout(w_in, mu, lv_ is not None)
            p3 = _matmul(xin, w_in.astype(BF16), tm=1024, tn=512).reshape(B, S, -1)
            y_seq, v_first = _rwkv_mixer(p3, mu, lw_, la_, lg_, lv_, vecs, v_first)
        else:
            w_in, gn = ret_layers[i]
            p3 = _matmul(xin, w_in.astype(BF16), tm=1024, tn=512, out_dtype=BF16).reshape(B, S, -1)
            y_seq = _retention_mixer(p3, positions, gn)
        y_mem = _memory_attention(p3, mem_kv)
        nrm = norms[i]
        xf, xb = _proj_ln(y_seq.reshape(T, SEQ_WIDTH), y_mem.reshape(T, MEM_WIDTH), w_outs[i], xf,
                          nrm[0], nrm[1])
        if i in dense:
            xf, xb = _ffn_ln(xf, xb, *dense[i], nrm[2], nrm[3])
        else:
            xf, xb = _moe_ln(xf, xb, *moe[i], nrm[2], nrm[3])
        xin = xb
    return xf.reshape(B, S, D)
```

```python
import functools

import numpy as np
import jax
import jax.numpy as jnp
from jax import lax
from jax.experimental import pallas as pl
from jax.experimental.pallas import tpu as pltpu

F32 = jnp.float32
BF16 = jnp.bfloat16

D_MODEL = 2048
DEPTH = 4
MEM_HEADS = 4
MEM_HEAD_DIM = 128
MEM_WIDTH = MEM_HEADS * MEM_HEAD_DIM
SEQ_WIDTH = D_MODEL - MEM_WIDTH

RWKV_HEAD = 64
RWKV_HEADS = SEQ_WIDTH // RWKV_HEAD
LORA_W = 96
LORA_A = 96
LORA_V = 64
LORA_G = 256
LORA_PACK = 256
RWKV_GN_EPS = 1e-5 * RWKV_HEAD
RWKV_CHUNK = 64

RET_HEADS = 6
RET_DV = SEQ_WIDTH // RET_HEADS
RET_DK = RET_DV // 2
RET_CHUNK = 128
ROPE_BASE = 10000.0

N_EXPERTS = 8
TOP_K = 2

ALPHA = (2.0 * DEPTH) ** 0.25
LN_EPS = 1e-5

LANES = 128
Q_MEM_OFF = 3 * SEQ_WIDTH
VMEM_LIMIT = 52 * 1024 * 1024
MOE_VMEM_LIMIT = 58 * 1024 * 1024


def _cparams(*semantics):
    return pltpu.CompilerParams(dimension_semantics=semantics, vmem_limit_bytes=VMEM_LIMIT)


def _dot(a, b):
    return jnp.dot(a, b, preferred_element_type=F32)


def _dot_nt(a, b):
    return lax.dot_general(a, b, (((1,), (1,)), ((), ())), preferred_element_type=F32)


def _dot_tn(a, b):
    return lax.dot_general(a, b, (((0,), (0,)), ((), ())), preferred_element_type=F32)


def _layer_norm_rows(z, g, b):
    mu = jnp.mean(z, axis=-1, keepdims=True)
    zc = z - mu
    var = jnp.mean(zc * zc, axis=-1, keepdims=True)
    return zc * lax.rsqrt(var + LN_EPS) * g + b


def _mm_kernel(a_ref, w_ref, o_ref, abf_ref):
    @pl.when(pl.program_id(1) == 0)
    def _():
        abf_ref[...] = a_ref[...].astype(BF16)

    o_ref[...] = _dot(abf_ref[...], w_ref[...]).astype(o_ref.dtype)


def _matmul(a, w, *, tm, tn, out_dtype=F32):
    M, K = a.shape
    N = w.shape[1]
    tm = min(tm, M)
    assert M % tm == 0 and N % tn == 0
    return pl.pallas_call(
        _mm_kernel,
        out_shape=jax.ShapeDtypeStruct((M, N), out_dtype),
        grid=(M // tm, N // tn),
        in_specs=[pl.BlockSpec((tm, K), lambda i, j: (i, 0)),
                  pl.BlockSpec((K, tn), lambda i, j: (0, j))],
        out_specs=pl.BlockSpec((tm, tn), lambda i, j: (i, j)),
        scratch_shapes=[pltpu.VMEM((tm, K), BF16)],
        compiler_params=_cparams("parallel", "arbitrary"),
        name="matmul",
    )(a, w)


def _router_kernel(x_ref, w_ref, o_ref):
    o_ref[...] = jnp.dot(x_ref[...], w_ref[...], preferred_element_type=F32,
                         precision=lax.Precision.HIGHEST)


def _router_logits(x, w_router):
    T = x.shape[0]
    tm = min(512, T)
    wp = jnp.pad(w_router, ((0, 0), (0, LANES - N_EXPERTS)))
    out = pl.pallas_call(
        _router_kernel,
        out_shape=jax.ShapeDtypeStruct((T, LANES), F32),
        grid=(T // tm,),
        in_specs=[pl.BlockSpec((tm, D_MODEL), lambda i: (i, 0)),
                  pl.BlockSpec((D_MODEL, LANES), lambda i: (0, 0))],
        out_specs=pl.BlockSpec((tm, LANES), lambda i: (i, 0)),
        compiler_params=_cparams("parallel"),
        name="router",
    )(x, wp)
    return out[:, :N_EXPERTS]


def _proj_ln_kernel(ys_ref, ym_ref, ws_ref, wm_ref, x_ref, g_ref, b_ref, o_ref, ob_ref):
    half = x_ref.shape[0] // 2
    for s in range(2):
        rs = slice(s * half, (s + 1) * half)
        h = _dot(ys_ref[rs, :], ws_ref[...]) + _dot(ym_ref[rs, :], wm_ref[...])
        out = _layer_norm_rows(ALPHA * x_ref[rs, :] + h, g_ref[...], b_ref[...])
        o_ref[rs, :] = out
        ob_ref[rs, :] = out.astype(BF16)


def _proj_ln(y_seq, y_mem, w_out, x, gamma, beta):
    T = x.shape[0]
    tm = min(512, T)
    ws = w_out[:SEQ_WIDTH].astype(BF16)
    wm = w_out[SEQ_WIDTH:].astype(BF16)
    row = lambda i: (i, 0)
    fixed = lambda i: (0, 0)
    once = pl.Buffered(1)
    return pl.pallas_call(
        _proj_ln_kernel,
        out_shape=(jax.ShapeDtypeStruct((T, D_MODEL), F32), jax.ShapeDtypeStruct((T, D_MODEL), BF16)),
        grid=(T // tm,),
        in_specs=[pl.BlockSpec((tm, SEQ_WIDTH), row), pl.BlockSpec((tm, MEM_WIDTH), row),
                  pl.BlockSpec((SEQ_WIDTH, D_MODEL), fixed, pipeline_mode=once),
                  pl.BlockSpec((MEM_WIDTH, D_MODEL), fixed, pipeline_mode=once),
                  pl.BlockSpec((tm, D_MODEL), row),
                  pl.BlockSpec((1, D_MODEL), fixed), pl.BlockSpec((1, D_MODEL), fixed)],
        out_specs=(pl.BlockSpec((tm, D_MODEL), row), pl.BlockSpec((tm, D_MODEL), row)),
        compiler_params=_cparams("parallel"),
        name="proj_ln",
    )(y_seq, y_mem, ws, wm, x, gamma.reshape(1, -1), beta.reshape(1, -1))


def _swiglu_partial(xb, wg, wu, wd):
    gate = _dot(xb, wg)
    up = _dot(xb, wu)
    h = (gate * jax.nn.sigmoid(gate) * up).astype(BF16)
    return _dot(h, wd)


def _ffn_ln_kernel(xb_ref, wg_ref, wu_ref, wd_ref, x_ref, g_ref, b_ref, o_ref, ob_ref, acc_ref):
    f = pl.program_id(1)

    @pl.when(f == 0)
    def _():
        acc_ref[...] = jnp.zeros_like(acc_ref)

    acc_ref[...] += _swiglu_partial(xb_ref[...], wg_ref[...], wu_ref[...], wd_ref[...])

    @pl.when(f == pl.num_programs(1) - 1)
    def _():
        out = _layer_norm_rows(ALPHA * x_ref[...] + acc_ref[...], g_ref[...], b_ref[...])
        o_ref[...] = out
        ob_ref[...] = out.astype(BF16)


def _ffn_ln(x, xb, w_gate, w_up, w_down, gamma, beta):
    T = x.shape[0]
    F = w_gate.shape[1]
    tm = min(512, T)
    tf = 512
    row = lambda i, f: (i, 0)
    fixed = lambda i, f: (0, 0)
    return pl.pallas_call(
        _ffn_ln_kernel,
        out_shape=(jax.ShapeDtypeStruct((T, D_MODEL), F32), jax.ShapeDtypeStruct((T, D_MODEL), BF16)),
        grid=(T // tm, F // tf),
        in_specs=[pl.BlockSpec((tm, D_MODEL), row),
                  pl.BlockSpec((D_MODEL, tf), lambda i, f: (0, f)),
                  pl.BlockSpec((D_MODEL, tf), lambda i, f: (0, f)),
                  pl.BlockSpec((tf, D_MODEL), lambda i, f: (f, 0)),
                  pl.BlockSpec((tm, D_MODEL), row),
                  pl.BlockSpec((1, D_MODEL), fixed), pl.BlockSpec((1, D_MODEL), fixed)],
        out_specs=(pl.BlockSpec((tm, D_MODEL), row), pl.BlockSpec((tm, D_MODEL), row)),
        scratch_shapes=[pltpu.VMEM((tm, D_MODEL), F32)],
        compiler_params=_cparams("parallel", "arbitrary"),
        name="ffn_ln",
    )(xb, w_gate.astype(BF16), w_up.astype(BF16), w_down.astype(BF16), x,
      gamma.reshape(1, -1), beta.reshape(1, -1))


MOE_TM = 1024
MOE_TF = 256
MOE_RUNS = 4


def _moe_kernel(te_ref, nu_ref, xs_ref, wg_ref, wu_ref, wd_ref, *rest):
    o_ref, acc_ref = rest[-2:]
    i = pl.program_id(0)
    f = pl.program_id(1)
    last = pl.num_programs(1) - 1
    used = i < nu_ref[0]

    @pl.when(jnp.logical_and(used, f == 0))
    def _():
        acc_ref[...] = jnp.zeros_like(acc_ref)

    @pl.when(used)
    def _():
        acc_ref[...] += _swiglu_partial(xs_ref[...], wg_ref[...].astype(BF16), wu_ref[...].astype(BF16),
                                        wd_ref[...].astype(BF16))

    @pl.when(jnp.logical_and(used, f == last))
    def _():
        o_ref[...] = acc_ref[...].astype(o_ref.dtype)

    @pl.when(jnp.logical_and(jnp.logical_not(used), f == last))
    def _():
        o_ref[...] = jnp.zeros_like(o_ref)


def _moe_grouped(xs, tile_expert, n_used, w_gate, w_up, w_down, ys_prev, tile0, total_tiles):
    tiles = xs.shape[0] // MOE_TM
    F = w_gate.shape[2]
    nf = F // MOE_TF

    def fcol(i, f, nu):
        return jnp.where(i < nu[0], f, nf - 1)

    in_specs = [pl.BlockSpec((MOE_TM, D_MODEL), lambda i, f, te, nu: (i, 0)),
                pl.BlockSpec((None, D_MODEL, MOE_TF), lambda i, f, te, nu: (te[i], 0, fcol(i, f, nu))),
                pl.BlockSpec((None, D_MODEL, MOE_TF), lambda i, f, te, nu: (te[i], 0, fcol(i, f, nu))),
                pl.BlockSpec((None, MOE_TF, D_MODEL), lambda i, f, te, nu: (te[i], fcol(i, f, nu), 0))]
    args = [tile_expert, n_used, xs, w_gate, w_up, w_down]
    aliases = {}
    if ys_prev is not None:
        in_specs.append(pl.BlockSpec(memory_space=pl.ANY))
        aliases = {len(args): 0}
        args.append(ys_prev)
    grid_spec = pltpu.PrefetchScalarGridSpec(
        num_scalar_prefetch=2,
        grid=(tiles, nf),
        in_specs=in_specs,
        out_specs=pl.BlockSpec((MOE_TM, D_MODEL), lambda i, f, te, nu: (tile0 + i, 0)),
        scratch_shapes=[pltpu.VMEM((MOE_TM, D_MODEL), F32)],
    )
    return pl.pallas_call(
        _moe_kernel,
        out_shape=jax.ShapeDtypeStruct((total_tiles * MOE_TM, D_MODEL), BF16),
        grid_spec=grid_spec,
        input_output_aliases=aliases,
        compiler_params=pltpu.CompilerParams(dimension_semantics=("arbitrary", "arbitrary"),
                                             vmem_limit_bytes=MOE_VMEM_LIMIT),
        name="moe_grouped",
    )(*args)


def _combine_ln_kernel(x_ref, ya_ref, yb_ref, w_ref, g_ref, b_ref, o_ref, ob_ref):
    w = w_ref[...]
    f = w[:, 0:1] * ya_ref[...].astype(F32) + w[:, 1:2] * yb_ref[...].astype(F32)
    out = _layer_norm_rows(ALPHA * x_ref[...] + f, g_ref[...], b_ref[...])
    o_ref[...] = out
    ob_ref[...] = out.astype(BF16)


def _combine_ln(x, y_rows, top_w, gamma, beta):
    T = x.shape[0]
    tm = min(512, T)
    nt = T // tm
    wpad = jnp.pad(top_w, ((0, 0), (0, LANES - TOP_K)))
    fixed = lambda i: (0, 0)
    row = lambda i: (i, 0)
    return pl.pallas_call(
        _combine_ln_kernel,
        out_shape=(jax.ShapeDtypeStruct((T, D_MODEL), F32), jax.ShapeDtypeStruct((T, D_MODEL), BF16)),
        grid=(nt,),
        in_specs=[pl.BlockSpec((tm, D_MODEL), row),
                  pl.BlockSpec((tm, D_MODEL), row),
                  pl.BlockSpec((tm, D_MODEL), lambda i: (i + nt, 0)),
                  pl.BlockSpec((tm, LANES), row),
                  pl.BlockSpec((1, D_MODEL), fixed), pl.BlockSpec((1, D_MODEL), fixed)],
        out_specs=(pl.BlockSpec((tm, D_MODEL), row), pl.BlockSpec((tm, D_MODEL), row)),
        compiler_params=_cparams("parallel"),
        name="combine_ln",
    )(x, y_rows, y_rows, wpad, gamma.reshape(1, -1), beta.reshape(1, -1))


def _moe_ln(x, xb, w_router, w_gate, w_up, w_down, gamma, beta):
    T = x.shape[0]
    logits = _router_logits(x, w_router)
    top_val, top_idx = lax.top_k(logits, TOP_K)
    top_w = jax.nn.softmax(top_val, axis=-1)

    e_flat = top_idx.reshape(-1).astype(jnp.int32)
    onehot = (e_flat[:, None] == jnp.arange(N_EXPERTS, dtype=jnp.int32)[None, :]).astype(jnp.int32)
    rank = jnp.sum((jnp.cumsum(onehot, axis=0) - onehot) * onehot, axis=1)
    counts = jnp.sum(onehot, axis=0)
    padded = ((counts + MOE_TM - 1) // MOE_TM) * MOE_TM
    ends = jnp.cumsum(padded)
    starts = ends - padded
    dest = starts[e_flat] + rank

    Tp = TOP_K * T + N_EXPERTS * MOE_TM
    n_tiles = Tp // MOE_TM
    token = jnp.arange(TOP_K * T, dtype=jnp.int32) // TOP_K
    src = jnp.zeros((Tp,), jnp.int32).at[dest].set(token)
    n_used = (ends[-1] // MOE_TM).astype(jnp.int32).reshape(1)
    tile_start = jnp.arange(n_tiles, dtype=jnp.int32) * MOE_TM
    tile_expert = jnp.sum((tile_start[:, None] >= ends[None, :]).astype(jnp.int32), axis=1)
    last_expert = jnp.sum((jnp.maximum(ends[-1] - 1, 0) >= ends).astype(jnp.int32))
    tile_expert = jnp.minimum(tile_expert, last_expert).astype(jnp.int32)

    runs = max(d for d in range(1, MOE_RUNS + 1) if n_tiles % d == 0)
    tiles = n_tiles // runs
    ys = None
    for c in range(runs):
        xs = xb.at[src[c * tiles * MOE_TM:(c + 1) * tiles * MOE_TM]].get(mode="promise_in_bounds")
        ys = _moe_grouped(xs, tile_expert[c * tiles:(c + 1) * tiles], jnp.clip(n_used - c * tiles, 0, tiles),
                          w_gate, w_up, w_down, ys, c * tiles, n_tiles)
    slot_major = dest.reshape(T, TOP_K).T.reshape(-1)
    y_rows = ys.at[slot_major].get(mode="promise_in_bounds")
    return _combine_ln(x, y_rows, top_w, gamma, beta)


def _memattn_kernel(q_ref, k_ref, v_ref, o_ref):
    q = q_ref[0]
    outs = []
    for h in range(MEM_HEADS):
        sl = slice(h * MEM_HEAD_DIM, (h + 1) * MEM_HEAD_DIM)
        s = _dot_nt(q[:, sl].astype(BF16), k_ref[0, :, sl]) * (MEM_HEAD_DIM ** -0.5)
        e = jnp.exp(s - jnp.max(s, axis=-1, keepdims=True))
        p = e / jnp.sum(e, axis=-1, keepdims=True)
        outs.append(_dot(p.astype(BF16), v_ref[0, :, sl]))
    o_ref[0] = jnp.concatenate(outs, axis=-1).astype(o_ref.dtype)


def _memory_attention(p3, mem_kv):
    B, S, _ = p3.shape
    M = mem_kv.shape[1]
    ts = min(512, S)
    return pl.pallas_call(
        _memattn_kernel,
        out_shape=jax.ShapeDtypeStruct((B, S, MEM_WIDTH), BF16),
        grid=(B, S // ts),
        in_specs=[pl.BlockSpec((1, ts, MEM_WIDTH), lambda b, s: (b, s, Q_MEM_OFF // MEM_WIDTH)),
                  pl.BlockSpec((1, M, MEM_WIDTH), lambda b, s: (b, 0, 0)),
                  pl.BlockSpec((1, M, MEM_WIDTH), lambda b, s: (b, 0, 1))],
        out_specs=pl.BlockSpec((1, ts, MEM_WIDTH), lambda b, s: (b, s, 0)),
        compiler_params=_cparams("parallel", "parallel"),
        name="mem_attn",
    )(p3, mem_kv, mem_kv)


def _ret_kernel(q_ref, k_ref, v_ref, g_ref, cos_ref, sin_ref, intra_ref, qd_ref, kd_ref, cd_ref,
                gn_ref, o_ref, state_ref):
    @pl.when(pl.program_id(1) == 0)
    def _():
        state_ref[...] = jnp.zeros_like(state_ref)

    cosf = cos_ref[0]
    sinf = sin_ref[0]
    half = RET_DK // 2
    rot = lambda x: x * cosf + pltpu.roll(x, half, 1) * sinf
    hs = range(RET_HEADS)
    ksl = [slice(h * RET_DK, (h + 1) * RET_DK) for h in hs]
    vsl = [slice(h * RET_DV, (h + 1) * RET_DV) for h in hs]

    qr = [rot(q_ref[0, :, ksl[h]].astype(F32)) for h in hs]
    kr = [rot(k_ref[0, :, ksl[h]].astype(F32)) * (RET_DK ** -0.5) for h in hs]
    vb = [v_ref[0, :, vsl[h]].astype(BF16) for h in hs]
    state = [state_ref[h] for h in hs]
    s = [_dot_nt(qr[h].astype(BF16), kr[h].astype(BF16)) * intra_ref[h] for h in hs]
    o = [_dot(s[h].astype(BF16), vb[h]) + _dot((qr[h] * qd_ref[h]).astype(BF16), state[h].astype(BF16))
         for h in hs]
    for h in hs:
        state_ref[h] = state[h] * cd_ref[h] + _dot_tn((kr[h] * kd_ref[h]).astype(BF16), vb[h])
    for h in hs:
        mu = jnp.mean(o[h], axis=-1, keepdims=True)
        oc = o[h] - mu
        var = jnp.mean(oc * oc, axis=-1, keepdims=True)
        y = oc * lax.rsqrt(var + LN_EPS) * gn_ref[0:1, vsl[h]] + gn_ref[1:2, vsl[h]]
        g = g_ref[0, :, vsl[h]].astype(F32)
        o_ref[0, :, vsl[h]] = (y * (g * jax.nn.sigmoid(g))).astype(o_ref.dtype)


def _retention_mixer(p3, positions, gn):
    B, S, _ = p3.shape
    C = RET_CHUNK
    H = RET_HEADS
    inv = 1.0 / (ROPE_BASE ** (jnp.arange(0, RET_DK, 2, dtype=F32) / RET_DK))
    ang = positions.astype(F32)[..., None] * inv
    cos, sin = jnp.cos(ang), jnp.sin(ang)
    cosf = jnp.concatenate([cos, cos], axis=-1)
    sinf = jnp.concatenate([-sin, sin], axis=-1)

    log_g = jnp.log1p(-jnp.exp2(-5.0 - jnp.arange(H, dtype=F32)))
    idx = jnp.arange(C, dtype=F32)
    diff = idx[:, None] - idx[None, :]
    intra = jnp.where(diff >= 0, jnp.exp(jnp.maximum(diff, 0.0)[None] * log_g[:, None, None]), 0.0)
    qd = jnp.broadcast_to(jnp.exp((idx + 1.0)[None, :, None] * log_g[:, None, None]), (H, C, RET_DK))
    kd = jnp.broadcast_to(jnp.exp((C - 1.0 - idx)[None, :, None] * log_g[:, None, None]), (H, C, RET_DK))
    cd = jnp.broadcast_to(jnp.exp(C * log_g)[:, None, None], (H, 1, RET_DV))

    QK = H * RET_DK
    assert 2 * QK == SEQ_WIDTH
    whole = lambda b, c: (0, 0, 0)
    return pl.pallas_call(
        _ret_kernel,
        out_shape=jax.ShapeDtypeStruct((B, S, SEQ_WIDTH), BF16),
        grid=(B, S // C),
        in_specs=[pl.BlockSpec((1, C, QK), lambda b, c: (b, c, 0)),
                  pl.BlockSpec((1, C, QK), lambda b, c: (b, c, 1)),
                  pl.BlockSpec((1, C, SEQ_WIDTH), lambda b, c: (b, c, 1)),
                  pl.BlockSpec((1, C, SEQ_WIDTH), lambda b, c: (b, c, 2)),
                  pl.BlockSpec((1, C, RET_DK), lambda b, c: (b, c, 0)),
                  pl.BlockSpec((1, C, RET_DK), lambda b, c: (b, c, 0)),
                  pl.BlockSpec((H, C, C), whole),
                  pl.BlockSpec((H, C, RET_DK), whole),
                  pl.BlockSpec((H, C, RET_DK), whole),
                  pl.BlockSpec((H, 1, RET_DV), whole),
                  pl.BlockSpec((2, SEQ_WIDTH), lambda b, c: (0, 0))],
        out_specs=pl.BlockSpec((1, C, SEQ_WIDTH), lambda b, c: (b, c, 0)),
        scratch_shapes=[pltpu.VMEM((H, RET_DK, RET_DV), F32)],
        compiler_params=_cparams("parallel", "arbitrary"),
        name="retention",
    )(p3, p3, p3, p3, cosf, sinf, intra, qd, kd, cd, gn)


RWKV_PAIRS = 12
RWKV_SUB = 2
PAIR = 2 * RWKV_HEAD
PREV_ROWS = 16

_NN = ((1,), (0,))
_NT = ((1,), (1,))
_TN = ((0,), (0,))


def _dg(a, b, dims):
    return lax.dot_general(a.astype(BF16), b.astype(BF16), (dims, ((), ())), preferred_element_type=F32)


def _rwkv_masks():
    L = RWKV_CHUNK
    row = np.arange(L)[:, None]
    lane = np.arange(PAIR)[None, :]
    m1 = np.broadcast_to(lane < RWKV_HEAD, (L, PAIR))
    small = np.stack([m1, ~m1, row > lane % L, row >= lane % L, row == lane % L]).astype(np.float32)
    same_head = (np.arange(2 * L)[:, None] // L == lane // L).astype(np.float32)
    tri = np.kron(np.eye(RWKV_SUB), row >= np.arange(L)[None, :]).astype(np.float32)
    return (jnp.asarray(small), jnp.asarray(same_head), jnp.asarray(same_head, dtype=BF16),
            jnp.asarray(tri, dtype=BF16))


def _rwkv_prepare(r, c, lw, k, v, kk, a, masks):
    n = len(r)
    rng = range(n)
    L = r[0].shape[0]
    assert L == RWKV_HEAD
    m1, m2, strict, incl, eye, same_head = masks
    stack = lambda x: jnp.concatenate([x * m1, x * m2], axis=0)
    blockdiag = lambda x: (jnp.concatenate([x, x], axis=0) * same_head).astype(BF16)

    p_incl = [jnp.exp(c[j]) for j in rng]
    p_inv = [jnp.exp(-c[j]) for j in rng]
    at = [-kk[j] * jnp.exp(c[j] - lw[j]) for j in rng]
    rt = [r[j] * p_incl[j] for j in rng]
    lhs = [jnp.concatenate([at[j], rt[j]], axis=0).astype(BF16) for j in rng]
    rhs = [jnp.concatenate([stack(kk[j] * a[j] * p_inv[j]), stack(k[j] * p_inv[j])], axis=0).astype(BF16)
           for j in rng]
    sc = [_dg(lhs[j], rhs[j], _NT) for j in rng]
    nmat = [sc[j][:L, :PAIR] * strict for j in rng]
    tinv = [eye + nmat[j] for j in rng]
    pw = [_dg(nmat[j], blockdiag(nmat[j]), _NN) for j in rng]
    rounds = int(np.log2(L)) - 1
    for i in range(rounds):
        if i < rounds - 1:
            both = [_dg(jnp.concatenate([pw[j], tinv[j]], axis=0), blockdiag(pw[j]), _NN) for j in rng]
            pw = [both[j][:L] for j in rng]
            tinv = [tinv[j] + both[j][L:] for j in rng]
        else:
            tinv = [tinv[j] + _dg(tinv[j], blockdiag(pw[j]), _NN) for j in rng]

    vs = [stack(v[j]).astype(BF16) for j in rng]
    mv = [_dg(sc[j][:L, PAIR:] * strict, vs[j], _NN) for j in rng]
    a_r = [(sc[j][L:, :] * jnp.concatenate([incl, incl], axis=1)).astype(BF16) for j in rng]
    return dict(lhs=lhs, rhs=rhs, vs=vs, mv=mv, a_r=a_r, tinv=[tinv[j].astype(BF16) for j in rng],
                decay=[p_incl[j][L - 1:L, :] for j in rng])


def _rwkv_apply(prep, entries, state, masks):
    m1, m2 = masks[0], masks[1]
    stack = lambda x: jnp.concatenate([x * m1, x * m2], axis=0)
    L = m1.shape[0]
    rng = range(len(entries))
    get = lambda name: [prep[name][e] for e in entries]
    lhs, rhs, vs, mv, a_r, tinv, decay = (get(n) for n in ("lhs", "rhs", "vs", "mv", "a_r", "tinv", "decay"))
    sg = [_dg(lhs[j], state[j], _NT) for j in rng]
    u = [_dg(tinv[j], stack(sg[j][:L] + mv[j]), _NN) for j in rng]
    uv = [jnp.concatenate([stack(u[j]).astype(BF16), vs[j]], axis=0) for j in rng]
    ys = [_dg(a_r[j], uv[j], _NN) for j in rng]
    y = [sg[j][L:] + ys[j] for j in rng]
    new_state = [(state[j] + _dg(uv[j], rhs[j], _TN)) * decay[j] for j in rng]
    return y, new_state


def _shift_mix(x, prev_block, mu, first):
    prev_row = jnp.where(first, 0.0, prev_block[PREV_ROWS - 1:PREV_ROWS, :])
    rows = lax.broadcasted_iota(jnp.int32, x.shape, 0)
    prev = jnp.where(rows == 0, prev_row, pltpu.roll(x, 1, 0))
    return x + mu * (prev - x)


def _rwkv_mixer_kernel(*refs, with_v):
    it = iter(refs)
    r_ref, k_ref, v_ref, xg_ref, xl_ref = (next(it) for _ in range(5))
    rp_ref, kp_ref, vp_ref, xgp_ref, xlp_ref = (next(it) for _ in range(5))
    mu3_ref, mug_ref, mul_ref, vec_ref = (next(it) for _ in range(4))
    small_ref, same_head_ref, ones_ref, tri_ref = (next(it) for _ in range(4))
    ww_ref, wa_ref = next(it), next(it)
    wv_ref = next(it) if with_v else None
    wg_ref = next(it)
    vf_ref = next(it) if with_v else None
    y_ref = next(it)
    vfo_ref = None if with_v else next(it)
    state_ref = next(it)

    first = pl.program_id(2) == 0

    @pl.when(first)
    def _():
        state_ref[...] = jnp.zeros_like(state_ref)

    L = RWKV_CHUNK
    mix = lambda x_ref, p_ref, mu: _shift_mix(x_ref[0].astype(F32), p_ref[0].astype(F32), mu, first)
    r = mix(r_ref, rp_ref, mu3_ref[0:1, :])
    k = mix(k_ref, kp_ref, mu3_ref[1:2, :])
    v = mix(v_ref, vp_ref, mu3_ref[2:3, :])
    xg = mix(xg_ref, xgp_ref, mug_ref[...])
    xl = mix(xl_ref, xlp_ref, mul_ref[...])

    w0, a0, v0 = vec_ref[0:1, :], vec_ref[1:2, :], vec_ref[2:3, :]
    k_k, k_a, r_k = vec_ref[3:4, :], vec_ref[4:5, :], vec_ref[5:6, :]
    gn_g, gn_b = vec_ref[6:7, :], vec_ref[7:8, :]

    lane = lax.broadcasted_iota(jnp.int32, xl.shape, 1)
    lin = jnp.where(lane < LORA_W, jnp.tanh(xl), xl).astype(BF16)
    w_arg = w0 + _dot(lin, ww_ref[...])
    a = jax.nn.sigmoid(a0 + _dot(lin, wa_ref[...]))
    if with_v:
        v = v + (vf_ref[0] - v) * jax.nn.sigmoid(v0 + _dot(lin, wv_ref[...]))
    else:
        vfo_ref[0] = v
    g = _dot(jax.nn.sigmoid(xg).astype(BF16), wg_ref[...])

    w_log = -(jnp.maximum(-w_arg, 0.0) + jnp.log1p(jnp.exp(-jnp.abs(w_arg)))) - 0.5
    lw = -jnp.exp(w_log)
    tri = tri_ref[...]
    h1 = lw.astype(BF16)
    rem = lw - h1.astype(F32)
    h2 = rem.astype(BF16)
    h3 = (rem - h2.astype(F32)).astype(BF16)
    c = _dot(tri, h1) + _dot(tri, h2) + _dot(tri, h3)

    kk_raw = k * k_k
    k2 = k * (1.0 + (a - 1.0) * k_a)
    ones_bd = ones_ref[...]
    masks = tuple(small_ref[i] for i in range(5)) + (same_head_ref[...],)
    inv_n = 1.0 / RWKV_HEAD

    pairs = range(RWKV_PAIRS)
    cols = [slice(j * PAIR, (j + 1) * PAIR) for j in pairs]
    rows = [slice(s * L, (s + 1) * L) for s in range(RWKV_SUB)]
    where = [(rs, cs) for rs in rows for cs in cols]
    ent = range(len(where))
    part = lambda x: [x[rs, cs] for rs, cs in where]
    kkr, r_e, k_e, v_e, g_e = part(kk_raw), part(r), part(k2), part(v), part(g)
    red = [_dot(jnp.concatenate([kkr[e] * kkr[e], r_e[e] * k_e[e] * r_k[:, where[e][1]]], axis=0).astype(BF16),
                ones_bd) for e in ent]
    kks = [kkr[e] / jnp.maximum(jnp.sqrt(red[e][:L]), 1e-12) for e in ent]
    prep = _rwkv_prepare(r_e, part(c), part(lw), k_e, v_e, kks, part(a), masks)
    state = [state_ref[j] for j in pairs]
    ys = []
    for s in range(RWKV_SUB):
        y_s, state = _rwkv_apply(prep, [s * RWKV_PAIRS + j for j in pairs], state, masks)
        ys += y_s
    for j in pairs:
        state_ref[j] = state[j]
    yc = [ys[e] - _dot(ys[e].astype(BF16), ones_bd) * inv_n for e in ent]
    var = [_dot((yc[e] * yc[e]).astype(BF16), ones_bd) * inv_n for e in ent]
    for e in ent:
        rs, cs = where[e]
        yn = yc[e] * lax.rsqrt(var[e] + RWKV_GN_EPS) * gn_g[:, cs] + gn_b[:, cs]
        y_ref[0, rs, cs] = ((yn + red[e][L:] * v_e[e]) * g_e[e]).astype(y_ref.dtype)


def _rwkv_layout(w_in, mu, with_v):
    sizes = [SEQ_WIDTH] * 3 + [LORA_W, LORA_A] + ([LORA_V] if with_v else []) + [LORA_G]
    offs = np.concatenate([[0], np.cumsum(sizes)])
    seq_cols = int(offs[-1])
    g_idx = len(sizes) - 1
    pad = LORA_PACK - LORA_W - LORA_A - (LORA_V if with_v else 0)
    cols = [w_in[:, :3 * SEQ_WIDTH], w_in[:, seq_cols:], w_in[:, offs[g_idx]:offs[g_idx + 1]],
            w_in[:, offs[3]:offs[g_idx]], jnp.zeros((w_in.shape[0], pad), w_in.dtype)]
    mus = [mu[:3 * SEQ_WIDTH], jnp.zeros((MEM_WIDTH,), mu.dtype), mu[offs[g_idx]:offs[g_idx + 1]],
           mu[offs[3]:offs[g_idx]], jnp.zeros((pad,), mu.dtype)]
    return jnp.concatenate(cols, axis=1), jnp.concatenate(mus)


def _rwkv_mixer(p3, mu, lora_w, lora_a, lora_g, lora_v, vecs, v_first):
    B, S, _ = p3.shape
    L = RWKV_SUB * RWKV_CHUNK
    W = RWKV_PAIRS * PAIR
    above = lambda t: jnp.maximum(t * (L // PREV_ROWS) - 1, 0)
    with_v = lora_v is not None
    if with_v:
        w0, a0, v0, k_k, k_a, r_k, gn_g, gn_b = vecs
    else:
        w0, a0, k_k, k_a, r_k, gn_g, gn_b = vecs
        v0 = jnp.zeros_like(w0)
    vec8 = jnp.stack([w0, a0, v0, k_k, k_a, r_k, gn_g, gn_b])
    mu3 = jnp.concatenate([mu[:3 * SEQ_WIDTH].reshape(3, SEQ_WIDTH), jnp.zeros((5, SEQ_WIDTH), F32)])
    g_off = Q_MEM_OFF + MEM_WIDTH
    l_off = g_off + LORA_G
    mug = mu[g_off:g_off + LORA_G].reshape(1, LORA_G)
    mul = mu[l_off:l_off + LORA_PACK].reshape(1, LORA_PACK)
    rows = lambda w, lo: jnp.zeros((LORA_PACK, SEQ_WIDTH), F32).at[lo:lo + w.shape[0]].set(w).astype(BF16)

    def cols(width, off):
        blk = off // width
        return pl.BlockSpec((1, L, width), lambda b, g, t: (b, t, blk))

    def cols_prev(width, off):
        blk = off // width
        return pl.BlockSpec((1, PREV_ROWS, width), lambda b, g, t: (b, above(t), blk))

    def seq(off):
        blk = off // W
        return pl.BlockSpec((1, L, W), lambda b, g, t: (b, t, blk + g))

    def seq_prev(off):
        blk = off // W
        return pl.BlockSpec((1, PREV_ROWS, W), lambda b, g, t: (b, above(t), blk + g))

    per_group = lambda nrows: pl.BlockSpec((nrows, W), lambda b, g, t: (0, g))
    fixed = lambda ncols: pl.BlockSpec((1, ncols), lambda b, g, t: (0, 0))

    in_specs = [seq(0), seq(SEQ_WIDTH), seq(2 * SEQ_WIDTH), cols(LORA_G, g_off), cols(LORA_PACK, l_off),
                seq_prev(0), seq_prev(SEQ_WIDTH), seq_prev(2 * SEQ_WIDTH),
                cols_prev(LORA_G, g_off), cols_prev(LORA_PACK, l_off),
                per_group(8), fixed(LORA_G), fixed(LORA_PACK), per_group(8)]
    masks = _rwkv_masks()
    in_specs += [pl.BlockSpec(m.shape, lambda b, g, t, nd=m.ndim: (0,) * nd) for m in masks]
    in_specs += [per_group(LORA_PACK), per_group(LORA_PACK)]
    args = [p3] * 10 + [mu3, mug, mul, vec8, *masks, rows(lora_w, 0), rows(lora_a, LORA_W)]
    if with_v:
        in_specs.append(per_group(LORA_PACK))
        args.append(rows(lora_v, LORA_W + LORA_A))
    in_specs.append(per_group(LORA_G))
    args.append(lora_g.astype(BF16))
    out_spec = pl.BlockSpec((1, L, W), lambda b, g, t: (b, t, g))
    y_shape = jax.ShapeDtypeStruct((B, S, SEQ_WIDTH), BF16)
    if with_v:
        in_specs.append(out_spec)
        args.append(v_first)
        out_shape, out_specs = y_shape, out_spec
    else:
        out_shape = (y_shape, jax.ShapeDtypeStruct((B, S, SEQ_WIDTH), F32))
        out_specs = (out_spec, out_spec)

    res = pl.pallas_call(
        functools.partial(_rwkv_mixer_kernel, with_v=with_v),
        out_shape=out_shape,
        grid=(B, SEQ_WIDTH // W, S // L),
        in_specs=in_specs,
        out_specs=out_specs,
        scratch_shapes=[pltpu.VMEM((RWKV_PAIRS, PAIR, PAIR), F32)],
        compiler_params=_cparams("parallel", "parallel", "arbitrary"),
        name="rwkv_mixer",
    )(*args)
    return (res, v_first) if with_v else res


def kernel(x, mem, positions, mem_w_kv, l0_w_in, l0_mu, l0_lora_w, l0_lora_a, l0_lora_g, l0_vecs, l0_w_out, l0_norms, l0_ffn_gate, l0_ffn_up, l0_ffn_down, l1_w_in, l1_gn, l1_w_out, l1_norms, l1_router, l1_moe_gate, l1_moe_up, l1_moe_down, l2_w_in, l2_mu, l2_lora_w, l2_lora_a, l2_lora_v, l2_lora_g, l2_vecs, l2_w_out, l2_norms, l2_ffn_gate, l2_ffn_up, l2_ffn_down, l3_w_in, l3_gn, l3_w_out, l3_norms, l3_router, l3_moe_gate, l3_moe_up, l3_moe_down):
    B, S, D = x.shape
    T = B * S
    M = mem.shape[1]
    mem_kv = _matmul(mem.reshape(B * M, D), mem_w_kv.astype(BF16), tm=1024, tn=512,
                     out_dtype=BF16).reshape(B, M, 2 * MEM_WIDTH)

    rwkv_layers = {
        0: (l0_w_in, l0_mu, l0_lora_w, l0_lora_a, l0_lora_g, None, l0_vecs),
        2: (l2_w_in, l2_mu, l2_lora_w, l2_lora_a, l2_lora_g, l2_lora_v, l2_vecs),
    }
    ret_layers = {1: (l1_w_in, l1_gn), 3: (l3_w_in, l3_gn)}
    w_outs = (l0_w_out, l1_w_out, l2_w_out, l3_w_out)
    norms = (l0_norms, l1_norms, l2_norms, l3_norms)
    dense = {0: (l0_ffn_gate, l0_ffn_up, l0_ffn_down), 2: (l2_ffn_gate, l2_ffn_up, l2_ffn_down)}
    moe = {1: (l1_router, l1_moe_gate, l1_moe_up, l1_moe_down),
           3: (l3_router, l3_moe_gate, l3_moe_up, l3_moe_down)}

    xf = x.reshape(T, D)
    xin = xf
    v_first = None
    for i in range(DEPTH):
        if i in rwkv_layers:
            w_in, mu, lw_, la_, lg_, lv_, vecs = rwkv_layers[i]
            w_in, mu = _rwkv_layout(w_in, mu, lv_ is not None)
            p3 = _matmul(xin, w_in.astype(BF16), tm=1024, tn=512, out_dtype=BF16).reshape(B, S, -1)
            y_seq, v_first = _rwkv_mixer(p3, mu, lw_, la_, lg_, lv_, vecs, v_first)
        else:
            w_in, gn = ret_layers[i]
            p3 = _matmul(xin, w_in.astype(BF16), tm=1024, tn=512, out_dtype=BF16).reshape(B, S, -1)
            y_seq = _retention_mixer(p3, positions, gn)
        y_mem = _memory_attention(p3, mem_kv)
        nrm = norms[i]
        xf, xb = _proj_ln(y_seq.reshape(T, SEQ_WIDTH), y_mem.reshape(T, MEM_WIDTH), w_outs[i], xf,
                          nrm[0], nrm[1])
        if i in dense:
            xf, xb = _ffn_ln(xf, xb, *dense[i], nrm[2], nrm[3])
        else:
            xf, xb = _moe_ln(xf, xb, *moe[i], nrm[2], nrm[3])
        xin = xb
    return xf.reshape(B, S, D)
```

```python
import functools

import numpy as np
import jax
import jax.numpy as jnp
from jax import lax
from jax.experimental import pallas as pl
from jax.experimental.pallas import tpu as pltpu

F32 = jnp.float32
BF16 = jnp.bfloat16

D_MODEL = 2048
DEPTH = 4
MEM_HEADS = 4
MEM_HEAD_DIM = 128
MEM_WIDTH = MEM_HEADS * MEM_HEAD_DIM
SEQ_WIDTH = D_MODEL - MEM_WIDTH

RWKV_HEAD = 64
RWKV_HEADS = SEQ_WIDTH // RWKV_HEAD
LORA_W = 96
LORA_A = 96
LORA_V = 64
LORA_G = 256
LORA_PACK = 256
RWKV_GN_EPS = 1e-5 * RWKV_HEAD
RWKV_CHUNK = 64

RET_HEADS = 6
RET_DV = SEQ_WIDTH // RET_HEADS
RET_DK = RET_DV // 2
RET_CHUNK = 128
ROPE_BASE = 10000.0

N_EXPERTS = 8
TOP_K = 2

ALPHA = (2.0 * DEPTH) ** 0.25
LN_EPS = 1e-5

LANES = 128
Q_MEM_OFF = 3 * SEQ_WIDTH
VMEM_LIMIT = 52 * 1024 * 1024
MOE_VMEM_LIMIT = 58 * 1024 * 1024


def _cparams(*semantics):
    return pltpu.CompilerParams(dimension_semantics=semantics, vmem_limit_bytes=VMEM_LIMIT)


def _dot(a, b):
    return jnp.dot(a, b, preferred_element_type=F32)


def _dot_nt(a, b):
    return lax.dot_general(a, b, (((1,), (1,)), ((), ())), preferred_element_type=F32)


def _dot_tn(a, b):
    return lax.dot_general(a, b, (((0,), (0,)), ((), ())), preferred_element_type=F32)


def _layer_norm_rows(z, g, b):
    mu = jnp.mean(z, axis=-1, keepdims=True)
    zc = z - mu
    var = jnp.mean(zc * zc, axis=-1, keepdims=True)
    return zc * lax.rsqrt(var + LN_EPS) * g + b


def _mm_kernel(a_ref, w_ref, o_ref, abf_ref):
    @pl.when(pl.program_id(1) == 0)
    def _():
        abf_ref[...] = a_ref[...].astype(BF16)

    o_ref[...] = _dot(abf_ref[...], w_ref[...]).astype(o_ref.dtype)


def _matmul(a, w, *, tm, tn, out_dtype=F32):
    M, K = a.shape
    N = w.shape[1]
    tm = min(tm, M)
    assert M % tm == 0 and N % tn == 0
    w_mode = pl.Buffered(1) if tn == N else None
    return pl.pallas_call(
        _mm_kernel,
        out_shape=jax.ShapeDtypeStruct((M, N), out_dtype),
        grid=(M // tm, N // tn),
        in_specs=[pl.BlockSpec((tm, K), lambda i, j: (i, 0)),
                  pl.BlockSpec((K, tn), lambda i, j: (0, j), pipeline_mode=w_mode)],
        out_specs=pl.BlockSpec((tm, tn), lambda i, j: (i, j)),
        scratch_shapes=[pltpu.VMEM((tm, K), BF16)],
        compiler_params=_cparams("parallel", "arbitrary"),
        name="matmul",
    )(a, w)


def _proj_ln_kernel(*refs, with_router):
    ys_ref, ym_ref, ws_ref, wm_ref, x_ref, g_ref, b_ref = refs[:7]
    if with_router:
        wrh_ref, wrl_ref, o_ref, ob_ref, logits_ref = refs[7:]
    else:
        o_ref, ob_ref = refs[7:]
    half = x_ref.shape[0] // 2
    for s in range(2):
        rs = slice(s * half, (s + 1) * half)
        h = _dot(ys_ref[rs, :], ws_ref[...]) + _dot(ym_ref[rs, :], wm_ref[...])
        out = _layer_norm_rows(ALPHA * x_ref[rs, :] + h, g_ref[...], b_ref[...])
        hi = out.astype(BF16)
        o_ref[rs, :] = out
        ob_ref[rs, :] = hi
        if with_router:
            lo = (out - hi.astype(F32)).astype(BF16)
            logits_ref[rs, :] = _dot(hi, wrh_ref[...]) + (_dot(lo, wrh_ref[...]) + _dot(hi, wrl_ref[...]))


def _proj_ln(y_seq, y_mem, w_out, x, gamma, beta, w_router=None):
    T = x.shape[0]
    tm = min(512, T)
    ws = w_out[:SEQ_WIDTH].astype(BF16)
    wm = w_out[SEQ_WIDTH:].astype(BF16)
    row = lambda i: (i, 0)
    fixed = lambda i: (0, 0)
    once = pl.Buffered(1)
    in_specs = [pl.BlockSpec((tm, SEQ_WIDTH), row), pl.BlockSpec((tm, MEM_WIDTH), row),
                pl.BlockSpec((SEQ_WIDTH, D_MODEL), fixed, pipeline_mode=once),
                pl.BlockSpec((MEM_WIDTH, D_MODEL), fixed, pipeline_mode=once),
                pl.BlockSpec((tm, D_MODEL), row),
                pl.BlockSpec((1, D_MODEL), fixed), pl.BlockSpec((1, D_MODEL), fixed)]
    args = [y_seq, y_mem, ws, wm, x, gamma.reshape(1, -1), beta.reshape(1, -1)]
    out_shape = [jax.ShapeDtypeStruct((T, D_MODEL), F32), jax.ShapeDtypeStruct((T, D_MODEL), BF16)]
    out_specs = [pl.BlockSpec((tm, D_MODEL), row), pl.BlockSpec((tm, D_MODEL), row)]
    if w_router is not None:
        wr = jnp.pad(w_router, ((0, 0), (0, LANES - N_EXPERTS)))
        wr_hi = wr.astype(BF16)
        wr_lo = (wr - wr_hi.astype(F32)).astype(BF16)
        in_specs += [pl.BlockSpec((D_MODEL, LANES), fixed, pipeline_mode=once)] * 2
        args += [wr_hi, wr_lo]
        out_shape.append(jax.ShapeDtypeStruct((T, LANES), F32))
        out_specs.append(pl.BlockSpec((tm, LANES), row))
    res = pl.pallas_call(
        functools.partial(_proj_ln_kernel, with_router=w_router is not None),
        out_shape=tuple(out_shape),
        grid=(T // tm,),
        in_specs=in_specs,
        out_specs=tuple(out_specs),
        compiler_params=_cparams("parallel"),
        name="proj_ln",
    )(*args)
    if w_router is None:
        return res
    return res[0], res[1], res[2][:, :N_EXPERTS]


def _swiglu_partial(xb, wg, wu, wd):
    gate = _dot(xb, wg)
    up = _dot(xb, wu)
    h = (gate * jax.nn.sigmoid(gate) * up).astype(BF16)
    return _dot(h, wd)


def _ffn_ln_kernel(xb_ref, wg_ref, wu_ref, wd_ref, x_ref, g_ref, b_ref, o_ref, ob_ref, acc_ref):
    f = pl.program_id(1)

    @pl.when(f == 0)
    def _():
        acc_ref[...] = jnp.zeros_like(acc_ref)

    acc_ref[...] += _swiglu_partial(xb_ref[...], wg_ref[...], wu_ref[...], wd_ref[...])

    @pl.when(f == pl.num_programs(1) - 1)
    def _():
        out = _layer_norm_rows(ALPHA * x_ref[...] + acc_ref[...], g_ref[...], b_ref[...])
        o_ref[...] = out
        ob_ref[...] = out.astype(BF16)


def _ffn_ln(x, xb, w_gate, w_up, w_down, gamma, beta):
    T = x.shape[0]
    F = w_gate.shape[1]
    tm = min(512, T)
    tf = 512
    row = lambda i, f: (i, 0)
    fixed = lambda i, f: (0, 0)
    return pl.pallas_call(
        _ffn_ln_kernel,
        out_shape=(jax.ShapeDtypeStruct((T, D_MODEL), F32), jax.ShapeDtypeStruct((T, D_MODEL), BF16)),
        grid=(T // tm, F // tf),
        in_specs=[pl.BlockSpec((tm, D_MODEL), row),
                  pl.BlockSpec((D_MODEL, tf), lambda i, f: (0, f)),
                  pl.BlockSpec((D_MODEL, tf), lambda i, f: (0, f)),
                  pl.BlockSpec((tf, D_MODEL), lambda i, f: (f, 0)),
                  pl.BlockSpec((tm, D_MODEL), row),
                  pl.BlockSpec((1, D_MODEL), fixed), pl.BlockSpec((1, D_MODEL), fixed)],
        out_specs=(pl.BlockSpec((tm, D_MODEL), row), pl.BlockSpec((tm, D_MODEL), row)),
        scratch_shapes=[pltpu.VMEM((tm, D_MODEL), F32)],
        compiler_params=_cparams("parallel", "arbitrary"),
        name="ffn_ln",
    )(xb, w_gate.astype(BF16), w_up.astype(BF16), w_down.astype(BF16), x,
      gamma.reshape(1, -1), beta.reshape(1, -1))


MOE_TM = 1024
MOE_TF = 256
MOE_RUNS = 4


def _moe_kernel(te_ref, nr_ref, xs_ref, wg_ref, wu_ref, wd_ref, *rest):
    o_ref, acc_ref = rest[-2:]
    i = pl.program_id(0)
    f = pl.program_id(1)
    last = pl.num_programs(1) - 1
    rows = nr_ref[i]
    half = MOE_TM // 2
    used = rows > 0

    @pl.when(jnp.logical_and(used, f == 0))
    def _():
        acc_ref[...] = jnp.zeros_like(acc_ref)

    @pl.when(rows > half)
    def _():
        acc_ref[...] += _swiglu_partial(xs_ref[...], wg_ref[...].astype(BF16), wu_ref[...].astype(BF16),
                                        wd_ref[...].astype(BF16))

    @pl.when(jnp.logical_and(used, rows <= half))
    def _():
        acc_ref[:half, :] += _swiglu_partial(xs_ref[:half, :], wg_ref[...].astype(BF16),
                                             wu_ref[...].astype(BF16), wd_ref[...].astype(BF16))

    @pl.when(jnp.logical_and(used, f == last))
    def _():
        o_ref[...] = acc_ref[...].astype(o_ref.dtype)

    @pl.when(jnp.logical_and(jnp.logical_not(used), f == last))
    def _():
        o_ref[...] = jnp.zeros_like(o_ref)


def _moe_grouped(xs, tile_expert, tile_rows, w_gate, w_up, w_down, ys_prev, tile0, total_tiles):
    tiles = xs.shape[0] // MOE_TM
    F = w_gate.shape[2]
    nf = F // MOE_TF

    def fcol(i, f, nr):
        return jnp.where(nr[i] > 0, f, nf - 1)

    in_specs = [pl.BlockSpec((MOE_TM, D_MODEL), lambda i, f, te, nu: (i, 0)),
                pl.BlockSpec((None, D_MODEL, MOE_TF), lambda i, f, te, nu: (te[i], 0, fcol(i, f, nu))),
                pl.BlockSpec((None, D_MODEL, MOE_TF), lambda i, f, te, nu: (te[i], 0, fcol(i, f, nu))),
                pl.BlockSpec((None, MOE_TF, D_MODEL), lambda i, f, te, nu: (te[i], fcol(i, f, nu), 0))]
    args = [tile_expert, tile_rows, xs, w_gate, w_up, w_down]
    aliases = {}
    if ys_prev is not None:
        in_specs.append(pl.BlockSpec(memory_space=pl.ANY))
        aliases = {len(args): 0}
        args.append(ys_prev)
    grid_spec = pltpu.PrefetchScalarGridSpec(
        num_scalar_prefetch=2,
        grid=(tiles, nf),
        in_specs=in_specs,
        out_specs=pl.BlockSpec((MOE_TM, D_MODEL), lambda i, f, te, nu: (tile0 + i, 0)),
        scratch_shapes=[pltpu.VMEM((MOE_TM, D_MODEL), F32)],
    )
    return pl.pallas_call(
        _moe_kernel,
        out_shape=jax.ShapeDtypeStruct((total_tiles * MOE_TM, D_MODEL), BF16),
        grid_spec=grid_spec,
        input_output_aliases=aliases,
        compiler_params=pltpu.CompilerParams(dimension_semantics=("arbitrary", "arbitrary"),
                                             vmem_limit_bytes=MOE_VMEM_LIMIT),
        name="moe_grouped",
    )(*args)


def _combine_ln_kernel(x_ref, ya_ref, yb_ref, w_ref, g_ref, b_ref, o_ref, ob_ref):
    w = w_ref[...]
    f = w[:, 0:1] * ya_ref[...].astype(F32) + w[:, 1:2] * yb_ref[...].astype(F32)
    out = _layer_norm_rows(ALPHA * x_ref[...] + f, g_ref[...], b_ref[...])
    o_ref[...] = out
    ob_ref[...] = out.astype(BF16)


def _combine_ln(x, y_rows, top_w, gamma, beta):
    T = x.shape[0]
    tm = min(512, T)
    nt = T // tm
    wpad = jnp.pad(top_w, ((0, 0), (0, LANES - TOP_K)))
    fixed = lambda i: (0, 0)
    row = lambda i: (i, 0)
    return pl.pallas_call(
        _combine_ln_kernel,
        out_shape=(jax.ShapeDtypeStruct((T, D_MODEL), F32), jax.ShapeDtypeStruct((T, D_MODEL), BF16)),
        grid=(nt,),
        in_specs=[pl.BlockSpec((tm, D_MODEL), row),
                  pl.BlockSpec((tm, D_MODEL), row),
                  pl.BlockSpec((tm, D_MODEL), lambda i: (i + nt, 0)),
                  pl.BlockSpec((tm, LANES), row),
                  pl.BlockSpec((1, D_MODEL), fixed), pl.BlockSpec((1, D_MODEL), fixed)],
        out_specs=(pl.BlockSpec((tm, D_MODEL), row), pl.BlockSpec((tm, D_MODEL), row)),
        compiler_params=_cparams("parallel"),
        name="combine_ln",
    )(x, y_rows, y_rows, wpad, gamma.reshape(1, -1), beta.reshape(1, -1))


def _moe_ln(x, xb, logits, w_gate, w_up, w_down, gamma, beta):
    T = x.shape[0]
    top_val, top_idx = lax.top_k(logits, TOP_K)
    top_w = jax.nn.softmax(top_val, axis=-1)

    e_flat = top_idx.reshape(-1).astype(jnp.int32)
    onehot = (e_flat[:, None] == jnp.arange(N_EXPERTS, dtype=jnp.int32)[None, :]).astype(jnp.int32)
    rank = jnp.sum((jnp.cumsum(onehot, axis=0) - onehot) * onehot, axis=1)
    counts = jnp.sum(onehot, axis=0)
    padded = ((counts + MOE_TM - 1) // MOE_TM) * MOE_TM
    ends = jnp.cumsum(padded)
    starts = ends - padded
    dest = starts[e_flat] + rank

    Tp = TOP_K * T + N_EXPERTS * MOE_TM
    n_tiles = Tp // MOE_TM
    token = jnp.arange(TOP_K * T, dtype=jnp.int32) // TOP_K
    src = jnp.zeros((Tp,), jnp.int32).at[dest].set(token)
    tile_start = jnp.arange(n_tiles, dtype=jnp.int32) * MOE_TM
    tile_expert = jnp.sum((tile_start[:, None] >= ends[None, :]).astype(jnp.int32), axis=1)
    last_expert = jnp.sum((jnp.maximum(ends[-1] - 1, 0) >= ends).astype(jnp.int32))
    tile_expert = jnp.minimum(tile_expert, last_expert).astype(jnp.int32)
    filled_end = (starts + counts)[tile_expert]
    tile_rows = jnp.where(tile_start < ends[-1], jnp.clip(filled_end - tile_start, 0, MOE_TM), 0).astype(jnp.int32)

    runs = max(d for d in range(1, MOE_RUNS + 1) if n_tiles % d == 0)
    tiles = n_tiles // runs
    ys = None
    for c in range(runs):
        xs = xb.at[src[c * tiles * MOE_TM:(c + 1) * tiles * MOE_TM]].get(mode="promise_in_bounds")
        ys = _moe_grouped(xs, tile_expert[c * tiles:(c + 1) * tiles], tile_rows[c * tiles:(c + 1) * tiles],
                          w_gate, w_up, w_down, ys, c * tiles, n_tiles)
    slot_major = dest.reshape(T, TOP_K).T.reshape(-1)
    y_rows = ys.at[slot_major].get(mode="promise_in_bounds")
    return _combine_ln(x, y_rows, top_w, gamma, beta)


def _memattn_kernel(q_ref, k_ref, v_ref, o_ref):
    q = q_ref[0]
    outs = []
    for h in range(MEM_HEADS):
        sl = slice(h * MEM_HEAD_DIM, (h + 1) * MEM_HEAD_DIM)
        s = _dot_nt(q[:, sl].astype(BF16), k_ref[0, :, sl]) * (MEM_HEAD_DIM ** -0.5)
        e = jnp.exp(s - jnp.max(s, axis=-1, keepdims=True))
        p = e / jnp.sum(e, axis=-1, keepdims=True)
        outs.append(_dot(p.astype(BF16), v_ref[0, :, sl]))
    o_ref[0] = jnp.concatenate(outs, axis=-1).astype(o_ref.dtype)


def _memory_attention(p3, mem_kv):
    B, S, _ = p3.shape
    M = mem_kv.shape[1]
    ts = min(512, S)
    return pl.pallas_call(
        _memattn_kernel,
        out_shape=jax.ShapeDtypeStruct((B, S, MEM_WIDTH), BF16),
        grid=(B, S // ts),
        in_specs=[pl.BlockSpec((1, ts, MEM_WIDTH), lambda b, s: (b, s, Q_MEM_OFF // MEM_WIDTH)),
                  pl.BlockSpec((1, M, MEM_WIDTH), lambda b, s: (b, 0, 0)),
                  pl.BlockSpec((1, M, MEM_WIDTH), lambda b, s: (b, 0, 1))],
        out_specs=pl.BlockSpec((1, ts, MEM_WIDTH), lambda b, s: (b, s, 0)),
        compiler_params=_cparams("parallel", "parallel"),
        name="mem_attn",
    )(p3, mem_kv, mem_kv)


def _ret_kernel(q_ref, k_ref, v_ref, g_ref, cos_ref, sin_ref, intra_ref, qd_ref, kd_ref, cd_ref,
                gn_ref, o_ref, state_ref):
    @pl.when(pl.program_id(1) == 0)
    def _():
        state_ref[...] = jnp.zeros_like(state_ref)

    cosf = cos_ref[0]
    sinf = sin_ref[0]
    half = RET_DK // 2
    rot = lambda x: x * cosf + pltpu.roll(x, half, 1) * sinf
    hs = range(RET_HEADS)
    ksl = [slice(h * RET_DK, (h + 1) * RET_DK) for h in hs]
    vsl = [slice(h * RET_DV, (h + 1) * RET_DV) for h in hs]

    qr = [rot(q_ref[0, :, ksl[h]].astype(F32)) for h in hs]
    kr = [rot(k_ref[0, :, ksl[h]].astype(F32)) * (RET_DK ** -0.5) for h in hs]
    vb = [v_ref[0, :, vsl[h]].astype(BF16) for h in hs]
    state = [state_ref[h] for h in hs]
    s = [_dot_nt(qr[h].astype(BF16), kr[h].astype(BF16)) * intra_ref[h] for h in hs]
    o = [_dot(s[h].astype(BF16), vb[h]) + _dot((qr[h] * qd_ref[h]).astype(BF16), state[h].astype(BF16))
         for h in hs]
    for h in hs:
        state_ref[h] = state[h] * cd_ref[h] + _dot_tn((kr[h] * kd_ref[h]).astype(BF16), vb[h])
    for h in hs:
        mu = jnp.mean(o[h], axis=-1, keepdims=True)
        oc = o[h] - mu
        var = jnp.mean(oc * oc, axis=-1, keepdims=True)
        y = oc * lax.rsqrt(var + LN_EPS) * gn_ref[0:1, vsl[h]] + gn_ref[1:2, vsl[h]]
        g = g_ref[0, :, vsl[h]].astype(F32)
        o_ref[0, :, vsl[h]] = (y * (g * jax.nn.sigmoid(g))).astype(o_ref.dtype)


def _retention_mixer(p3, positions, gn):
    B, S, _ = p3.shape
    C = RET_CHUNK
    H = RET_HEADS
    inv = 1.0 / (ROPE_BASE ** (jnp.arange(0, RET_DK, 2, dtype=F32) / RET_DK))
    ang = positions.astype(F32)[..., None] * inv
    cos, sin = jnp.cos(ang), jnp.sin(ang)
    cosf = jnp.concatenate([cos, cos], axis=-1)
    sinf = jnp.concatenate([-sin, sin], axis=-1)

    log_g = jnp.log1p(-jnp.exp2(-5.0 - jnp.arange(H, dtype=F32)))
    idx = jnp.arange(C, dtype=F32)
    diff = idx[:, None] - idx[None, :]
    intra = jnp.where(diff >= 0, jnp.exp(jnp.maximum(diff, 0.0)[None] * log_g[:, None, None]), 0.0)
    qd = jnp.broadcast_to(jnp.exp((idx + 1.0)[None, :, None] * log_g[:, None, None]), (H, C, RET_DK))
    kd = jnp.broadcast_to(jnp.exp((C - 1.0 - idx)[None, :, None] * log_g[:, None, None]), (H, C, RET_DK))
    cd = jnp.broadcast_to(jnp.exp(C * log_g)[:, None, None], (H, 1, RET_DV))

    QK = H * RET_DK
    assert 2 * QK == SEQ_WIDTH
    whole = lambda b, c: (0, 0, 0)
    return pl.pallas_call(
        _ret_kernel,
        out_shape=jax.ShapeDtypeStruct((B, S, SEQ_WIDTH), BF16),
        grid=(B, S // C),
        in_specs=[pl.BlockSpec((1, C, QK), lambda b, c: (b, c, 0)),
                  pl.BlockSpec((1, C, QK), lambda b, c: (b, c, 1)),
                  pl.BlockSpec((1, C, SEQ_WIDTH), lambda b, c: (b, c, 1)),
                  pl.BlockSpec((1, C, SEQ_WIDTH), lambda b, c: (b, c, 2)),
                  pl.BlockSpec((1, C, RET_DK), lambda b, c: (b, c, 0)),
                  pl.BlockSpec((1, C, RET_DK), lambda b, c: (b, c, 0)),
                  pl.BlockSpec((H, C, C), whole),
                  pl.BlockSpec((H, C, RET_DK), whole),
                  pl.BlockSpec((H, C, RET_DK), whole),
                  pl.BlockSpec((H, 1, RET_DV), whole),
                  pl.BlockSpec((2, SEQ_WIDTH), lambda b, c: (0, 0))],
        out_specs=pl.BlockSpec((1, C, SEQ_WIDTH), lambda b, c: (b, c, 0)),
        scratch_shapes=[pltpu.VMEM((H, RET_DK, RET_DV), F32)],
        compiler_params=_cparams("parallel", "arbitrary"),
        name="retention",
    )(p3, p3, p3, p3, cosf, sinf, intra, qd, kd, cd, gn)


RWKV_PAIRS = 12
RWKV_SUB = 2
PAIR = 2 * RWKV_HEAD
PREV_ROWS = 16

_NN = ((1,), (0,))
_NT = ((1,), (1,))
_TN = ((0,), (0,))


def _dg(a, b, dims):
    return lax.dot_general(a.astype(BF16), b.astype(BF16), (dims, ((), ())), preferred_element_type=F32)


def _rwkv_masks():
    L = RWKV_CHUNK
    row = np.arange(L)[:, None]
    lane = np.arange(PAIR)[None, :]
    m1 = np.broadcast_to(lane < RWKV_HEAD, (L, PAIR))
    small = np.stack([m1, ~m1, row > lane % L, row >= lane % L, row == lane % L]).astype(np.float32)
    same_head = (np.arange(2 * L)[:, None] // L == lane // L).astype(np.float32)
    tri = np.kron(np.eye(RWKV_SUB), row >= np.arange(L)[None, :]).astype(np.float32)
    return (jnp.asarray(small), jnp.asarray(same_head), jnp.asarray(same_head, dtype=BF16),
            jnp.asarray(tri, dtype=BF16))


def _rwkv_prepare(r, c, lw, k, v, kk, a, masks):
    n = len(r)
    rng = range(n)
    L = r[0].shape[0]
    assert L == RWKV_HEAD
    m1, m2, strict, incl, eye, same_head = masks
    stack = lambda x: jnp.concatenate([x * m1, x * m2], axis=0)
    blockdiag = lambda x: (jnp.concatenate([x, x], axis=0) * same_head).astype(BF16)

    p_incl = [jnp.exp(c[j]) for j in rng]
    p_inv = [jnp.exp(-c[j]) for j in rng]
    at = [-kk[j] * jnp.exp(c[j] - lw[j]) for j in rng]
    rt = [r[j] * p_incl[j] for j in rng]
    lhs = [jnp.concatenate([at[j], rt[j]], axis=0).astype(BF16) for j in rng]
    rhs = [jnp.concatenate([stack(kk[j] * a[j] * p_inv[j]), stack(k[j] * p_inv[j])], axis=0).astype(BF16)
           for j in rng]
    sc = [_dg(lhs[j], rhs[j], _NT) for j in rng]
    nmat = [sc[j][:L, :PAIR] * strict for j in rng]
    tinv = [eye + nmat[j] for j in rng]
    pw = [_dg(nmat[j], blockdiag(nmat[j]), _NN) for j in rng]
    rounds = int(np.log2(L)) - 1
    for i in range(rounds):
        if i < rounds - 1:
            both = [_dg(jnp.concatenate([pw[j], tinv[j]], axis=0), blockdiag(pw[j]), _NN) for j in rng]
            pw = [both[j][:L] for j in rng]
            tinv = [tinv[j] + both[j][L:] for j in rng]
        else:
            tinv = [tinv[j] + _dg(tinv[j], blockdiag(pw[j]), _NN) for j in rng]

    vs = [stack(v[j]).astype(BF16) for j in rng]
    mv = [_dg(sc[j][:L, PAIR:] * strict, vs[j], _NN) for j in rng]
    a_r = [(sc[j][L:, :] * jnp.concatenate([incl, incl], axis=1)).astype(BF16) for j in rng]
    return dict(lhs=lhs, rhs=rhs, vs=vs, mv=mv, a_r=a_r, tinv=[tinv[j].astype(BF16) for j in rng],
                decay=[p_incl[j][L - 1:L, :] for j in rng])


def _rwkv_apply(prep, entries, state, masks):
    m1, m2 = masks[0], masks[1]
    stack = lambda x: jnp.concatenate([x * m1, x * m2], axis=0)
    L = m1.shape[0]
    rng = range(len(entries))
    get = lambda name: [prep[name][e] for e in entries]
    lhs, rhs, vs, mv, a_r, tinv, decay = (get(n) for n in ("lhs", "rhs", "vs", "mv", "a_r", "tinv", "decay"))
    sg = [_dg(lhs[j], state[j], _NT) for j in rng]
    u = [_dg(tinv[j], stack(sg[j][:L] + mv[j]), _NN) for j in rng]
    uv = [jnp.concatenate([stack(u[j]).astype(BF16), vs[j]], axis=0) for j in rng]
    ys = [_dg(a_r[j], uv[j], _NN) for j in rng]
    y = [sg[j][L:] + ys[j] for j in rng]
    new_state = [(state[j] + _dg(uv[j], rhs[j], _TN)) * decay[j] for j in rng]
    return y, new_state


def _shift_mix(x, prev_block, mu, first):
    prev_row = jnp.where(first, 0.0, prev_block[PREV_ROWS - 1:PREV_ROWS, :])
    rows = lax.broadcasted_iota(jnp.int32, x.shape, 0)
    prev = jnp.where(rows == 0, prev_row, pltpu.roll(x, 1, 0))
    return x + mu * (prev - x)


def _rwkv_mixer_kernel(*refs, with_v):
    it = iter(refs)
    r_ref, k_ref, v_ref, xg_ref, xl_ref = (next(it) for _ in range(5))
    rp_ref, kp_ref, vp_ref, xgp_ref, xlp_ref = (next(it) for _ in range(5))
    mu3_ref, mug_ref, mul_ref, vec_ref = (next(it) for _ in range(4))
    small_ref, same_head_ref, ones_ref, tri_ref = (next(it) for _ in range(4))
    ww_ref, wa_ref = next(it), next(it)
    wv_ref = next(it) if with_v else None
    wg_ref = next(it)
    vf_ref = next(it) if with_v else None
    y_ref = next(it)
    vfo_ref = None if with_v else next(it)
    state_ref = next(it)

    first = pl.program_id(2) == 0

    @pl.when(first)
    def _():
        state_ref[...] = jnp.zeros_like(state_ref)

    L = RWKV_CHUNK
    mix = lambda x_ref, p_ref, mu: _shift_mix(x_ref[0].astype(F32), p_ref[0].astype(F32), mu, first)
    r = mix(r_ref, rp_ref, mu3_ref[0:1, :])
    k = mix(k_ref, kp_ref, mu3_ref[1:2, :])
    v = mix(v_ref, vp_ref, mu3_ref[2:3, :])
    xg = mix(xg_ref, xgp_ref, mug_ref[...])
    xl = mix(xl_ref, xlp_ref, mul_ref[...])

    w0, a0, v0 = vec_ref[0:1, :], vec_ref[1:2, :], vec_ref[2:3, :]
    k_k, k_a, r_k = vec_ref[3:4, :], vec_ref[4:5, :], vec_ref[5:6, :]
    gn_g, gn_b = vec_ref[6:7, :], vec_ref[7:8, :]

    lane = lax.broadcasted_iota(jnp.int32, xl.shape, 1)
    lin = jnp.where(lane < LORA_W, jnp.tanh(xl), xl).astype(BF16)
    w_arg = w0 + _dot(lin, ww_ref[...])
    a = jax.nn.sigmoid(a0 + _dot(lin, wa_ref[...]))
    if with_v:
        v = v + (vf_ref[0] - v) * jax.nn.sigmoid(v0 + _dot(lin, wv_ref[...]))
    else:
        vfo_ref[0] = v
    g = _dot(jax.nn.sigmoid(xg).astype(BF16), wg_ref[...])

    w_log = -(jnp.maximum(-w_arg, 0.0) + jnp.log1p(jnp.exp(-jnp.abs(w_arg)))) - 0.5
    lw = -jnp.exp(w_log)
    tri = tri_ref[...]
    h1 = lw.astype(BF16)
    rem = lw - h1.astype(F32)
    h2 = rem.astype(BF16)
    h3 = (rem - h2.astype(F32)).astype(BF16)
    c = _dot(tri, h1) + _dot(tri, h2) + _dot(tri, h3)

    kk_raw = k * k_k
    k2 = k * (1.0 + (a - 1.0) * k_a)
    ones_bd = ones_ref[...]
    masks = tuple(small_ref[i] for i in range(5)) + (same_head_ref[...],)
    inv_n = 1.0 / RWKV_HEAD

    pairs = range(RWKV_PAIRS)
    cols = [slice(j * PAIR, (j + 1) * PAIR) for j in pairs]
    rows = [slice(s * L, (s + 1) * L) for s in range(RWKV_SUB)]
    where = [(rs, cs) for rs in rows for cs in cols]
    ent = range(len(where))
    part = lambda x: [x[rs, cs] for rs, cs in where]
    kkr, r_e, k_e, v_e, g_e = part(kk_raw), part(r), part(k2), part(v), part(g)
    red = [_dot(jnp.concatenate([kkr[e] * kkr[e], r_e[e] * k_e[e] * r_k[:, where[e][1]]], axis=0).astype(BF16),
                ones_bd) for e in ent]
    kks = [kkr[e] / jnp.maximum(jnp.sqrt(red[e][:L]), 1e-12) for e in ent]
    prep = _rwkv_prepare(r_e, part(c), part(lw), k_e, v_e, kks, part(a), masks)
    state = [state_ref[j] for j in pairs]
    ys = []
    for s in range(RWKV_SUB):
        y_s, state = _rwkv_apply(prep, [s * RWKV_PAIRS + j for j in pairs], state, masks)
        ys += y_s
    for j in pairs:
        state_ref[j] = state[j]
    yc = [ys[e] - _dot(ys[e].astype(BF16), ones_bd) * inv_n for e in ent]
    var = [_dot((yc[e] * yc[e]).astype(BF16), ones_bd) * inv_n for e in ent]
    for e in ent:
        rs, cs = where[e]
        yn = yc[e] * lax.rsqrt(var[e] + RWKV_GN_EPS) * gn_g[:, cs] + gn_b[:, cs]
        y_ref[0, rs, cs] = ((yn + red[e][L:] * v_e[e]) * g_e[e]).astype(y_ref.dtype)


def _rwkv_layout(w_in, mu, with_v):
    sizes = [SEQ_WIDTH] * 3 + [LORA_W, LORA_A] + ([LORA_V] if with_v else []) + [LORA_G]
    offs = np.concatenate([[0], np.cumsum(sizes)])
    seq_cols = int(offs[-1])
    g_idx = len(sizes) - 1
    pad = LORA_PACK - LORA_W - LORA_A - (LORA_V if with_v else 0)
    cols = [w_in[:, :3 * SEQ_WIDTH], w_in[:, seq_cols:], w_in[:, offs[g_idx]:offs[g_idx + 1]],
            w_in[:, offs[3]:offs[g_idx]], jnp.zeros((w_in.shape[0], pad), w_in.dtype)]
    mus = [mu[:3 * SEQ_WIDTH], jnp.zeros((MEM_WIDTH,), mu.dtype), mu[offs[g_idx]:offs[g_idx + 1]],
           mu[offs[3]:offs[g_idx]], jnp.zeros((pad,), mu.dtype)]
    return jnp.concatenate(cols, axis=1), jnp.concatenate(mus)


def _rwkv_mixer(p3, mu, lora_w, lora_a, lora_g, lora_v, vecs, v_first):
    B, S, _ = p3.shape
    L = RWKV_SUB * RWKV_CHUNK
    W = RWKV_PAIRS * PAIR
    above = lambda t: jnp.maximum(t * (L // PREV_ROWS) - 1, 0)
    with_v = lora_v is not None
    if with_v:
        w0, a0, v0, k_k, k_a, r_k, gn_g, gn_b = vecs
    else:
        w0, a0, k_k, k_a, r_k, gn_g, gn_b = vecs
        v0 = jnp.zeros_like(w0)
    vec8 = jnp.stack([w0, a0, v0, k_k, k_a, r_k, gn_g, gn_b])
    mu3 = jnp.concatenate([mu[:3 * SEQ_WIDTH].reshape(3, SEQ_WIDTH), jnp.zeros((5, SEQ_WIDTH), F32)])
    g_off = Q_MEM_OFF + MEM_WIDTH
    l_off = g_off + LORA_G
    mug = mu[g_off:g_off + LORA_G].reshape(1, LORA_G)
    mul = mu[l_off:l_off + LORA_PACK].reshape(1, LORA_PACK)
    rows = lambda w, lo: jnp.zeros((LORA_PACK, SEQ_WIDTH), F32).at[lo:lo + w.shape[0]].set(w).astype(BF16)

    def cols(width, off):
        blk = off // width
        return pl.BlockSpec((1, L, width), lambda b, g, t: (b, t, blk))

    def cols_prev(width, off):
        blk = off // width
        return pl.BlockSpec((1, PREV_ROWS, width), lambda b, g, t: (b, above(t), blk))

    def seq(off):
        blk = off // W
        return pl.BlockSpec((1, L, W), lambda b, g, t: (b, t, blk + g))

    def seq_prev(off):
        blk = off // W
        return pl.BlockSpec((1, PREV_ROWS, W), lambda b, g, t: (b, above(t), blk + g))

    per_group = lambda nrows: pl.BlockSpec((nrows, W), lambda b, g, t: (0, g))
    fixed = lambda ncols: pl.BlockSpec((1, ncols), lambda b, g, t: (0, 0))

    in_specs = [seq(0), seq(SEQ_WIDTH), seq(2 * SEQ_WIDTH), cols(LORA_G, g_off), cols(LORA_PACK, l_off),
                seq_prev(0), seq_prev(SEQ_WIDTH), seq_prev(2 * SEQ_WIDTH),
                cols_prev(LORA_G, g_off), cols_prev(LORA_PACK, l_off),
                per_group(8), fixed(LORA_G), fixed(LORA_PACK), per_group(8)]
    masks = _rwkv_masks()
    in_specs += [pl.BlockSpec(m.shape, lambda b, g, t, nd=m.ndim: (0,) * nd) for m in masks]
    in_specs += [per_group(LORA_PACK), per_group(LORA_PACK)]
    args = [p3] * 10 + [mu3, mug, mul, vec8, *masks, rows(lora_w, 0), rows(lora_a, LORA_W)]
    if with_v:
        in_specs.append(per_group(LORA_PACK))
        args.append(rows(lora_v, LORA_W + LORA_A))
    in_specs.append(per_group(LORA_G))
    args.append(lora_g.astype(BF16))
    out_spec = pl.BlockSpec((1, L, W), lambda b, g, t: (b, t, g))
    y_shape = jax.ShapeDtypeStruct((B, S, SEQ_WIDTH), BF16)
    if with_v:
        in_specs.append(out_spec)
        args.append(v_first)
        out_shape, out_specs = y_shape, out_spec
    else:
        out_shape = (y_shape, jax.ShapeDtypeStruct((B, S, SEQ_WIDTH), F32))
        out_specs = (out_spec, out_spec)

    res = pl.pallas_call(
        functools.partial(_rwkv_mixer_kernel, with_v=with_v),
        out_shape=out_shape,
        grid=(B, SEQ_WIDTH // W, S // L),
        in_specs=in_specs,
        out_specs=out_specs,
        scratch_shapes=[pltpu.VMEM((RWKV_PAIRS, PAIR, PAIR), F32)],
        compiler_params=_cparams("parallel", "parallel", "arbitrary"),
        name="rwkv_mixer",
    )(*args)
    return (res, v_first) if with_v else res


def kernel(x, mem, positions, mem_w_kv, l0_w_in, l0_mu, l0_lora_w, l0_lora_a, l0_lora_g, l0_vecs, l0_w_out, l0_norms, l0_ffn_gate, l0_ffn_up, l0_ffn_down, l1_w_in, l1_gn, l1_w_out, l1_norms, l1_router, l1_moe_gate, l1_moe_up, l1_moe_down, l2_w_in, l2_mu, l2_lora_w, l2_lora_a, l2_lora_v, l2_lora_g, l2_vecs, l2_w_out, l2_norms, l2_ffn_gate, l2_ffn_up, l2_ffn_down, l3_w_in, l3_gn, l3_w_out, l3_norms, l3_router, l3_moe_gate, l3_moe_up, l3_moe_down):
    B, S, D = x.shape
    T = B * S
    M = mem.shape[1]
    mem_kv = _matmul(mem.reshape(B * M, D), mem_w_kv.astype(BF16), tm=1024, tn=512,
                     out_dtype=BF16).reshape(B, M, 2 * MEM_WIDTH)

    rwkv_layers = {
        0: (l0_w_in, l0_mu, l0_lora_w, l0_lora_a, l0_lora_g, None, l0_vecs),
        2: (l2_w_in, l2_mu, l2_lora_w, l2_lora_a, l2_lora_g, l2_lora_v, l2_vecs),
    }
    ret_layers = {1: (l1_w_in, l1_gn), 3: (l3_w_in, l3_gn)}
    w_outs = (l0_w_out, l1_w_out, l2_w_out, l3_w_out)
    norms = (l0_norms, l1_norms, l2_norms, l3_norms)
    dense = {0: (l0_ffn_gate, l0_ffn_up, l0_ffn_down), 2: (l2_ffn_gate, l2_ffn_up, l2_ffn_down)}
    moe = {1: (l1_router, l1_moe_gate, l1_moe_up, l1_moe_down),
           3: (l3_router, l3_moe_gate, l3_moe_up, l3_moe_down)}

    xf = x.reshape(T, D)
    xin = xf
    v_first = None
    for i in range(DEPTH):
        if i in rwkv_layers:
            w_in, mu, lw_, la_, lg_, lv_, vecs = rwkv_layers[i]
            w_in, mu = _rwkv_layout(w_in, mu, lv_ is not None)
            p3 = _matmul(xin, w_in.astype(BF16), tm=512, tn=w_in.shape[1], out_dtype=BF16).reshape(B, S, -1)
            y_seq, v_first = _rwkv_mixer(p3, mu, lw_, la_, lg_, lv_, vecs, v_first)
        else:
            w_in, gn = ret_layers[i]
            p3 = _matmul(xin, w_in.astype(BF16), tm=512, tn=w_in.shape[1], out_dtype=BF16).reshape(B, S, -1)
            y_seq = _retention_mixer(p3, positions, gn)
        y_mem = _memory_attention(p3, mem_kv)
        nrm = norms[i]
        proj = _proj_ln(y_seq.reshape(T, SEQ_WIDTH), y_mem.reshape(T, MEM_WIDTH), w_outs[i], xf,
                        nrm[0], nrm[1], None if i in dense else moe[i][0])
        if i in dense:
            xf, xb = _ffn_ln(*proj, *dense[i], nrm[2], nrm[3])
        else:
            xf, xb = _moe_ln(*proj, *moe[i][1:], nrm[2], nrm[3])
        xin = xb
    return xf.reshape(B, S, D)
```

```python
import functools

import numpy as np
import jax
import jax.numpy as jnp
from jax import lax
from jax.experimental import pallas as pl
from jax.experimental.pallas import tpu as pltpu

F32 = jnp.float32
BF16 = jnp.bfloat16

D_MODEL = 2048
DEPTH = 4
MEM_HEADS = 4
MEM_HEAD_DIM = 128
MEM_WIDTH = MEM_HEADS * MEM_HEAD_DIM
SEQ_WIDTH = D_MODEL - MEM_WIDTH

RWKV_HEAD = 64
RWKV_HEADS = SEQ_WIDTH // RWKV_HEAD
LORA_W = 96
LORA_A = 96
LORA_V = 64
LORA_G = 256
LORA_PACK = 256
RWKV_GN_EPS = 1e-5 * RWKV_HEAD
RWKV_CHUNK = 64

RET_HEADS = 6
RET_DV = SEQ_WIDTH // RET_HEADS
RET_DK = RET_DV // 2
RET_CHUNK = 128
ROPE_BASE = 10000.0

N_EXPERTS = 8
TOP_K = 2

ALPHA = (2.0 * DEPTH) ** 0.25
LN_EPS = 1e-5

LANES = 128
Q_MEM_OFF = 3 * SEQ_WIDTH
VMEM_LIMIT = 52 * 1024 * 1024
MOE_VMEM_LIMIT = 58 * 1024 * 1024


def _cparams(*semantics):
    return pltpu.CompilerParams(dimension_semantics=semantics, vmem_limit_bytes=VMEM_LIMIT)


def _dot(a, b):
    return jnp.dot(a, b, preferred_element_type=F32)


def _dot_nt(a, b):
    return lax.dot_general(a, b, (((1,), (1,)), ((), ())), preferred_element_type=F32)


def _dot_tn(a, b):
    return lax.dot_general(a, b, (((0,), (0,)), ((), ())), preferred_element_type=F32)


def _layer_norm_rows(z, g, b):
    mu = jnp.mean(z, axis=-1, keepdims=True)
    zc = z - mu
    var = jnp.mean(zc * zc, axis=-1, keepdims=True)
    return zc * lax.rsqrt(var + LN_EPS) * g + b


def _mm_kernel(a_ref, w_ref, o_ref, abf_ref):
    @pl.when(pl.program_id(1) == 0)
    def _():
        abf_ref[...] = a_ref[...].astype(BF16)

    o_ref[...] = _dot(abf_ref[...], w_ref[...]).astype(o_ref.dtype)


def _matmul(a, w, *, tm, tn, out_dtype=F32):
    M, K = a.shape
    N = w.shape[1]
    tm = min(tm, M)
    assert M % tm == 0 and N % tn == 0
    w_mode = pl.Buffered(1) if tn == N else None
    return pl.pallas_call(
        _mm_kernel,
        out_shape=jax.ShapeDtypeStruct((M, N), out_dtype),
        grid=(M // tm, N // tn),
        in_specs=[pl.BlockSpec((tm, K), lambda i, j: (i, 0)),
                  pl.BlockSpec((K, tn), lambda i, j: (0, j), pipeline_mode=w_mode)],
        out_specs=pl.BlockSpec((tm, tn), lambda i, j: (i, j)),
        scratch_shapes=[pltpu.VMEM((tm, K), BF16)],
        compiler_params=_cparams("parallel", "arbitrary"),
        name="matmul",
    )(a, w)


def _proj_ln_kernel(*refs, with_router):
    ys_ref, ym_ref, ws_ref, wm_ref, x_ref, g_ref, b_ref = refs[:7]
    if with_router:
        wrh_ref, wrl_ref, o_ref, ob_ref, logits_ref = refs[7:]
    else:
        o_ref, ob_ref = refs[7:]
    half = x_ref.shape[0] // 2
    for s in range(2):
        rs = slice(s * half, (s + 1) * half)
        h = _dot(ys_ref[rs, :], ws_ref[...]) + _dot(ym_ref[rs, :], wm_ref[...])
        out = _layer_norm_rows(ALPHA * x_ref[rs, :] + h, g_ref[...], b_ref[...])
        hi = out.astype(BF16)
        o_ref[rs, :] = out
        ob_ref[rs, :] = hi
        if with_router:
            lo = (out - hi.astype(F32)).astype(BF16)
            logits_ref[rs, :] = _dot(hi, wrh_ref[...]) + (_dot(lo, wrh_ref[...]) + _dot(hi, wrl_ref[...]))


def _proj_ln(y_seq, y_mem, w_out, x, gamma, beta, w_router=None):
    T = x.shape[0]
    tm = min(512, T)
    ws = w_out[:SEQ_WIDTH].astype(BF16)
    wm = w_out[SEQ_WIDTH:].astype(BF16)
    row = lambda i: (i, 0)
    fixed = lambda i: (0, 0)
    once = pl.Buffered(1)
    in_specs = [pl.BlockSpec((tm, SEQ_WIDTH), row), pl.BlockSpec((tm, MEM_WIDTH), row),
                pl.BlockSpec((SEQ_WIDTH, D_MODEL), fixed, pipeline_mode=once),
                pl.BlockSpec((MEM_WIDTH, D_MODEL), fixed, pipeline_mode=once),
                pl.BlockSpec((tm, D_MODEL), row),
                pl.BlockSpec((1, D_MODEL), fixed), pl.BlockSpec((1, D_MODEL), fixed)]
    args = [y_seq, y_mem, ws, wm, x, gamma.reshape(1, -1), beta.reshape(1, -1)]
    out_shape = [jax.ShapeDtypeStruct((T, D_MODEL), F32), jax.ShapeDtypeStruct((T, D_MODEL), BF16)]
    out_specs = [pl.BlockSpec((tm, D_MODEL), row), pl.BlockSpec((tm, D_MODEL), row)]
    if w_router is not None:
        wr = jnp.pad(w_router, ((0, 0), (0, LANES - N_EXPERTS)))
        wr_hi = wr.astype(BF16)
        wr_lo = (wr - wr_hi.astype(F32)).astype(BF16)
        in_specs += [pl.BlockSpec((D_MODEL, LANES), fixed, pipeline_mode=once)] * 2
        args += [wr_hi, wr_lo]
        out_shape.append(jax.ShapeDtypeStruct((T, LANES), F32))
        out_specs.append(pl.BlockSpec((tm, LANES), row))
    res = pl.pallas_call(
        functools.partial(_proj_ln_kernel, with_router=w_router is not None),
        out_shape=tuple(out_shape),
        grid=(T // tm,),
        in_specs=in_specs,
        out_specs=tuple(out_specs),
        compiler_params=_cparams("parallel"),
        name="proj_ln",
    )(*args)
    if w_router is None:
        return res
    return res[0], res[1], res[2][:, :N_EXPERTS]


def _swiglu_partial(xb, wg, wu, wd, splits=1):
    w = wg.shape[1] // splits
    out = None
    for s in range(splits):
        cs = slice(s * w, (s + 1) * w)
        gate = _dot(xb, wg[:, cs])
        up = _dot(xb, wu[:, cs])
        h = (gate * jax.nn.sigmoid(gate) * up).astype(BF16)
        d = _dot(h, wd[cs, :])
        out = d if out is None else out + d
    return out


def _ffn_ln_kernel(xb_ref, wg_ref, wu_ref, wd_ref, x_ref, g_ref, b_ref, o_ref, ob_ref, acc_ref):
    f = pl.program_id(1)

    @pl.when(f == 0)
    def _():
        acc_ref[...] = jnp.zeros_like(acc_ref)

    acc_ref[...] += _swiglu_partial(xb_ref[...], wg_ref[...], wu_ref[...], wd_ref[...], splits=2)

    @pl.when(f == pl.num_programs(1) - 1)
    def _():
        out = _layer_norm_rows(ALPHA * x_ref[...] + acc_ref[...], g_ref[...], b_ref[...])
        o_ref[...] = out
        ob_ref[...] = out.astype(BF16)


def _ffn_ln(x, xb, w_gate, w_up, w_down, gamma, beta):
    T = x.shape[0]
    F = w_gate.shape[1]
    tm = min(512, T)
    tf = 512
    row = lambda i, f: (i, 0)
    fixed = lambda i, f: (0, 0)
    return pl.pallas_call(
        _ffn_ln_kernel,
        out_shape=(jax.ShapeDtypeStruct((T, D_MODEL), F32), jax.ShapeDtypeStruct((T, D_MODEL), BF16)),
        grid=(T // tm, F // tf),
        in_specs=[pl.BlockSpec((tm, D_MODEL), row),
                  pl.BlockSpec((D_MODEL, tf), lambda i, f: (0, f)),
                  pl.BlockSpec((D_MODEL, tf), lambda i, f: (0, f)),
                  pl.BlockSpec((tf, D_MODEL), lambda i, f: (f, 0)),
                  pl.BlockSpec((tm, D_MODEL), row),
                  pl.BlockSpec((1, D_MODEL), fixed), pl.BlockSpec((1, D_MODEL), fixed)],
        out_specs=(pl.BlockSpec((tm, D_MODEL), row), pl.BlockSpec((tm, D_MODEL), row)),
        scratch_shapes=[pltpu.VMEM((tm, D_MODEL), F32)],
        compiler_params=_cparams("parallel", "arbitrary"),
        name="ffn_ln",
    )(xb, w_gate.astype(BF16), w_up.astype(BF16), w_down.astype(BF16), x,
      gamma.reshape(1, -1), beta.reshape(1, -1))


MOE_TM = 1024
MOE_TF = 512
MOE_RUNS = 4


def _moe_kernel(te_ref, nr_ref, xs_ref, wg_ref, wu_ref, wd_ref, *rest):
    o_ref, acc_ref = rest[-2:]
    i = pl.program_id(0)
    f = pl.program_id(1)
    last = pl.num_programs(1) - 1
    rows = nr_ref[i]
    half = MOE_TM // 2
    used = rows > 0

    @pl.when(jnp.logical_and(used, f == 0))
    def _():
        acc_ref[...] = jnp.zeros_like(acc_ref)

    @pl.when(rows > half)
    def _():
        acc_ref[...] += _swiglu_partial(xs_ref[...], wg_ref[...].astype(BF16), wu_ref[...].astype(BF16),
                                        wd_ref[...].astype(BF16))

    @pl.when(jnp.logical_and(used, rows <= half))
    def _():
        acc_ref[:half, :] += _swiglu_partial(xs_ref[:half, :], wg_ref[...].astype(BF16),
                                             wu_ref[...].astype(BF16), wd_ref[...].astype(BF16))

    @pl.when(jnp.logical_and(used, f == last))
    def _():
        o_ref[...] = acc_ref[...].astype(o_ref.dtype)

    @pl.when(jnp.logical_and(jnp.logical_not(used), f == last))
    def _():
        o_ref[...] = jnp.zeros_like(o_ref)


def _moe_grouped(xs, tile_expert, tile_rows, w_gate, w_up, w_down, ys_prev, tile0, total_tiles):
    tiles = xs.shape[0] // MOE_TM
    F = w_gate.shape[2]
    nf = F // MOE_TF

    def fcol(i, f, nr):
        return jnp.where(nr[i] > 0, f, nf - 1)

    in_specs = [pl.BlockSpec((MOE_TM, D_MODEL), lambda i, f, te, nu: (i, 0)),
                pl.BlockSpec((None, D_MODEL, MOE_TF), lambda i, f, te, nu: (te[i], 0, fcol(i, f, nu))),
                pl.BlockSpec((None, D_MODEL, MOE_TF), lambda i, f, te, nu: (te[i], 0, fcol(i, f, nu))),
                pl.BlockSpec((None, MOE_TF, D_MODEL), lambda i, f, te, nu: (te[i], fcol(i, f, nu), 0))]
    args = [tile_expert, tile_rows, xs, w_gate, w_up, w_down]
    aliases = {}
    if ys_prev is not None:
        in_specs.append(pl.BlockSpec(memory_space=pl.ANY))
        aliases = {len(args): 0}
        args.append(ys_prev)
    grid_spec = pltpu.PrefetchScalarGridSpec(
        num_scalar_prefetch=2,
        grid=(tiles, nf),
        in_specs=in_specs,
        out_specs=pl.BlockSpec((MOE_TM, D_MODEL), lambda i, f, te, nu: (tile0 + i, 0)),
        scratch_shapes=[pltpu.VMEM((MOE_TM, D_MODEL), F32)],
    )
    return pl.pallas_call(
        _moe_kernel,
        out_shape=jax.ShapeDtypeStruct((total_tiles * MOE_TM, D_MODEL), BF16),
        grid_spec=grid_spec,
        input_output_aliases=aliases,
        compiler_params=pltpu.CompilerParams(dimension_semantics=("arbitrary", "arbitrary"),
                                             vmem_limit_bytes=MOE_VMEM_LIMIT),
        name="moe_grouped",
    )(*args)


def _combine_ln_kernel(x_ref, ya_ref, yb_ref, w_ref, g_ref, b_ref, o_ref, ob_ref):
    w = w_ref[...]
    f = w[:, 0:1] * ya_ref[...].astype(F32) + w[:, 1:2] * yb_ref[...].astype(F32)
    out = _layer_norm_rows(ALPHA * x_ref[...] + f, g_ref[...], b_ref[...])
    o_ref[...] = out
    ob_ref[...] = out.astype(BF16)


def _combine_ln(x, y_rows, top_w, gamma, beta):
    T = x.shape[0]
    tm = min(512, T)
    nt = T // tm
    wpad = jnp.pad(top_w, ((0, 0), (0, LANES - TOP_K)))
    fixed = lambda i: (0, 0)
    row = lambda i: (i, 0)
    return pl.pallas_call(
        _combine_ln_kernel,
        out_shape=(jax.ShapeDtypeStruct((T, D_MODEL), F32), jax.ShapeDtypeStruct((T, D_MODEL), BF16)),
        grid=(nt,),
        in_specs=[pl.BlockSpec((tm, D_MODEL), row),
                  pl.BlockSpec((tm, D_MODEL), row),
                  pl.BlockSpec((tm, D_MODEL), lambda i: (i + nt, 0)),
                  pl.BlockSpec((tm, LANES), row),
                  pl.BlockSpec((1, D_MODEL), fixed), pl.BlockSpec((1, D_MODEL), fixed)],
        out_specs=(pl.BlockSpec((tm, D_MODEL), row), pl.BlockSpec((tm, D_MODEL), row)),
        compiler_params=_cparams("parallel"),
        name="combine_ln",
    )(x, y_rows, y_rows, wpad, gamma.reshape(1, -1), beta.reshape(1, -1))


def _moe_ln(x, xb, logits, w_gate, w_up, w_down, gamma, beta):
    T = x.shape[0]
    top_val, top_idx = lax.top_k(logits, TOP_K)
    top_w = jax.nn.softmax(top_val, axis=-1)

    e_flat = top_idx.reshape(-1).astype(jnp.int32)
    onehot = (e_flat[:, None] == jnp.arange(N_EXPERTS, dtype=jnp.int32)[None, :]).astype(jnp.int32)
    rank = jnp.sum((jnp.cumsum(onehot, axis=0) - onehot) * onehot, axis=1)
    counts = jnp.sum(onehot, axis=0)
    padded = ((counts + MOE_TM - 1) // MOE_TM) * MOE_TM
    ends = jnp.cumsum(padded)
    starts = ends - padded
    dest = starts[e_flat] + rank

    Tp = TOP_K * T + N_EXPERTS * MOE_TM
    n_tiles = Tp // MOE_TM
    token = jnp.arange(TOP_K * T, dtype=jnp.int32) // TOP_K
    src = jnp.zeros((Tp,), jnp.int32).at[dest].set(token)
    tile_start = jnp.arange(n_tiles, dtype=jnp.int32) * MOE_TM
    tile_expert = jnp.sum((tile_start[:, None] >= ends[None, :]).astype(jnp.int32), axis=1)
    last_expert = jnp.sum((jnp.maximum(ends[-1] - 1, 0) >= ends).astype(jnp.int32))
    tile_expert = jnp.minimum(tile_expert, last_expert).astype(jnp.int32)
    filled_end = (starts + counts)[tile_expert]
    tile_rows = jnp.where(tile_start < ends[-1], jnp.clip(filled_end - tile_start, 0, MOE_TM), 0).astype(jnp.int32)

    runs = max(d for d in range(1, MOE_RUNS + 1) if n_tiles % d == 0)
    tiles = n_tiles // runs
    ys = None
    for c in range(runs):
        xs = xb.at[src[c * tiles * MOE_TM:(c + 1) * tiles * MOE_TM]].get(mode="promise_in_bounds")
        ys = _moe_grouped(xs, tile_expert[c * tiles:(c + 1) * tiles], tile_rows[c * tiles:(c + 1) * tiles],
                          w_gate, w_up, w_down, ys, c * tiles, n_tiles)
    slot_major = dest.reshape(T, TOP_K).T.reshape(-1)
    y_rows = ys.at[slot_major].get(mode="promise_in_bounds")
    return _combine_ln(x, y_rows, top_w, gamma, beta)


def _memattn_kernel(q_ref, k_ref, v_ref, o_ref):
    q = q_ref[0]
    outs = []
    for h in range(MEM_HEADS):
        sl = slice(h * MEM_HEAD_DIM, (h + 1) * MEM_HEAD_DIM)
        s = _dot_nt(q[:, sl].astype(BF16), k_ref[0, :, sl]) * (MEM_HEAD_DIM ** -0.5)
        e = jnp.exp(s - jnp.max(s, axis=-1, keepdims=True))
        p = e / jnp.sum(e, axis=-1, keepdims=True)
        outs.append(_dot(p.astype(BF16), v_ref[0, :, sl]))
    o_ref[0] = jnp.concatenate(outs, axis=-1).astype(o_ref.dtype)


def _memory_attention(p3, mem_kv):
    B, S, _ = p3.shape
    M = mem_kv.shape[1]
    ts = min(512, S)
    return pl.pallas_call(
        _memattn_kernel,
        out_shape=jax.ShapeDtypeStruct((B, S, MEM_WIDTH), BF16),
        grid=(B, S // ts),
        in_specs=[pl.BlockSpec((1, ts, MEM_WIDTH), lambda b, s: (b, s, Q_MEM_OFF // MEM_WIDTH)),
                  pl.BlockSpec((1, M, MEM_WIDTH), lambda b, s: (b, 0, 0)),
                  pl.BlockSpec((1, M, MEM_WIDTH), lambda b, s: (b, 0, 1))],
        out_specs=pl.BlockSpec((1, ts, MEM_WIDTH), lambda b, s: (b, s, 0)),
        compiler_params=_cparams("parallel", "parallel"),
        name="mem_attn",
    )(p3, mem_kv, mem_kv)


def _ret_kernel(q_ref, k_ref, v_ref, g_ref, cos_ref, sin_ref, intra_ref, qd_ref, kd_ref, cd_ref,
                gn_ref, o_ref, state_ref):
    @pl.when(pl.program_id(1) == 0)
    def _():
        state_ref[...] = jnp.zeros_like(state_ref)

    cosf = cos_ref[0]
    sinf = sin_ref[0]
    half = RET_DK // 2
    rot = lambda x: x * cosf + pltpu.roll(x, half, 1) * sinf
    hs = range(RET_HEADS)
    ksl = [slice(h * RET_DK, (h + 1) * RET_DK) for h in hs]
    vsl = [slice(h * RET_DV, (h + 1) * RET_DV) for h in hs]

    qr = [rot(q_ref[0, :, ksl[h]].astype(F32)) for h in hs]
    kr = [rot(k_ref[0, :, ksl[h]].astype(F32)) * (RET_DK ** -0.5) for h in hs]
    vb = [v_ref[0, :, vsl[h]].astype(BF16) for h in hs]
    state = [state_ref[h] for h in hs]
    s = [_dot_nt(qr[h].astype(BF16), kr[h].astype(BF16)) * intra_ref[h] for h in hs]
    o = [_dot(s[h].astype(BF16), vb[h]) + _dot((qr[h] * qd_ref[h]).astype(BF16), state[h].astype(BF16))
         for h in hs]
    for h in hs:
        state_ref[h] = state[h] * cd_ref[h] + _dot_tn((kr[h] * kd_ref[h]).astype(BF16), vb[h])
    for h in hs:
        mu = jnp.mean(o[h], axis=-1, keepdims=True)
        oc = o[h] - mu
        var = jnp.mean(oc * oc, axis=-1, keepdims=True)
        y = oc * lax.rsqrt(var + LN_EPS) * gn_ref[0:1, vsl[h]] + gn_ref[1:2, vsl[h]]
        g = g_ref[0, :, vsl[h]].astype(F32)
        o_ref[0, :, vsl[h]] = (y * (g * jax.nn.sigmoid(g))).astype(o_ref.dtype)


def _retention_mixer(p3, positions, gn):
    B, S, _ = p3.shape
    C = RET_CHUNK
    H = RET_HEADS
    inv = 1.0 / (ROPE_BASE ** (jnp.arange(0, RET_DK, 2, dtype=F32) / RET_DK))
    ang = positions.astype(F32)[..., None] * inv
    cos, sin = jnp.cos(ang), jnp.sin(ang)
    cosf = jnp.concatenate([cos, cos], axis=-1)
    sinf = jnp.concatenate([-sin, sin], axis=-1)

    log_g = jnp.log1p(-jnp.exp2(-5.0 - jnp.arange(H, dtype=F32)))
    idx = jnp.arange(C, dtype=F32)
    diff = idx[:, None] - idx[None, :]
    intra = jnp.where(diff >= 0, jnp.exp(jnp.maximum(diff, 0.0)[None] * log_g[:, None, None]), 0.0)
    qd = jnp.broadcast_to(jnp.exp((idx + 1.0)[None, :, None] * log_g[:, None, None]), (H, C, RET_DK))
    kd = jnp.broadcast_to(jnp.exp((C - 1.0 - idx)[None, :, None] * log_g[:, None, None]), (H, C, RET_DK))
    cd = jnp.broadcast_to(jnp.exp(C * log_g)[:, None, None], (H, 1, RET_DV))

    QK = H * RET_DK
    assert 2 * QK == SEQ_WIDTH
    whole = lambda b, c: (0, 0, 0)
    return pl.pallas_call(
        _ret_kernel,
        out_shape=jax.ShapeDtypeStruct((B, S, SEQ_WIDTH), BF16),
        grid=(B, S // C),
        in_specs=[pl.BlockSpec((1, C, QK), lambda b, c: (b, c, 0)),
                  pl.BlockSpec((1, C, QK), lambda b, c: (b, c, 1)),
                  pl.BlockSpec((1, C, SEQ_WIDTH), lambda b, c: (b, c, 1)),
                  pl.BlockSpec((1, C, SEQ_WIDTH), lambda b, c: (b, c, 2)),
                  pl.BlockSpec((1, C, RET_DK), lambda b, c: (b, c, 0)),
                  pl.BlockSpec((1, C, RET_DK), lambda b, c: (b, c, 0)),
                  pl.BlockSpec((H, C, C), whole),
                  pl.BlockSpec((H, C, RET_DK), whole),
                  pl.BlockSpec((H, C, RET_DK), whole),
                  pl.BlockSpec((H, 1, RET_DV), whole),
                  pl.BlockSpec((2, SEQ_WIDTH), lambda b, c: (0, 0))],
        out_specs=pl.BlockSpec((1, C, SEQ_WIDTH), lambda b, c: (b, c, 0)),
        scratch_shapes=[pltpu.VMEM((H, RET_DK, RET_DV), F32)],
        compiler_params=_cparams("parallel", "arbitrary"),
        name="retention",
    )(p3, p3, p3, p3, cosf, sinf, intra, qd, kd, cd, gn)


RWKV_PAIRS = 12
RWKV_SUB = 2
PAIR = 2 * RWKV_HEAD
PREV_ROWS = 16

_NN = ((1,), (0,))
_NT = ((1,), (1,))
_TN = ((0,), (0,))


def _dg(a, b, dims):
    return lax.dot_general(a.astype(BF16), b.astype(BF16), (dims, ((), ())), preferred_element_type=F32)


def _rwkv_masks():
    L = RWKV_CHUNK
    row = np.arange(L)[:, None]
    lane = np.arange(PAIR)[None, :]
    m1 = np.broadcast_to(lane < RWKV_HEAD, (L, PAIR))
    small = np.stack([m1, ~m1, row > lane % L, row >= lane % L, row == lane % L]).astype(np.float32)
    same_head = (np.arange(2 * L)[:, None] // L == lane // L).astype(np.float32)
    tri = np.kron(np.eye(RWKV_SUB), row >= np.arange(L)[None, :]).astype(np.float32)
    return jnp.asarray(small), jnp.asarray(same_head, dtype=BF16), jnp.asarray(tri, dtype=BF16)


def _stacker(m1, m2):
    m1b, m2b = m1.astype(BF16), m2.astype(BF16)

    def stack(x):
        xb = x.astype(BF16)
        return jnp.concatenate([xb * m1b, xb * m2b], axis=0)

    return stack


def _rwkv_prepare(r, c, lw, k, v, kk, a, masks):
    n = len(r)
    rng = range(n)
    L = r[0].shape[0]
    assert L == RWKV_HEAD
    m1, m2, strict, incl, eye = masks
    stack = _stacker(m1, m2)
    blockdiag = stack
    strict_b = strict.astype(BF16)
    incl_b = jnp.concatenate([incl, incl], axis=1).astype(BF16)

    p_incl = [jnp.exp(c[j]) for j in rng]
    p_inv = [jnp.exp(-c[j]) for j in rng]
    at = [-kk[j] * jnp.exp(c[j] - lw[j]) for j in rng]
    rt = [r[j] * p_incl[j] for j in rng]
    lhs = [jnp.concatenate([at[j], rt[j]], axis=0).astype(BF16) for j in rng]
    rhs = [jnp.concatenate([stack(kk[j] * a[j] * p_inv[j]), stack(k[j] * p_inv[j])], axis=0)
           for j in rng]
    sc = [_dg(lhs[j], rhs[j], _NT) for j in rng]
    nmat = [sc[j][:L, :PAIR] * strict for j in rng]
    tinv = [eye + nmat[j] for j in rng]
    pw = [_dg(nmat[j], blockdiag(nmat[j]), _NN) for j in rng]
    rounds = int(np.log2(L)) - 1
    for i in range(rounds):
        if i < rounds - 1:
            both = [_dg(jnp.concatenate([pw[j], tinv[j]], axis=0), blockdiag(pw[j]), _NN) for j in rng]
            pw = [both[j][:L] for j in rng]
            tinv = [tinv[j] + both[j][L:] for j in rng]
        else:
            tinv = [tinv[j] + _dg(tinv[j], blockdiag(pw[j]), _NN) for j in rng]

    vs = [stack(v[j]) for j in rng]
    mv = [_dg(sc[j][:L, PAIR:].astype(BF16) * strict_b, vs[j], _NN) for j in rng]
    a_r = [sc[j][L:, :].astype(BF16) * incl_b for j in rng]
    return dict(lhs=lhs, rhs=rhs, vs=vs, mv=mv, a_r=a_r, tinv=[tinv[j].astype(BF16) for j in rng],
                decay=[p_incl[j][L - 1:L, :] for j in rng])


def _rwkv_apply(prep, entries, state, masks):
    m1, m2 = masks[0], masks[1]
    stack = _stacker(m1, m2)
    L = m1.shape[0]
    rng = range(len(entries))
    get = lambda name: [prep[name][e] for e in entries]
    lhs, rhs, vs, mv, a_r, tinv, decay = (get(n) for n in ("lhs", "rhs", "vs", "mv", "a_r", "tinv", "decay"))
    sg = [_dg(lhs[j], state[j], _NT) for j in rng]
    u = [_dg(tinv[j], stack(sg[j][:L] + mv[j]), _NN) for j in rng]
    uv = [jnp.concatenate([stack(u[j]), vs[j]], axis=0) for j in rng]
    ys = [_dg(a_r[j], uv[j], _NN) for j in rng]
    y = [sg[j][L:] + ys[j] for j in rng]
    new_state = [(state[j] + _dg(uv[j], rhs[j], _TN)) * decay[j] for j in rng]
    return y, new_state


def _shift_mix(x, prev_block, mu, first):
    prev_row = jnp.where(first, 0.0, prev_block[PREV_ROWS - 1:PREV_ROWS, :])
    rows = lax.broadcasted_iota(jnp.int32, x.shape, 0)
    prev = jnp.where(rows == 0, prev_row, pltpu.roll(x, 1, 0))
    return x + mu * (prev - x)


def _rwkv_mixer_kernel(*refs, with_v):
    it = iter(refs)
    r_ref, k_ref, v_ref, xg_ref, xl_ref = (next(it) for _ in range(5))
    rp_ref, kp_ref, vp_ref, xgp_ref, xlp_ref = (next(it) for _ in range(5))
    mu3_ref, mug_ref, mul_ref, vec_ref = (next(it) for _ in range(4))
    small_ref, ones_ref, tri_ref = (next(it) for _ in range(3))
    ww_ref, wa_ref = next(it), next(it)
    wv_ref = next(it) if with_v else None
    wg_ref = next(it)
    vf_ref = next(it) if with_v else None
    y_ref = next(it)
    vfo_ref = None if with_v else next(it)
    state_ref = next(it)

    first = pl.program_id(2) == 0

    @pl.when(first)
    def _():
        state_ref[...] = jnp.zeros_like(state_ref)

    L = RWKV_CHUNK
    mix = lambda x_ref, p_ref, mu: _shift_mix(x_ref[0].astype(F32), p_ref[0].astype(F32), mu, first)
    r = mix(r_ref, rp_ref, mu3_ref[0:1, :])
    k = mix(k_ref, kp_ref, mu3_ref[1:2, :])
    v = mix(v_ref, vp_ref, mu3_ref[2:3, :])
    xg = mix(xg_ref, xgp_ref, mug_ref[...])
    xl = mix(xl_ref, xlp_ref, mul_ref[...])

    w0, a0, v0 = vec_ref[0:1, :], vec_ref[1:2, :], vec_ref[2:3, :]
    k_k, k_a, r_k = vec_ref[3:4, :], vec_ref[4:5, :], vec_ref[5:6, :]
    gn_g, gn_b = vec_ref[6:7, :], vec_ref[7:8, :]

    lane = lax.broadcasted_iota(jnp.int32, xl.shape, 1)
    lin = jnp.where(lane < LORA_W, jnp.tanh(xl), xl).astype(BF16)
    w_arg = w0 + _dot(lin, ww_ref[...])
    a = jax.nn.sigmoid(a0 + _dot(lin, wa_ref[...]))
    if with_v:
        v = v + (vf_ref[0] - v) * jax.nn.sigmoid(v0 + _dot(lin, wv_ref[...]))
    else:
        vfo_ref[0] = v
    g = _dot(jax.nn.sigmoid(xg).astype(BF16), wg_ref[...])

    w_log = -(jnp.maximum(-w_arg, 0.0) + jnp.log1p(jnp.exp(-jnp.abs(w_arg)))) - 0.5
    lw = -jnp.exp(w_log)
    tri = tri_ref[...]
    h1 = lw.astype(BF16)
    h2 = (lw - h1.astype(F32)).astype(BF16)
    c = _dot(tri, h1) + _dot(tri, h2)

    kk_raw = k * k_k
    k2 = k * (1.0 + (a - 1.0) * k_a)
    ones_bd = ones_ref[...]
    masks = tuple(small_ref[i] for i in range(5))
    inv_n = 1.0 / RWKV_HEAD

    pairs = range(RWKV_PAIRS)
    cols = [slice(j * PAIR, (j + 1) * PAIR) for j in pairs]
    rows = [slice(s * L, (s + 1) * L) for s in range(RWKV_SUB)]
    where = [(rs, cs) for rs in rows for cs in cols]
    ent = range(len(where))
    part = lambda x: [x[rs, cs] for rs, cs in where]
    kkr, r_e, k_e, v_e, g_e = part(kk_raw), part(r), part(k2), part(v), part(g)
    red = [_dot(jnp.concatenate([kkr[e] * kkr[e], r_e[e] * k_e[e] * r_k[:, where[e][1]]], axis=0).astype(BF16),
                ones_bd) for e in ent]
    kks = [kkr[e] / jnp.maximum(jnp.sqrt(red[e][:L]), 1e-12) for e in ent]
    prep = _rwkv_prepare(r_e, part(c), part(lw), k_e, v_e, kks, part(a), masks)
    state = [state_ref[j] for j in pairs]
    ys = []
    for s in range(RWKV_SUB):
        y_s, state = _rwkv_apply(prep, [s * RWKV_PAIRS + j for j in pairs], state, masks)
        ys += y_s
    for j in pairs:
        state_ref[j] = state[j]
    yc = [ys[e] - _dot(ys[e].astype(BF16), ones_bd) * inv_n for e in ent]
    var = [_dot((yc[e] * yc[e]).astype(BF16), ones_bd) * inv_n for e in ent]
    for e in ent:
        rs, cs = where[e]
        yn = yc[e] * lax.rsqrt(var[e] + RWKV_GN_EPS) * gn_g[:, cs] + gn_b[:, cs]
        y_ref[0, rs, cs] = ((yn + red[e][L:] * v_e[e]) * g_e[e]).astype(y_ref.dtype)


def _rwkv_layout(w_in, mu, with_v):
    sizes = [SEQ_WIDTH] * 3 + [LORA_W, LORA_A] + ([LORA_V] if with_v else []) + [LORA_G]
    offs = np.concatenate([[0], np.cumsum(sizes)])
    seq_cols = int(offs[-1])
    g_idx = len(sizes) - 1
    pad = LORA_PACK - LORA_W - LORA_A - (LORA_V if with_v else 0)
    cols = [w_in[:, :3 * SEQ_WIDTH], w_in[:, seq_cols:], w_in[:, offs[g_idx]:offs[g_idx + 1]],
            w_in[:, offs[3]:offs[g_idx]], jnp.zeros((w_in.shape[0], pad), w_in.dtype)]
    mus = [mu[:3 * SEQ_WIDTH], jnp.zeros((MEM_WIDTH,), mu.dtype), mu[offs[g_idx]:offs[g_idx + 1]],
           mu[offs[3]:offs[g_idx]], jnp.zeros((pad,), mu.dtype)]
    return jnp.concatenate(cols, axis=1), jnp.concatenate(mus)


def _rwkv_mixer(p3, mu, lora_w, lora_a, lora_g, lora_v, vecs, v_first):
    B, S, _ = p3.shape
    L = RWKV_SUB * RWKV_CHUNK
    W = RWKV_PAIRS * PAIR
    above = lambda t: jnp.maximum(t * (L // PREV_ROWS) - 1, 0)
    with_v = lora_v is not None
    if with_v:
        w0, a0, v0, k_k, k_a, r_k, gn_g, gn_b = vecs
    else:
        w0, a0, k_k, k_a, r_k, gn_g, gn_b = vecs
        v0 = jnp.zeros_like(w0)
    vec8 = jnp.stack([w0, a0, v0, k_k, k_a, r_k, gn_g, gn_b])
    mu3 = jnp.concatenate([mu[:3 * SEQ_WIDTH].reshape(3, SEQ_WIDTH), jnp.zeros((5, SEQ_WIDTH), F32)])
    g_off = Q_MEM_OFF + MEM_WIDTH
    l_off = g_off + LORA_G
    mug = mu[g_off:g_off + LORA_G].reshape(1, LORA_G)
    mul = mu[l_off:l_off + LORA_PACK].reshape(1, LORA_PACK)
    rows = lambda w, lo: jnp.zeros((LORA_PACK, SEQ_WIDTH), F32).at[lo:lo + w.shape[0]].set(w).astype(BF16)

    def cols(width, off):
        blk = off // width
        return pl.BlockSpec((1, L, width), lambda b, g, t: (b, t, blk))

    def cols_prev(width, off):
        blk = off // width
        return pl.BlockSpec((1, PREV_ROWS, width), lambda b, g, t: (b, above(t), blk))

    def seq(off):
        blk = off // W
        return pl.BlockSpec((1, L, W), lambda b, g, t: (b, t, blk + g))

    def seq_prev(off):
        blk = off // W
        return pl.BlockSpec((1, PREV_ROWS, W), lambda b, g, t: (b, above(t), blk + g))

    per_group = lambda nrows: pl.BlockSpec((nrows, W), lambda b, g, t: (0, g))
    fixed = lambda ncols: pl.BlockSpec((1, ncols), lambda b, g, t: (0, 0))

    in_specs = [seq(0), seq(SEQ_WIDTH), seq(2 * SEQ_WIDTH), cols(LORA_G, g_off), cols(LORA_PACK, l_off),
                seq_prev(0), seq_prev(SEQ_WIDTH), seq_prev(2 * SEQ_WIDTH),
                cols_prev(LORA_G, g_off), cols_prev(LORA_PACK, l_off),
                per_group(8), fixed(LORA_G), fixed(LORA_PACK), per_group(8)]
    masks = _rwkv_masks()
    in_specs += [pl.BlockSpec(m.shape, lambda b, g, t, nd=m.ndim: (0,) * nd) for m in masks]
    in_specs += [per_group(LORA_PACK), per_group(LORA_PACK)]
    args = [p3] * 10 + [mu3, mug, mul, vec8, *masks, rows(lora_w, 0), rows(lora_a, LORA_W)]
    if with_v:
        in_specs.append(per_group(LORA_PACK))
        args.append(rows(lora_v, LORA_W + LORA_A))
    in_specs.append(per_group(LORA_G))
    args.append(lora_g.astype(BF16))
    out_spec = pl.BlockSpec((1, L, W), lambda b, g, t: (b, t, g))
    y_shape = jax.ShapeDtypeStruct((B, S, SEQ_WIDTH), BF16)
    if with_v:
        in_specs.append(out_spec)
        args.append(v_first)
        out_shape, out_specs = y_shape, out_spec
    else:
        out_shape = (y_shape, jax.ShapeDtypeStruct((B, S, SEQ_WIDTH), F32))
        out_specs = (out_spec, out_spec)

    res = pl.pallas_call(
        functools.partial(_rwkv_mixer_kernel, with_v=with_v),
        out_shape=out_shape,
        grid=(B, SEQ_WIDTH // W, S // L),
        in_specs=in_specs,
        out_specs=out_specs,
        scratch_shapes=[pltpu.VMEM((RWKV_PAIRS, PAIR, PAIR), F32)],
        compiler_params=_cparams("parallel", "parallel", "arbitrary"),
        name="rwkv_mixer",
    )(*args)
    return (res, v_first) if with_v else res


def kernel(x, mem, positions, mem_w_kv, l0_w_in, l0_mu, l0_lora_w, l0_lora_a, l0_lora_g, l0_vecs, l0_w_out, l0_norms, l0_ffn_gate, l0_ffn_up, l0_ffn_down, l1_w_in, l1_gn, l1_w_out, l1_norms, l1_router, l1_moe_gate, l1_moe_up, l1_moe_down, l2_w_in, l2_mu, l2_lora_w, l2_lora_a, l2_lora_v, l2_lora_g, l2_vecs, l2_w_out, l2_norms, l2_ffn_gate, l2_ffn_up, l2_ffn_down, l3_w_in, l3_gn, l3_w_out, l3_norms, l3_router, l3_moe_gate, l3_moe_up, l3_moe_down):
    B, S, D = x.shape
    T = B * S
    M = mem.shape[1]
    mem_kv = _matmul(mem.reshape(B * M, D), mem_w_kv.astype(BF16), tm=1024, tn=512,
                     out_dtype=BF16).reshape(B, M, 2 * MEM_WIDTH)

    rwkv_layers = {
        0: (l0_w_in, l0_mu, l0_lora_w, l0_lora_a, l0_lora_g, None, l0_vecs),
        2: (l2_w_in, l2_mu, l2_lora_w, l2_lora_a, l2_lora_g, l2_lora_v, l2_vecs),
    }
    ret_layers = {1: (l1_w_in, l1_gn), 3: (l3_w_in, l3_gn)}
    w_outs = (l0_w_out, l1_w_out, l2_w_out, l3_w_out)
    norms = (l0_norms, l1_norms, l2_norms, l3_norms)
    dense = {0: (l0_ffn_gate, l0_ffn_up, l0_ffn_down), 2: (l2_ffn_gate, l2_ffn_up, l2_ffn_down)}
    moe = {1: (l1_router, l1_moe_gate, l1_moe_up, l1_moe_down),
           3: (l3_router, l3_moe_gate, l3_moe_up, l3_moe_down)}

    xf = x.reshape(T, D)
    xin = xf
    v_first = None
    for i in range(DEPTH):
        if i in rwkv_layers:
            w_in, mu, lw_, la_, lg_, lv_, vecs = rwkv_layers[i]
            w_in, mu = _rwkv_layout(w_in, mu, lv_ is not None)
            p3 = _matmul(xin, w_in.astype(BF16), tm=512, tn=w_in.shape[1], out_dtype=BF16).reshape(B, S, -1)
            y_seq, v_first = _rwkv_mixer(p3, mu, lw_, la_, lg_, lv_, vecs, v_first)
        else:
            w_in, gn = ret_layers[i]
            p3 = _matmul(xin, w_in.astype(BF16), tm=512, tn=w_in.shape[1], out_dtype=BF16).reshape(B, S, -1)
            y_seq = _retention_mixer(p3, positions, gn)
        y_mem = _memory_attention(p3, mem_kv)
        nrm = norms[i]
        proj = _proj_ln(y_seq.reshape(T, SEQ_WIDTH), y_mem.reshape(T, MEM_WIDTH), w_outs[i], xf,
                        nrm[0], nrm[1], None if i in dense else moe[i][0])
        if i in dense:
            xf, xb = _ffn_ln(*proj, *dense[i], nrm[2], nrm[3])
        else:
            xf, xb = _moe_ln(*proj, *moe[i][1:], nrm[2], nrm[3])
        xin = xb
    return xf.reshape(B, S, D)
```

```python
import functools

import numpy as np
import jax
import jax.numpy as jnp
from jax import lax
from jax.experimental import pallas as pl
from jax.experimental.pallas import tpu as pltpu

F32 = jnp.float32
BF16 = jnp.bfloat16

D_MODEL = 2048
DEPTH = 4
MEM_HEADS = 4
MEM_HEAD_DIM = 128
MEM_WIDTH = MEM_HEADS * MEM_HEAD_DIM
SEQ_WIDTH = D_MODEL - MEM_WIDTH

RWKV_HEAD = 64
RWKV_HEADS = SEQ_WIDTH // RWKV_HEAD
LORA_W = 96
LORA_A = 96
LORA_V = 64
LORA_G = 256
LORA_PACK = 256
RWKV_GN_EPS = 1e-5 * RWKV_HEAD
RWKV_CHUNK = 64

RET_HEADS = 6
RET_DV = SEQ_WIDTH // RET_HEADS
RET_DK = RET_DV // 2
RET_CHUNK = 128
ROPE_BASE = 10000.0

N_EXPERTS = 8
TOP_K = 2

ALPHA = (2.0 * DEPTH) ** 0.25
LN_EPS = 1e-5

LANES = 128
Q_MEM_OFF = 3 * SEQ_WIDTH
VMEM_LIMIT = 52 * 1024 * 1024
MOE_VMEM_LIMIT = 58 * 1024 * 1024


def _cparams(*semantics):
    return pltpu.CompilerParams(dimension_semantics=semantics, vmem_limit_bytes=VMEM_LIMIT)


def _dot(a, b):
    return jnp.dot(a, b, preferred_element_type=F32)


def _dot_nt(a, b):
    return lax.dot_general(a, b, (((1,), (1,)), ((), ())), preferred_element_type=F32)


def _dot_tn(a, b):
    return lax.dot_general(a, b, (((0,), (0,)), ((), ())), preferred_element_type=F32)


def _layer_norm_rows(z, g, b):
    mu = jnp.mean(z, axis=-1, keepdims=True)
    zc = z - mu
    var = jnp.mean(zc * zc, axis=-1, keepdims=True)
    return zc * lax.rsqrt(var + LN_EPS) * g + b


def _mm_kernel(a_ref, w_ref, o_ref, abf_ref):
    @pl.when(pl.program_id(1) == 0)
    def _():
        abf_ref[...] = a_ref[...].astype(BF16)

    o_ref[...] = _dot(abf_ref[...], w_ref[...]).astype(o_ref.dtype)


def _matmul(a, w, *, tm, tn, out_dtype=F32):
    M, K = a.shape
    N = w.shape[1]
    tm = min(tm, M)
    assert M % tm == 0 and N % tn == 0
    w_mode = pl.Buffered(1) if tn == N else None
    return pl.pallas_call(
        _mm_kernel,
        out_shape=jax.ShapeDtypeStruct((M, N), out_dtype),
        grid=(M // tm, N // tn),
        in_specs=[pl.BlockSpec((tm, K), lambda i, j: (i, 0)),
                  pl.BlockSpec((K, tn), lambda i, j: (0, j), pipeline_mode=w_mode)],
        out_specs=pl.BlockSpec((tm, tn), lambda i, j: (i, j)),
        scratch_shapes=[pltpu.VMEM((tm, K), BF16)],
        compiler_params=_cparams("parallel", "arbitrary"),
        name="matmul",
    )(a, w)


def _proj_ln_kernel(*refs, with_router):
    ys_ref, ym_ref, ws_ref, wm_ref, x_ref, g_ref, b_ref = refs[:7]
    if with_router:
        wrh_ref, wrl_ref, o_ref, ob_ref, logits_ref = refs[7:]
    else:
        o_ref, ob_ref = refs[7:]
    half = x_ref.shape[0] // 2
    for s in range(2):
        rs = slice(s * half, (s + 1) * half)
        h = _dot(ys_ref[rs, :], ws_ref[...]) + _dot(ym_ref[rs, :], wm_ref[...])
        out = _layer_norm_rows(ALPHA * x_ref[rs, :] + h, g_ref[...], b_ref[...])
        hi = out.astype(BF16)
        o_ref[rs, :] = out
        ob_ref[rs, :] = hi
        if with_router:
            lo = (out - hi.astype(F32)).astype(BF16)
            logits_ref[rs, :] = _dot(hi, wrh_ref[...]) + (_dot(lo, wrh_ref[...]) + _dot(hi, wrl_ref[...]))


def _proj_ln(y_seq, y_mem, w_out, x, gamma, beta, w_router=None):
    T = x.shape[0]
    tm = min(512, T)
    ws = w_out[:SEQ_WIDTH].astype(BF16)
    wm = w_out[SEQ_WIDTH:].astype(BF16)
    row = lambda i: (i, 0)
    fixed = lambda i: (0, 0)
    once = pl.Buffered(1)
    in_specs = [pl.BlockSpec((tm, SEQ_WIDTH), row), pl.BlockSpec((tm, MEM_WIDTH), row),
                pl.BlockSpec((SEQ_WIDTH, D_MODEL), fixed, pipeline_mode=once),
                pl.BlockSpec((MEM_WIDTH, D_MODEL), fixed, pipeline_mode=once),
                pl.BlockSpec((tm, D_MODEL), row),
                pl.BlockSpec((1, D_MODEL), fixed), pl.BlockSpec((1, D_MODEL), fixed)]
    args = [y_seq, y_mem, ws, wm, x, gamma.reshape(1, -1), beta.reshape(1, -1)]
    out_shape = [jax.ShapeDtypeStruct((T, D_MODEL), F32), jax.ShapeDtypeStruct((T, D_MODEL), BF16)]
    out_specs = [pl.BlockSpec((tm, D_MODEL), row), pl.BlockSpec((tm, D_MODEL), row)]
    if w_router is not None:
        wr = jnp.pad(w_router, ((0, 0), (0, LANES - N_EXPERTS)))
        wr_hi = wr.astype(BF16)
        wr_lo = (wr - wr_hi.astype(F32)).astype(BF16)
        in_specs += [pl.BlockSpec((D_MODEL, LANES), fixed, pipeline_mode=once)] * 2
        args += [wr_hi, wr_lo]
        out_shape.append(jax.ShapeDtypeStruct((T, LANES), F32))
        out_specs.append(pl.BlockSpec((tm, LANES), row))
    res = pl.pallas_call(
        functools.partial(_proj_ln_kernel, with_router=w_router is not None),
        out_shape=tuple(out_shape),
        grid=(T // tm,),
        in_specs=in_specs,
        out_specs=tuple(out_specs),
        compiler_params=_cparams("parallel"),
        name="proj_ln",
    )(*args)
    if w_router is None:
        return res
    return res[0], res[1], res[2][:, :N_EXPERTS]


def _swiglu_partial(xb, wg, wu, wd, splits=1):
    w = wg.shape[1] // splits
    out = None
    for s in range(splits):
        cs = slice(s * w, (s + 1) * w)
        gate = _dot(xb, wg[:, cs])
        up = _dot(xb, wu[:, cs])
        h = (gate * jax.nn.sigmoid(gate) * up).astype(BF16)
        d = _dot(h, wd[cs, :])
        out = d if out is None else out + d
    return out


def _ffn_ln_kernel(xb_ref, wg_ref, wu_ref, wd_ref, x_ref, g_ref, b_ref, o_ref, ob_ref, acc_ref):
    f = pl.program_id(1)

    @pl.when(f == 0)
    def _():
        acc_ref[...] = jnp.zeros_like(acc_ref)

    acc_ref[...] += _swiglu_partial(xb_ref[...], wg_ref[...], wu_ref[...], wd_ref[...], splits=2)

    @pl.when(f == pl.num_programs(1) - 1)
    def _():
        out = _layer_norm_rows(ALPHA * x_ref[...] + acc_ref[...], g_ref[...], b_ref[...])
        o_ref[...] = out
        ob_ref[...] = out.astype(BF16)


def _ffn_ln(x, xb, w_gate, w_up, w_down, gamma, beta):
    T = x.shape[0]
    F = w_gate.shape[1]
    tm = min(512, T)
    tf = 512
    row = lambda i, f: (i, 0)
    fixed = lambda i, f: (0, 0)
    return pl.pallas_call(
        _ffn_ln_kernel,
        out_shape=(jax.ShapeDtypeStruct((T, D_MODEL), F32), jax.ShapeDtypeStruct((T, D_MODEL), BF16)),
        grid=(T // tm, F // tf),
        in_specs=[pl.BlockSpec((tm, D_MODEL), row),
                  pl.BlockSpec((D_MODEL, tf), lambda i, f: (0, f)),
                  pl.BlockSpec((D_MODEL, tf), lambda i, f: (0, f)),
                  pl.BlockSpec((tf, D_MODEL), lambda i, f: (f, 0)),
                  pl.BlockSpec((tm, D_MODEL), row),
                  pl.BlockSpec((1, D_MODEL), fixed), pl.BlockSpec((1, D_MODEL), fixed)],
        out_specs=(pl.BlockSpec((tm, D_MODEL), row), pl.BlockSpec((tm, D_MODEL), row)),
        scratch_shapes=[pltpu.VMEM((tm, D_MODEL), F32)],
        compiler_params=_cparams("parallel", "arbitrary"),
        name="ffn_ln",
    )(xb, w_gate.astype(BF16), w_up.astype(BF16), w_down.astype(BF16), x,
      gamma.reshape(1, -1), beta.reshape(1, -1))


MOE_TM = 1024
MOE_TF = 512
MOE_RUNS = 4


def _moe_kernel(te_ref, nr_ref, xs_ref, wg_ref, wu_ref, wd_ref, *rest):
    o_ref, acc_ref = rest[-2:]
    i = pl.program_id(0)
    f = pl.program_id(1)
    last = pl.num_programs(1) - 1
    rows = nr_ref[i]
    half = MOE_TM // 2
    used = rows > 0

    @pl.when(jnp.logical_and(used, f == 0))
    def _():
        acc_ref[...] = jnp.zeros_like(acc_ref)

    @pl.when(rows > half)
    def _():
        acc_ref[...] += _swiglu_partial(xs_ref[...], wg_ref[...].astype(BF16), wu_ref[...].astype(BF16),
                                        wd_ref[...].astype(BF16))

    @pl.when(jnp.logical_and(used, rows <= half))
    def _():
        acc_ref[:half, :] += _swiglu_partial(xs_ref[:half, :], wg_ref[...].astype(BF16),
                                             wu_ref[...].astype(BF16), wd_ref[...].astype(BF16))

    @pl.when(jnp.logical_and(used, f == last))
    def _():
        o_ref[...] = acc_ref[...].astype(o_ref.dtype)

    @pl.when(jnp.logical_and(jnp.logical_not(used), f == last))
    def _():
        o_ref[...] = jnp.zeros_like(o_ref)


def _moe_grouped(xs, tile_expert, tile_rows, w_gate, w_up, w_down, ys_prev, tile0, total_tiles):
    tiles = xs.shape[0] // MOE_TM
    F = w_gate.shape[2]
    nf = F // MOE_TF

    def fcol(i, f, nr):
        return jnp.where(nr[i] > 0, f, nf - 1)

    in_specs = [pl.BlockSpec((MOE_TM, D_MODEL), lambda i, f, te, nu: (i, 0)),
                pl.BlockSpec((None, D_MODEL, MOE_TF), lambda i, f, te, nu: (te[i], 0, fcol(i, f, nu))),
                pl.BlockSpec((None, D_MODEL, MOE_TF), lambda i, f, te, nu: (te[i], 0, fcol(i, f, nu))),
                pl.BlockSpec((None, MOE_TF, D_MODEL), lambda i, f, te, nu: (te[i], fcol(i, f, nu), 0))]
    args = [tile_expert, tile_rows, xs, w_gate, w_up, w_down]
    aliases = {}
    if ys_prev is not None:
        in_specs.append(pl.BlockSpec(memory_space=pl.ANY))
        aliases = {len(args): 0}
        args.append(ys_prev)
    grid_spec = pltpu.PrefetchScalarGridSpec(
        num_scalar_prefetch=2,
        grid=(tiles, nf),
        in_specs=in_specs,
        out_specs=pl.BlockSpec((MOE_TM, D_MODEL), lambda i, f, te, nu: (tile0 + i, 0)),
        scratch_shapes=[pltpu.VMEM((MOE_TM, D_MODEL), F32)],
    )
    return pl.pallas_call(
        _moe_kernel,
        out_shape=jax.ShapeDtypeStruct((total_tiles * MOE_TM, D_MODEL), BF16),
        grid_spec=grid_spec,
        input_output_aliases=aliases,
        compiler_params=pltpu.CompilerParams(dimension_semantics=("arbitrary", "arbitrary"),
                                             vmem_limit_bytes=MOE_VMEM_LIMIT),
        name="moe_grouped",
    )(*args)


def _combine_ln_kernel(x_ref, ya_ref, yb_ref, w_ref, g_ref, b_ref, o_ref, ob_ref):
    w = w_ref[...]
    f = w[:, 0:1] * ya_ref[...].astype(F32) + w[:, 1:2] * yb_ref[...].astype(F32)
    out = _layer_norm_rows(ALPHA * x_ref[...] + f, g_ref[...], b_ref[...])
    o_ref[...] = out
    ob_ref[...] = out.astype(BF16)


def _combine_ln(x, y_rows, top_w, gamma, beta):
    T = x.shape[0]
    tm = min(512, T)
    nt = T // tm
    wpad = jnp.pad(top_w, ((0, 0), (0, LANES - TOP_K)))
    fixed = lambda i: (0, 0)
    row = lambda i: (i, 0)
    return pl.pallas_call(
        _combine_ln_kernel,
        out_shape=(jax.ShapeDtypeStruct((T, D_MODEL), F32), jax.ShapeDtypeStruct((T, D_MODEL), BF16)),
        grid=(nt,),
        in_specs=[pl.BlockSpec((tm, D_MODEL), row),
                  pl.BlockSpec((tm, D_MODEL), row),
                  pl.BlockSpec((tm, D_MODEL), lambda i: (i + nt, 0)),
                  pl.BlockSpec((tm, LANES), row),
                  pl.BlockSpec((1, D_MODEL), fixed), pl.BlockSpec((1, D_MODEL), fixed)],
        out_specs=(pl.BlockSpec((tm, D_MODEL), row), pl.BlockSpec((tm, D_MODEL), row)),
        compiler_params=_cparams("parallel"),
        name="combine_ln",
    )(x, y_rows, y_rows, wpad, gamma.reshape(1, -1), beta.reshape(1, -1))


def _moe_ln(x, xb, logits, w_gate, w_up, w_down, gamma, beta):
    T = x.shape[0]
    top_val, top_idx = lax.top_k(logits, TOP_K)
    top_w = jax.nn.softmax(top_val, axis=-1)

    e_flat = top_idx.reshape(-1).astype(jnp.int32)
    onehot = (e_flat[:, None] == jnp.arange(N_EXPERTS, dtype=jnp.int32)[None, :]).astype(jnp.int32)
    rank = jnp.sum((jnp.cumsum(onehot, axis=0) - onehot) * onehot, axis=1)
    counts = jnp.sum(onehot, axis=0)
    padded = ((counts + MOE_TM - 1) // MOE_TM) * MOE_TM
    ends = jnp.cumsum(padded)
    starts = ends - padded
    dest = starts[e_flat] + rank

    Tp = TOP_K * T + N_EXPERTS * MOE_TM
    n_tiles = Tp // MOE_TM
    tile_start = jnp.arange(n_tiles, dtype=jnp.int32) * MOE_TM
    tile_expert = jnp.sum((tile_start[:, None] >= ends[None, :]).astype(jnp.int32), axis=1)
    last_expert = jnp.sum((jnp.maximum(ends[-1] - 1, 0) >= ends).astype(jnp.int32))
    tile_expert = jnp.minimum(tile_expert, last_expert).astype(jnp.int32)
    filled_end = (starts + counts)[tile_expert]
    tile_rows = jnp.where(tile_start < ends[-1], jnp.clip(filled_end - tile_start, 0, MOE_TM), 0).astype(jnp.int32)

    _, pair_by_row = lax.sort_key_val(dest, jnp.arange(TOP_K * T, dtype=jnp.int32))
    row_expert = jnp.repeat(tile_expert, MOE_TM)
    local = jnp.arange(Tp, dtype=jnp.int32) - starts[row_expert]
    compact = (jnp.cumsum(counts) - counts)[row_expert] + local
    valid = jnp.logical_and(local < counts[row_expert], jnp.repeat(tile_rows, MOE_TM) > 0)
    src = jnp.where(valid, pair_by_row.at[jnp.where(valid, compact, 0)].get(mode="promise_in_bounds") // TOP_K, 0)

    runs = max(d for d in range(1, MOE_RUNS + 1) if n_tiles % d == 0)
    tiles = n_tiles // runs
    ys = None
    for c in range(runs):
        xs = xb.at[src[c * tiles * MOE_TM:(c + 1) * tiles * MOE_TM]].get(mode="promise_in_bounds")
        ys = _moe_grouped(xs, tile_expert[c * tiles:(c + 1) * tiles], tile_rows[c * tiles:(c + 1) * tiles],
                          w_gate, w_up, w_down, ys, c * tiles, n_tiles)
    slot_major = dest.reshape(T, TOP_K).T.reshape(-1)
    y_rows = ys.at[slot_major].get(mode="promise_in_bounds")
    return _combine_ln(x, y_rows, top_w, gamma, beta)


def _memattn_kernel(q_ref, k_ref, v_ref, o_ref):
    q = q_ref[0]
    outs = []
    for h in range(MEM_HEADS):
        sl = slice(h * MEM_HEAD_DIM, (h + 1) * MEM_HEAD_DIM)
        s = _dot_nt(q[:, sl].astype(BF16), k_ref[0, :, sl]) * (MEM_HEAD_DIM ** -0.5)
        e = jnp.exp(s - jnp.max(s, axis=-1, keepdims=True))
        p = e / jnp.sum(e, axis=-1, keepdims=True)
        outs.append(_dot(p.astype(BF16), v_ref[0, :, sl]))
    o_ref[0] = jnp.concatenate(outs, axis=-1).astype(o_ref.dtype)


def _memory_attention(p3, mem_kv):
    B, S, _ = p3.shape
    M = mem_kv.shape[1]
    ts = min(512, S)
    return pl.pallas_call(
        _memattn_kernel,
        out_shape=jax.ShapeDtypeStruct((B, S, MEM_WIDTH), BF16),
        grid=(B, S // ts),
        in_specs=[pl.BlockSpec((1, ts, MEM_WIDTH), lambda b, s: (b, s, Q_MEM_OFF // MEM_WIDTH)),
                  pl.BlockSpec((1, M, MEM_WIDTH), lambda b, s: (b, 0, 0)),
                  pl.BlockSpec((1, M, MEM_WIDTH), lambda b, s: (b, 0, 1))],
        out_specs=pl.BlockSpec((1, ts, MEM_WIDTH), lambda b, s: (b, s, 0)),
        compiler_params=_cparams("parallel", "parallel"),
        name="mem_attn",
    )(p3, mem_kv, mem_kv)


def _ret_kernel(q_ref, k_ref, v_ref, g_ref, cos_ref, sin_ref, intra_ref, qd_ref, kd_ref, cd_ref,
                gn_ref, o_ref, state_ref):
    @pl.when(pl.program_id(1) == 0)
    def _():
        state_ref[...] = jnp.zeros_like(state_ref)

    cosf = cos_ref[0]
    sinf = sin_ref[0]
    half = RET_DK // 2
    rot = lambda x: x * cosf + pltpu.roll(x, half, 1) * sinf
    hs = range(RET_HEADS)
    ksl = [slice(h * RET_DK, (h + 1) * RET_DK) for h in hs]
    vsl = [slice(h * RET_DV, (h + 1) * RET_DV) for h in hs]

    qr = [rot(q_ref[0, :, ksl[h]].astype(F32)) for h in hs]
    kr = [rot(k_ref[0, :, ksl[h]].astype(F32)) * (RET_DK ** -0.5) for h in hs]
    vb = [v_ref[0, :, vsl[h]].astype(BF16) for h in hs]
    state = [state_ref[h] for h in hs]
    s = [_dot_nt(qr[h].astype(BF16), kr[h].astype(BF16)) * intra_ref[h] for h in hs]
    o = [_dot(s[h].astype(BF16), vb[h]) + _dot((qr[h] * qd_ref[h]).astype(BF16), state[h].astype(BF16))
         for h in hs]
    for h in hs:
        state_ref[h] = state[h] * cd_ref[h] + _dot_tn((kr[h] * kd_ref[h]).astype(BF16), vb[h])
    for h in hs:
        mu = jnp.mean(o[h], axis=-1, keepdims=True)
        oc = o[h] - mu
        var = jnp.mean(oc * oc, axis=-1, keepdims=True)
        y = oc * lax.rsqrt(var + LN_EPS) * gn_ref[0:1, vsl[h]] + gn_ref[1:2, vsl[h]]
        g = g_ref[0, :, vsl[h]].astype(F32)
        o_ref[0, :, vsl[h]] = (y * (g * jax.nn.sigmoid(g))).astype(o_ref.dtype)


def _retention_mixer(p3, positions, gn):
    B, S, _ = p3.shape
    C = RET_CHUNK
    H = RET_HEADS
    inv = 1.0 / (ROPE_BASE ** (jnp.arange(0, RET_DK, 2, dtype=F32) / RET_DK))
    ang = positions.astype(F32)[..., None] * inv
    cos, sin = jnp.cos(ang), jnp.sin(ang)
    cosf = jnp.concatenate([cos, cos], axis=-1)
    sinf = jnp.concatenate([-sin, sin], axis=-1)

    log_g = jnp.log1p(-jnp.exp2(-5.0 - jnp.arange(H, dtype=F32)))
    idx = jnp.arange(C, dtype=F32)
    diff = idx[:, None] - idx[None, :]
    intra = jnp.where(diff >= 0, jnp.exp(jnp.maximum(diff, 0.0)[None] * log_g[:, None, None]), 0.0)
    qd = jnp.broadcast_to(jnp.exp((idx + 1.0)[None, :, None] * log_g[:, None, None]), (H, C, RET_DK))
    kd = jnp.broadcast_to(jnp.exp((C - 1.0 - idx)[None, :, None] * log_g[:, None, None]), (H, C, RET_DK))
    cd = jnp.broadcast_to(jnp.exp(C * log_g)[:, None, None], (H, 1, RET_DV))

    QK = H * RET_DK
    assert 2 * QK == SEQ_WIDTH
    whole = lambda b, c: (0, 0, 0)
    return pl.pallas_call(
        _ret_kernel,
        out_shape=jax.ShapeDtypeStruct((B, S, SEQ_WIDTH), BF16),
        grid=(B, S // C),
        in_specs=[pl.BlockSpec((1, C, QK), lambda b, c: (b, c, 0)),
                  pl.BlockSpec((1, C, QK), lambda b, c: (b, c, 1)),
                  pl.BlockSpec((1, C, SEQ_WIDTH), lambda b, c: (b, c, 1)),
                  pl.BlockSpec((1, C, SEQ_WIDTH), lambda b, c: (b, c, 2)),
                  pl.BlockSpec((1, C, RET_DK), lambda b, c: (b, c, 0)),
                  pl.BlockSpec((1, C, RET_DK), lambda b, c: (b, c, 0)),
                  pl.BlockSpec((H, C, C), whole),
                  pl.BlockSpec((H, C, RET_DK), whole),
                  pl.BlockSpec((H, C, RET_DK), whole),
                  pl.BlockSpec((H, 1, RET_DV), whole),
                  pl.BlockSpec((2, SEQ_WIDTH), lambda b, c: (0, 0))],
        out_specs=pl.BlockSpec((1, C, SEQ_WIDTH), lambda b, c: (b, c, 0)),
        scratch_shapes=[pltpu.VMEM((H, RET_DK, RET_DV), F32)],
        compiler_params=_cparams("parallel", "arbitrary"),
        name="retention",
    )(p3, p3, p3, p3, cosf, sinf, intra, qd, kd, cd, gn)


RWKV_PAIRS = 12
RWKV_SUB = 2
PAIR = 2 * RWKV_HEAD
PREV_ROWS = 16

_NN = ((1,), (0,))
_NT = ((1,), (1,))
_TN = ((0,), (0,))


def _dg(a, b, dims):
    return lax.dot_general(a.astype(BF16), b.astype(BF16), (dims, ((), ())), preferred_element_type=F32)


def _rwkv_masks():
    L = RWKV_CHUNK
    row = np.arange(L)[:, None]
    lane = np.arange(PAIR)[None, :]
    m1 = np.broadcast_to(lane < RWKV_HEAD, (L, PAIR))
    small = np.stack([m1, ~m1, row > lane % L, row >= lane % L, row == lane % L]).astype(np.float32)
    same_head = (np.arange(2 * L)[:, None] // L == lane // L).astype(np.float32)
    tri = np.kron(np.eye(RWKV_SUB), row >= np.arange(L)[None, :]).astype(np.float32)
    return jnp.asarray(small), jnp.asarray(same_head, dtype=BF16), jnp.asarray(tri, dtype=BF16)


def _stacker(m1, m2):
    m1b, m2b = m1.astype(BF16), m2.astype(BF16)

    def stack(x):
        xb = x.astype(BF16)
        return jnp.concatenate([xb * m1b, xb * m2b], axis=0)

    return stack


def _rwkv_prepare(r, c, lw, k, v, kk, a, masks):
    n = len(r)
    rng = range(n)
    L = r[0].shape[0]
    assert L == RWKV_HEAD
    m1, m2, strict, incl, eye = masks
    stack = _stacker(m1, m2)
    blockdiag = stack
    strict_b = strict.astype(BF16)
    incl_b = jnp.concatenate([incl, incl], axis=1).astype(BF16)

    p_incl = [jnp.exp(c[j]) for j in rng]
    p_inv = [jnp.exp(-c[j]) for j in rng]
    at = [-kk[j] * jnp.exp(c[j] - lw[j]) for j in rng]
    rt = [r[j] * p_incl[j] for j in rng]
    lhs = [jnp.concatenate([at[j], rt[j]], axis=0).astype(BF16) for j in rng]
    rhs = [jnp.concatenate([stack(kk[j] * a[j] * p_inv[j]), stack(k[j] * p_inv[j])], axis=0)
           for j in rng]
    sc = [_dg(lhs[j], rhs[j], _NT) for j in rng]
    nmat = [sc[j][:L, :PAIR] * strict for j in rng]
    tinv = [eye + nmat[j] for j in rng]
    pw = [_dg(nmat[j], blockdiag(nmat[j]), _NN) for j in rng]
    rounds = int(np.log2(L)) - 1
    for i in range(rounds):
        if i < rounds - 1:
            both = [_dg(jnp.concatenate([pw[j], tinv[j]], axis=0), blockdiag(pw[j]), _NN) for j in rng]
            pw = [both[j][:L] for j in rng]
            tinv = [tinv[j] + both[j][L:] for j in rng]
        else:
            tinv = [tinv[j] + _dg(tinv[j], blockdiag(pw[j]), _NN) for j in rng]

    vs = [stack(v[j]) for j in rng]
    mv = [_dg(sc[j][:L, PAIR:].astype(BF16) * strict_b, vs[j], _NN) for j in rng]
    a_r = [sc[j][L:, :].astype(BF16) * incl_b for j in rng]
    return dict(lhs=lhs, rhs=rhs, vs=vs, mv=mv, a_r=a_r, tinv=[tinv[j].astype(BF16) for j in rng],
                decay=[p_incl[j][L - 1:L, :] for j in rng])


def _rwkv_apply(prep, entries, state, masks):
    m1, m2 = masks[0], masks[1]
    stack = _stacker(m1, m2)
    L = m1.shape[0]
    rng = range(len(entries))
    get = lambda name: [prep[name][e] for e in entries]
    lhs, rhs, vs, mv, a_r, tinv, decay = (get(n) for n in ("lhs", "rhs", "vs", "mv", "a_r", "tinv", "decay"))
    sg = [_dg(lhs[j], state[j], _NT) for j in rng]
    u = [_dg(tinv[j], stack(sg[j][:L] + mv[j]), _NN) for j in rng]
    uv = [jnp.concatenate([stack(u[j]), vs[j]], axis=0) for j in rng]
    ys = [_dg(a_r[j], uv[j], _NN) for j in rng]
    y = [sg[j][L:] + ys[j] for j in rng]
    new_state = [(state[j] + _dg(uv[j], rhs[j], _TN)) * decay[j] for j in rng]
    return y, new_state


def _shift_mix(x, prev_block, mu, first):
    prev_row = jnp.where(first, 0.0, prev_block[PREV_ROWS - 1:PREV_ROWS, :])
    rows = lax.broadcasted_iota(jnp.int32, x.shape, 0)
    prev = jnp.where(rows == 0, prev_row, pltpu.roll(x, 1, 0))
    return x + mu * (prev - x)


def _rwkv_mixer_kernel(*refs, with_v):
    it = iter(refs)
    r_ref, k_ref, v_ref, xg_ref, xl_ref = (next(it) for _ in range(5))
    rp_ref, kp_ref, vp_ref, xgp_ref, xlp_ref = (next(it) for _ in range(5))
    mu3_ref, mug_ref, mul_ref, vec_ref = (next(it) for _ in range(4))
    small_ref, ones_ref, tri_ref = (next(it) for _ in range(3))
    ww_ref, wa_ref = next(it), next(it)
    wv_ref = next(it) if with_v else None
    wg_ref = next(it)
    vf_ref = next(it) if with_v else None
    y_ref = next(it)
    vfo_ref = None if with_v else next(it)
    state_ref = next(it)

    first = pl.program_id(2) == 0

    @pl.when(first)
    def _():
        state_ref[...] = jnp.zeros_like(state_ref)

    L = RWKV_CHUNK
    mix = lambda x_ref, p_ref, mu: _shift_mix(x_ref[0].astype(F32), p_ref[0].astype(F32), mu, first)
    r = mix(r_ref, rp_ref, mu3_ref[0:1, :])
    k = mix(k_ref, kp_ref, mu3_ref[1:2, :])
    v = mix(v_ref, vp_ref, mu3_ref[2:3, :])
    xg = mix(xg_ref, xgp_ref, mug_ref[...])
    xl = mix(xl_ref, xlp_ref, mul_ref[...])

    w0, a0, v0 = vec_ref[0:1, :], vec_ref[1:2, :], vec_ref[2:3, :]
    k_k, k_a, r_k = vec_ref[3:4, :], vec_ref[4:5, :], vec_ref[5:6, :]
    gn_g, gn_b = vec_ref[6:7, :], vec_ref[7:8, :]

    lane = lax.broadcasted_iota(jnp.int32, xl.shape, 1)
    lin = jnp.where(lane < LORA_W, jnp.tanh(xl), xl).astype(BF16)
    w_arg = w0 + _dot(lin, ww_ref[...])
    a = jax.nn.sigmoid(a0 + _dot(lin, wa_ref[...]))
    if with_v:
        v = v + (vf_ref[0] - v) * jax.nn.sigmoid(v0 + _dot(lin, wv_ref[...]))
    else:
        vfo_ref[0] = v
    g = _dot(jax.nn.sigmoid(xg).astype(BF16), wg_ref[...])

    w_log = -(jnp.maximum(-w_arg, 0.0) + jnp.log1p(jnp.exp(-jnp.abs(w_arg)))) - 0.5
    lw = -jnp.exp(w_log)
    tri = tri_ref[...]
    h1 = lw.astype(BF16)
    h2 = (lw - h1.astype(F32)).astype(BF16)
    c = _dot(tri, h1) + _dot(tri, h2)

    kk_raw = k * k_k
    k2 = k * (1.0 + (a - 1.0) * k_a)
    ones_bd = ones_ref[...]
    masks = tuple(small_ref[i] for i in range(5))
    inv_n = 1.0 / RWKV_HEAD

    pairs = range(RWKV_PAIRS)
    cols = [slice(j * PAIR, (j + 1) * PAIR) for j in pairs]
    rows = [slice(s * L, (s + 1) * L) for s in range(RWKV_SUB)]
    where = [(rs, cs) for rs in rows for cs in cols]
    ent = range(len(where))
    part = lambda x: [x[rs, cs] for rs, cs in where]
    kkr, r_e, k_e, v_e, g_e = part(kk_raw), part(r), part(k2), part(v), part(g)
    red = [_dot(jnp.concatenate([kkr[e] * kkr[e], r_e[e] * k_e[e] * r_k[:, where[e][1]]], axis=0).astype(BF16),
                ones_bd) for e in ent]
    kks = [kkr[e] / jnp.maximum(jnp.sqrt(red[e][:L]), 1e-12) for e in ent]
    prep = _rwkv_prepare(r_e, part(c), part(lw), k_e, v_e, kks, part(a), masks)
    state = [state_ref[j] for j in pairs]
    ys = []
    for s in range(RWKV_SUB):
        y_s, state = _rwkv_apply(prep, [s * RWKV_PAIRS + j for j in pairs], state, masks)
        ys += y_s
    for j in pairs:
        state_ref[j] = state[j]
    yc = [ys[e] - _dot(ys[e].astype(BF16), ones_bd) * inv_n for e in ent]
    var = [_dot((yc[e] * yc[e]).astype(BF16), ones_bd) * inv_n for e in ent]
    for e in ent:
        rs, cs = where[e]
        yn = yc[e] * lax.rsqrt(var[e] + RWKV_GN_EPS) * gn_g[:, cs] + gn_b[:, cs]
        y_ref[0, rs, cs] = ((yn + red[e][L:] * v_e[e]) * g_e[e]).astype(y_ref.dtype)


def _rwkv_layout(w_in, mu, with_v):
    sizes = [SEQ_WIDTH] * 3 + [LORA_W, LORA_A] + ([LORA_V] if with_v else []) + [LORA_G]
    offs = np.concatenate([[0], np.cumsum(sizes)])
    seq_cols = int(offs[-1])
    g_idx = len(sizes) - 1
    pad = LORA_PACK - LORA_W - LORA_A - (LORA_V if with_v else 0)
    cols = [w_in[:, :3 * SEQ_WIDTH], w_in[:, seq_cols:], w_in[:, offs[g_idx]:offs[g_idx + 1]],
            w_in[:, offs[3]:offs[g_idx]], jnp.zeros((w_in.shape[0], pad), w_in.dtype)]
    mus = [mu[:3 * SEQ_WIDTH], jnp.zeros((MEM_WIDTH,), mu.dtype), mu[offs[g_idx]:offs[g_idx + 1]],
           mu[offs[3]:offs[g_idx]], jnp.zeros((pad,), mu.dtype)]
    return jnp.concatenate(cols, axis=1), jnp.concatenate(mus)


def _rwkv_mixer(p3, mu, lora_w, lora_a, lora_g, lora_v, vecs, v_first):
    B, S, _ = p3.shape
    L = RWKV_SUB * RWKV_CHUNK
    W = RWKV_PAIRS * PAIR
    above = lambda t: jnp.maximum(t * (L // PREV_ROWS) - 1, 0)
    with_v = lora_v is not None
    if with_v:
        w0, a0, v0, k_k, k_a, r_k, gn_g, gn_b = vecs
    else:
        w0, a0, k_k, k_a, r_k, gn_g, gn_b = vecs
        v0 = jnp.zeros_like(w0)
    vec8 = jnp.stack([w0, a0, v0, k_k, k_a, r_k, gn_g, gn_b])
    mu3 = jnp.concatenate([mu[:3 * SEQ_WIDTH].reshape(3, SEQ_WIDTH), jnp.zeros((5, SEQ_WIDTH), F32)])
    g_off = Q_MEM_OFF + MEM_WIDTH
    l_off = g_off + LORA_G
    mug = mu[g_off:g_off + LORA_G].reshape(1, LORA_G)
    mul = mu[l_off:l_off + LORA_PACK].reshape(1, LORA_PACK)
    rows = lambda w, lo: jnp.zeros((LORA_PACK, SEQ_WIDTH), F32).at[lo:lo + w.shape[0]].set(w).astype(BF16)

    def cols(width, off):
        blk = off // width
        return pl.BlockSpec((1, L, width), lambda b, g, t: (b, t, blk))

    def cols_prev(width, off):
        blk = off // width
        return pl.BlockSpec((1, PREV_ROWS, width), lambda b, g, t: (b, above(t), blk))

    def seq(off):
        blk = off // W
        return pl.BlockSpec((1, L, W), lambda b, g, t: (b, t, blk + g))

    def seq_prev(off):
        blk = off // W
        return pl.BlockSpec((1, PREV_ROWS, W), lambda b, g, t: (b, above(t), blk + g))

    per_group = lambda nrows: pl.BlockSpec((nrows, W), lambda b, g, t: (0, g))
    fixed = lambda ncols: pl.BlockSpec((1, ncols), lambda b, g, t: (0, 0))

    in_specs = [seq(0), seq(SEQ_WIDTH), seq(2 * SEQ_WIDTH), cols(LORA_G, g_off), cols(LORA_PACK, l_off),
                seq_prev(0), seq_prev(SEQ_WIDTH), seq_prev(2 * SEQ_WIDTH),
                cols_prev(LORA_G, g_off), cols_prev(LORA_PACK, l_off),
                per_group(8), fixed(LORA_G), fixed(LORA_PACK), per_group(8)]
    masks = _rwkv_masks()
    in_specs += [pl.BlockSpec(m.shape, lambda b, g, t, nd=m.ndim: (0,) * nd) for m in masks]
    in_specs += [per_group(LORA_PACK), per_group(LORA_PACK)]
    args = [p3] * 10 + [mu3, mug, mul, vec8, *masks, rows(lora_w, 0), rows(lora_a, LORA_W)]
    if with_v:
        in_specs.append(per_group(LORA_PACK))
        args.append(rows(lora_v, LORA_W + LORA_A))
    in_specs.append(per_group(LORA_G))
    args.append(lora_g.astype(BF16))
    out_spec = pl.BlockSpec((1, L, W), lambda b, g, t: (b, t, g))
    y_shape = jax.ShapeDtypeStruct((B, S, SEQ_WIDTH), BF16)
    if with_v:
        in_specs.append(out_spec)
        args.append(v_first)
        out_shape, out_specs = y_shape, out_spec
    else:
        out_shape = (y_shape, jax.ShapeDtypeStruct((B, S, SEQ_WIDTH), F32))
        out_specs = (out_spec, out_spec)

    res = pl.pallas_call(
        functools.partial(_rwkv_mixer_kernel, with_v=with_v),
        out_shape=out_shape,
        grid=(B, SEQ_WIDTH // W, S // L),
        in_specs=in_specs,
        out_specs=out_specs,
        scratch_shapes=[pltpu.VMEM((RWKV_PAIRS, PAIR, PAIR), F32)],
        compiler_params=_cparams("parallel", "parallel", "arbitrary"),
        name="rwkv_mixer",
    )(*args)
    return (res, v_first) if with_v else res


def kernel(x, mem, positions, mem_w_kv, l0_w_in, l0_mu, l0_lora_w, l0_lora_a, l0_lora_g, l0_vecs, l0_w_out, l0_norms, l0_ffn_gate, l0_ffn_up, l0_ffn_down, l1_w_in, l1_gn, l1_w_out, l1_norms, l1_router, l1_moe_gate, l1_moe_up, l1_moe_down, l2_w_in, l2_mu, l2_lora_w, l2_lora_a, l2_lora_v, l2_lora_g, l2_vecs, l2_w_out, l2_norms, l2_ffn_gate, l2_ffn_up, l2_ffn_down, l3_w_in, l3_gn, l3_w_out, l3_norms, l3_router, l3_moe_gate, l3_moe_up, l3_moe_down):
    B, S, D = x.shape
    T = B * S
    M = mem.shape[1]
    mem_kv = _matmul(mem.reshape(B * M, D), mem_w_kv.astype(BF16), tm=1024, tn=512,
                     out_dtype=BF16).reshape(B, M, 2 * MEM_WIDTH)

    rwkv_layers = {
        0: (l0_w_in, l0_mu, l0_lora_w, l0_lora_a, l0_lora_g, None, l0_vecs),
        2: (l2_w_in, l2_mu, l2_lora_w, l2_lora_a, l2_lora_g, l2_lora_v, l2_vecs),
    }
    ret_layers = {1: (l1_w_in, l1_gn), 3: (l3_w_in, l3_gn)}
    w_outs = (l0_w_out, l1_w_out, l2_w_out, l3_w_out)
    norms = (l0_norms, l1_norms, l2_norms, l3_norms)
    dense = {0: (l0_ffn_gate, l0_ffn_up, l0_ffn_down), 2: (l2_ffn_gate, l2_ffn_up, l2_ffn_down)}
    moe = {1: (l1_router, l1_moe_gate, l1_moe_up, l1_moe_down),
           3: (l3_router, l3_moe_gate, l3_moe_up, l3_moe_down)}

    xf = x.reshape(T, D)
    xin = xf
    v_first = None
    for i in range(DEPTH):
        if i in rwkv_layers:
            w_in, mu, lw_, la_, lg_, lv_, vecs = rwkv_layers[i]
            w_in, mu = _rwkv_layout(w_in, mu, lv_ is not None)
            p3 = _matmul(xin, w_in.astype(BF16), tm=512, tn=w_in.shape[1], out_dtype=BF16).reshape(B, S, -1)
            y_seq, v_first = _rwkv_mixer(p3, mu, lw_, la_, lg_, lv_, vecs, v_first)
        else:
            w_in, gn = ret_layers[i]
            p3 = _matmul(xin, w_in.astype(BF16), tm=512, tn=w_in.shape[1], out_dtype=BF16).reshape(B, S, -1)
            y_seq = _retention_mixer(p3, positions, gn)
        y_mem = _memory_attention(p3, mem_kv)
        nrm = norms[i]
        proj = _proj_ln(y_seq.reshape(T, SEQ_WIDTH), y_mem.reshape(T, MEM_WIDTH), w_outs[i], xf,
                        nrm[0], nrm[1], None if i in dense else moe[i][0])
        if i in dense:
            xf, xb = _ffn_ln(*proj, *dense[i], nrm[2], nrm[3])
        else:
            xf, xb = _moe_ln(*proj, *moe[i][1:], nrm[2], nrm[3])
        xin = xb
    return xf.reshape(B, S, D)
```

```python
import functools

import numpy as np
import jax
import jax.numpy as jnp
from jax import lax
from jax.experimental import pallas as pl
from jax.experimental.pallas import tpu as pltpu

F32 = jnp.float32
BF16 = jnp.bfloat16

D_MODEL = 2048
DEPTH = 4
MEM_HEADS = 4
MEM_HEAD_DIM = 128
MEM_WIDTH = MEM_HEADS * MEM_HEAD_DIM
SEQ_WIDTH = D_MODEL - MEM_WIDTH

RWKV_HEAD = 64
RWKV_HEADS = SEQ_WIDTH // RWKV_HEAD
LORA_W = 96
LORA_A = 96
LORA_V = 64
LORA_G = 256
LORA_PACK = 256
RWKV_GN_EPS = 1e-5 * RWKV_HEAD
RWKV_CHUNK = 64

RET_HEADS = 6
RET_DV = SEQ_WIDTH // RET_HEADS
RET_DK = RET_DV // 2
RET_CHUNK = 128
ROPE_BASE = 10000.0

N_EXPERTS = 8
TOP_K = 2

ALPHA = (2.0 * DEPTH) ** 0.25
LN_EPS = 1e-5

LANES = 128
Q_MEM_OFF = 3 * SEQ_WIDTH
VMEM_LIMIT = 52 * 1024 * 1024
MOE_VMEM_LIMIT = 58 * 1024 * 1024


def _cparams(*semantics):
    return pltpu.CompilerParams(dimension_semantics=semantics, vmem_limit_bytes=VMEM_LIMIT)


def _dot(a, b):
    return jnp.dot(a, b, preferred_element_type=F32)


def _dot_nt(a, b):
    return lax.dot_general(a, b, (((1,), (1,)), ((), ())), preferred_element_type=F32)


def _dot_tn(a, b):
    return lax.dot_general(a, b, (((0,), (0,)), ((), ())), preferred_element_type=F32)


def _layer_norm_rows(z, g, b):
    mu = jnp.mean(z, axis=-1, keepdims=True)
    zc = z - mu
    var = jnp.mean(zc * zc, axis=-1, keepdims=True)
    return zc * lax.rsqrt(var + LN_EPS) * g + b


def _mm_kernel(a_ref, w_ref, o_ref, abf_ref):
    @pl.when(pl.program_id(1) == 0)
    def _():
        abf_ref[...] = a_ref[...].astype(BF16)

    o_ref[...] = _dot(abf_ref[...], w_ref[...]).astype(o_ref.dtype)


def _matmul(a, w, *, tm, tn, out_dtype=F32):
    M, K = a.shape
    N = w.shape[1]
    tm = min(tm, M)
    assert M % tm == 0 and N % tn == 0
    w_mode = pl.Buffered(1) if tn == N else None
    return pl.pallas_call(
        _mm_kernel,
        out_shape=jax.ShapeDtypeStruct((M, N), out_dtype),
        grid=(M // tm, N // tn),
        in_specs=[pl.BlockSpec((tm, K), lambda i, j: (i, 0)),
                  pl.BlockSpec((K, tn), lambda i, j: (0, j), pipeline_mode=w_mode)],
        out_specs=pl.BlockSpec((tm, tn), lambda i, j: (i, j)),
        scratch_shapes=[pltpu.VMEM((tm, K), BF16)],
        compiler_params=_cparams("parallel", "arbitrary"),
        name="matmul",
    )(a, w)


def _proj_ln_kernel(*refs, with_router):
    ys_ref, ym_ref, ws_ref, wm_ref, x_ref, g_ref, b_ref = refs[:7]
    if with_router:
        wrh_ref, wrl_ref, o_ref, ob_ref, logits_ref = refs[7:]
    else:
        o_ref, ob_ref = refs[7:]
    half = x_ref.shape[0] // 2
    for s in range(2):
        rs = slice(s * half, (s + 1) * half)
        h = _dot(ys_ref[rs, :], ws_ref[...]) + _dot(ym_ref[rs, :], wm_ref[...])
        out = _layer_norm_rows(ALPHA * x_ref[rs, :] + h, g_ref[...], b_ref[...])
        hi = out.astype(BF16)
        o_ref[rs, :] = out
        ob_ref[rs, :] = hi
        if with_router:
            lo = (out - hi.astype(F32)).astype(BF16)
            logits_ref[rs, :] = _dot(hi, wrh_ref[...]) + (_dot(lo, wrh_ref[...]) + _dot(hi, wrl_ref[...]))


def _proj_ln(y_seq, y_mem, w_out, x, gamma, beta, w_router=None):
    T = x.shape[0]
    tm = min(512, T)
    ws = w_out[:SEQ_WIDTH].astype(BF16)
    wm = w_out[SEQ_WIDTH:].astype(BF16)
    row = lambda i: (i, 0)
    fixed = lambda i: (0, 0)
    once = pl.Buffered(1)
    in_specs = [pl.BlockSpec((tm, SEQ_WIDTH), row), pl.BlockSpec((tm, MEM_WIDTH), row),
                pl.BlockSpec((SEQ_WIDTH, D_MODEL), fixed, pipeline_mode=once),
                pl.BlockSpec((MEM_WIDTH, D_MODEL), fixed, pipeline_mode=once),
                pl.BlockSpec((tm, D_MODEL), row),
                pl.BlockSpec((1, D_MODEL), fixed), pl.BlockSpec((1, D_MODEL), fixed)]
    args = [y_seq, y_mem, ws, wm, x, gamma.reshape(1, -1), beta.reshape(1, -1)]
    out_shape = [jax.ShapeDtypeStruct((T, D_MODEL), F32), jax.ShapeDtypeStruct((T, D_MODEL), BF16)]
    out_specs = [pl.BlockSpec((tm, D_MODEL), row), pl.BlockSpec((tm, D_MODEL), row)]
    if w_router is not None:
        wr = jnp.pad(w_router, ((0, 0), (0, LANES - N_EXPERTS)))
        wr_hi = wr.astype(BF16)
        wr_lo = (wr - wr_hi.astype(F32)).astype(BF16)
        in_specs += [pl.BlockSpec((D_MODEL, LANES), fixed, pipeline_mode=once)] * 2
        args += [wr_hi, wr_lo]
        out_shape.append(jax.ShapeDtypeStruct((T, LANES), F32))
        out_specs.append(pl.BlockSpec((tm, LANES), row))
    res = pl.pallas_call(
        functools.partial(_proj_ln_kernel, with_router=w_router is not None),
        out_shape=tuple(out_shape),
        grid=(T // tm,),
        in_specs=in_specs,
        out_specs=tuple(out_specs),
        compiler_params=_cparams("parallel"),
        name="proj_ln",
    )(*args)
    if w_router is None:
        return res
    return res[0], res[1], res[2][:, :N_EXPERTS]


def _swiglu_partial(xb, wg, wu, wd, splits=1):
    w = wg.shape[1] // splits
    out = None
    for s in range(splits):
        cs = slice(s * w, (s + 1) * w)
        gate = _dot(xb, wg[:, cs])
        up = _dot(xb, wu[:, cs])
        h = (gate * jax.nn.sigmoid(gate) * up).astype(BF16)
        d = _dot(h, wd[cs, :])
        out = d if out is None else out + d
    return out


def _ffn_ln_kernel(xb_ref, wg_ref, wu_ref, wd_ref, x_ref, g_ref, b_ref, o_ref, ob_ref, acc_ref):
    f = pl.program_id(1)

    @pl.when(f == 0)
    def _():
        acc_ref[...] = jnp.zeros_like(acc_ref)

    acc_ref[...] += _swiglu_partial(xb_ref[...], wg_ref[...], wu_ref[...], wd_ref[...], splits=2)

    @pl.when(f == pl.num_programs(1) - 1)
    def _():
        out = _layer_norm_rows(ALPHA * x_ref[...] + acc_ref[...], g_ref[...], b_ref[...])
        o_ref[...] = out
        ob_ref[...] = out.astype(BF16)


def _ffn_ln(x, xb, w_gate, w_up, w_down, gamma, beta):
    T = x.shape[0]
    F = w_gate.shape[1]
    tm = min(512, T)
    tf = 512
    row = lambda i, f: (i, 0)
    fixed = lambda i, f: (0, 0)
    return pl.pallas_call(
        _ffn_ln_kernel,
        out_shape=(jax.ShapeDtypeStruct((T, D_MODEL), F32), jax.ShapeDtypeStruct((T, D_MODEL), BF16)),
        grid=(T // tm, F // tf),
        in_specs=[pl.BlockSpec((tm, D_MODEL), row),
                  pl.BlockSpec((D_MODEL, tf), lambda i, f: (0, f)),
                  pl.BlockSpec((D_MODEL, tf), lambda i, f: (0, f)),
                  pl.BlockSpec((tf, D_MODEL), lambda i, f: (f, 0)),
                  pl.BlockSpec((tm, D_MODEL), row),
                  pl.BlockSpec((1, D_MODEL), fixed), pl.BlockSpec((1, D_MODEL), fixed)],
        out_specs=(pl.BlockSpec((tm, D_MODEL), row), pl.BlockSpec((tm, D_MODEL), row)),
        scratch_shapes=[pltpu.VMEM((tm, D_MODEL), F32)],
        compiler_params=_cparams("parallel", "arbitrary"),
        name="ffn_ln",
    )(xb, w_gate.astype(BF16), w_up.astype(BF16), w_down.astype(BF16), x,
      gamma.reshape(1, -1), beta.reshape(1, -1))


MOE_TM = 1024
MOE_TF = 512
MOE_RUNS = 4


def _moe_kernel(te_ref, nr_ref, xs_ref, wg_ref, wu_ref, wd_ref, prev_ref, o_ref, acc_ref):
    del prev_ref
    i = pl.program_id(0)
    f = pl.program_id(1)
    last = pl.num_programs(1) - 1
    rows = nr_ref[i]
    half = MOE_TM // 2
    used = rows > 0

    @pl.when(jnp.logical_and(used, f == 0))
    def _():
        acc_ref[...] = jnp.zeros_like(acc_ref)

    @pl.when(rows > half)
    def _():
        acc_ref[...] += _swiglu_partial(xs_ref[...], wg_ref[...].astype(BF16), wu_ref[...].astype(BF16),
                                        wd_ref[...].astype(BF16))

    @pl.when(jnp.logical_and(used, rows <= half))
    def _():
        acc_ref[:half, :] += _swiglu_partial(xs_ref[:half, :], wg_ref[...].astype(BF16),
                                             wu_ref[...].astype(BF16), wd_ref[...].astype(BF16))

    @pl.when(jnp.logical_and(used, f == last))
    def _():
        o_ref[...] = acc_ref[...].astype(o_ref.dtype)

    @pl.when(jnp.logical_and(jnp.logical_not(used), f == last))
    def _():
        o_ref[...] = jnp.zeros_like(o_ref)


def _moe_grouped(xs, tile_expert, tile_rows, w_gate, w_up, w_down, ys_prev, tile0, total_tiles):
    tiles = xs.shape[0] // MOE_TM
    F = w_gate.shape[2]
    nf = F // MOE_TF

    def fcol(i, f, nr):
        return jnp.where(nr[i] > 0, f, nf - 1)

    in_specs = [pl.BlockSpec((MOE_TM, D_MODEL), lambda i, f, te, nu: (i, 0)),
                pl.BlockSpec((None, D_MODEL, MOE_TF), lambda i, f, te, nu: (te[i], 0, fcol(i, f, nu))),
                pl.BlockSpec((None, D_MODEL, MOE_TF), lambda i, f, te, nu: (te[i], 0, fcol(i, f, nu))),
                pl.BlockSpec((None, MOE_TF, D_MODEL), lambda i, f, te, nu: (te[i], fcol(i, f, nu), 0))]
    in_specs.append(pl.BlockSpec(memory_space=pl.ANY))
    args = [tile_expert, tile_rows, xs, w_gate, w_up, w_down, ys_prev]
    grid_spec = pltpu.PrefetchScalarGridSpec(
        num_scalar_prefetch=2,
        grid=(tiles, nf),
        in_specs=in_specs,
        out_specs=pl.BlockSpec((MOE_TM, D_MODEL), lambda i, f, te, nu: (tile0 + i, 0)),
        scratch_shapes=[pltpu.VMEM((MOE_TM, D_MODEL), F32)],
    )
    return pl.pallas_call(
        _moe_kernel,
        out_shape=jax.ShapeDtypeStruct((total_tiles * MOE_TM, D_MODEL), BF16),
        grid_spec=grid_spec,
        input_output_aliases={len(args) - 1: 0},
        compiler_params=pltpu.CompilerParams(dimension_semantics=("arbitrary", "arbitrary"),
                                             vmem_limit_bytes=MOE_VMEM_LIMIT),
        name="moe_grouped",
    )(*args)


def _combine_ln_kernel(x_ref, ya_ref, yb_ref, w_ref, g_ref, b_ref, o_ref, ob_ref):
    w = w_ref[...]
    f = w[:, 0:1] * ya_ref[...].astype(F32) + w[:, 1:2] * yb_ref[...].astype(F32)
    out = _layer_norm_rows(ALPHA * x_ref[...] + f, g_ref[...], b_ref[...])
    o_ref[...] = out
    ob_ref[...] = out.astype(BF16)


def _combine_ln(x, y_rows, top_w, gamma, beta):
    T = x.shape[0]
    tm = min(512, T)
    nt = T // tm
    wpad = jnp.pad(top_w, ((0, 0), (0, LANES - TOP_K)))
    fixed = lambda i: (0, 0)
    row = lambda i: (i, 0)
    return pl.pallas_call(
        _combine_ln_kernel,
        out_shape=(jax.ShapeDtypeStruct((T, D_MODEL), F32), jax.ShapeDtypeStruct((T, D_MODEL), BF16)),
        grid=(nt,),
        in_specs=[pl.BlockSpec((tm, D_MODEL), row),
                  pl.BlockSpec((tm, D_MODEL), row),
                  pl.BlockSpec((tm, D_MODEL), lambda i: (i + nt, 0)),
                  pl.BlockSpec((tm, LANES), row),
                  pl.BlockSpec((1, D_MODEL), fixed), pl.BlockSpec((1, D_MODEL), fixed)],
        out_specs=(pl.BlockSpec((tm, D_MODEL), row), pl.BlockSpec((tm, D_MODEL), row)),
        compiler_params=_cparams("parallel"),
        name="combine_ln",
    )(x, y_rows, y_rows, wpad, gamma.reshape(1, -1), beta.reshape(1, -1))


def _moe_ln(x, xb, logits, w_gate, w_up, w_down, gamma, beta):
    T = x.shape[0]
    top_val, top_idx = lax.top_k(logits, TOP_K)
    top_w = jax.nn.softmax(top_val, axis=-1)

    e_flat = top_idx.reshape(-1).astype(jnp.int32)
    onehot = (e_flat[:, None] == jnp.arange(N_EXPERTS, dtype=jnp.int32)[None, :]).astype(jnp.int32)
    rank = jnp.sum((jnp.cumsum(onehot, axis=0) - onehot) * onehot, axis=1)
    counts = jnp.sum(onehot, axis=0)
    padded = ((counts + MOE_TM - 1) // MOE_TM) * MOE_TM
    ends = jnp.cumsum(padded)
    starts = ends - padded
    dest = starts[e_flat] + rank

    Tp = TOP_K * T + N_EXPERTS * MOE_TM
    n_tiles = Tp // MOE_TM
    token = jnp.arange(TOP_K * T, dtype=jnp.int32) // TOP_K
    src = jnp.zeros((Tp,), jnp.int32).at[dest].set(token)
    tile_start = jnp.arange(n_tiles, dtype=jnp.int32) * MOE_TM
    tile_expert = jnp.sum((tile_start[:, None] >= ends[None, :]).astype(jnp.int32), axis=1)
    last_expert = jnp.sum((jnp.maximum(ends[-1] - 1, 0) >= ends).astype(jnp.int32))
    tile_expert = jnp.minimum(tile_expert, last_expert).astype(jnp.int32)
    filled_end = (starts + counts)[tile_expert]
    tile_rows = jnp.where(tile_start < ends[-1], jnp.clip(filled_end - tile_start, 0, MOE_TM), 0).astype(jnp.int32)

    runs = max(d for d in range(1, MOE_RUNS + 1) if n_tiles % d == 0)
    tiles = n_tiles // runs
    ys = jnp.zeros((Tp, D_MODEL), BF16)
    for c in range(runs):
        xs = xb.at[src[c * tiles * MOE_TM:(c + 1) * tiles * MOE_TM]].get(mode="promise_in_bounds")
        ys = _moe_grouped(xs, tile_expert[c * tiles:(c + 1) * tiles], tile_rows[c * tiles:(c + 1) * tiles],
                          w_gate, w_up, w_down, ys, c * tiles, n_tiles)
    slot_major = dest.reshape(T, TOP_K).T.reshape(-1)
    y_rows = ys.at[slot_major].get(mode="promise_in_bounds")
    return _combine_ln(x, y_rows, top_w, gamma, beta)


def _memattn_kernel(q_ref, k_ref, v_ref, o_ref):
    q = q_ref[0]
    outs = []
    for h in range(MEM_HEADS):
        sl = slice(h * MEM_HEAD_DIM, (h + 1) * MEM_HEAD_DIM)
        s = _dot_nt(q[:, sl].astype(BF16), k_ref[0, :, sl]) * (MEM_HEAD_DIM ** -0.5)
        e = jnp.exp(s - jnp.max(s, axis=-1, keepdims=True))
        p = e / jnp.sum(e, axis=-1, keepdims=True)
        outs.append(_dot(p.astype(BF16), v_ref[0, :, sl]))
    o_ref[0] = jnp.concatenate(outs, axis=-1).astype(o_ref.dtype)


def _memory_attention(p3, mem_kv):
    B, S, _ = p3.shape
    M = mem_kv.shape[1]
    ts = min(512, S)
    return pl.pallas_call(
        _memattn_kernel,
        out_shape=jax.ShapeDtypeStruct((B, S, MEM_WIDTH), BF16),
        grid=(B, S // ts),
        in_specs=[pl.BlockSpec((1, ts, MEM_WIDTH), lambda b, s: (b, s, Q_MEM_OFF // MEM_WIDTH)),
                  pl.BlockSpec((1, M, MEM_WIDTH), lambda b, s: (b, 0, 0)),
                  pl.BlockSpec((1, M, MEM_WIDTH), lambda b, s: (b, 0, 1))],
        out_specs=pl.BlockSpec((1, ts, MEM_WIDTH), lambda b, s: (b, s, 0)),
        compiler_params=_cparams("parallel", "parallel"),
        name="mem_attn",
    )(p3, mem_kv, mem_kv)


def _ret_kernel(q_ref, k_ref, v_ref, g_ref, cos_ref, sin_ref, intra_ref, qd_ref, kd_ref, cd_ref,
                gn_ref, o_ref, state_ref):
    @pl.when(pl.program_id(1) == 0)
    def _():
        state_ref[...] = jnp.zeros_like(state_ref)

    cosf = cos_ref[0]
    sinf = sin_ref[0]
    half = RET_DK // 2
    rot = lambda x: x * cosf + pltpu.roll(x, half, 1) * sinf
    hs = range(RET_HEADS)
    ksl = [slice(h * RET_DK, (h + 1) * RET_DK) for h in hs]
    vsl = [slice(h * RET_DV, (h + 1) * RET_DV) for h in hs]

    qr = [rot(q_ref[0, :, ksl[h]].astype(F32)) for h in hs]
    kr = [rot(k_ref[0, :, ksl[h]].astype(F32)) * (RET_DK ** -0.5) for h in hs]
    vb = [v_ref[0, :, vsl[h]].astype(BF16) for h in hs]
    state = [state_ref[h] for h in hs]
    s = [_dot_nt(qr[h].astype(BF16), kr[h].astype(BF16)) * intra_ref[h] for h in hs]
    o = [_dot(s[h].astype(BF16), vb[h]) + _dot((qr[h] * qd_ref[h]).astype(BF16), state[h].astype(BF16))
         for h in hs]
    for h in hs:
        state_ref[h] = state[h] * cd_ref[h] + _dot_tn((kr[h] * kd_ref[h]).astype(BF16), vb[h])
    for h in hs:
        mu = jnp.mean(o[h], axis=-1, keepdims=True)
        oc = o[h] - mu
        var = jnp.mean(oc * oc, axis=-1, keepdims=True)
        y = oc * lax.rsqrt(var + LN_EPS) * gn_ref[0:1, vsl[h]] + gn_ref[1:2, vsl[h]]
        g = g_ref[0, :, vsl[h]].astype(F32)
        o_ref[0, :, vsl[h]] = (y * (g * jax.nn.sigmoid(g))).astype(o_ref.dtype)


def _retention_mixer(p3, positions, gn):
    B, S, _ = p3.shape
    C = RET_CHUNK
    H = RET_HEADS
    inv = 1.0 / (ROPE_BASE ** (jnp.arange(0, RET_DK, 2, dtype=F32) / RET_DK))
    ang = positions.astype(F32)[..., None] * inv
    cos, sin = jnp.cos(ang), jnp.sin(ang)
    cosf = jnp.concatenate([cos, cos], axis=-1)
    sinf = jnp.concatenate([-sin, sin], axis=-1)

    log_g = jnp.log1p(-jnp.exp2(-5.0 - jnp.arange(H, dtype=F32)))
    idx = jnp.arange(C, dtype=F32)
    diff = idx[:, None] - idx[None, :]
    intra = jnp.where(diff >= 0, jnp.exp(jnp.maximum(diff, 0.0)[None] * log_g[:, None, None]), 0.0)
    qd = jnp.broadcast_to(jnp.exp((idx + 1.0)[None, :, None] * log_g[:, None, None]), (H, C, RET_DK))
    kd = jnp.broadcast_to(jnp.exp((C - 1.0 - idx)[None, :, None] * log_g[:, None, None]), (H, C, RET_DK))
    cd = jnp.broadcast_to(jnp.exp(C * log_g)[:, None, None], (H, 1, RET_DV))

    QK = H * RET_DK
    assert 2 * QK == SEQ_WIDTH
    whole = lambda b, c: (0, 0, 0)
    return pl.pallas_call(
        _ret_kernel,
        out_shape=jax.ShapeDtypeStruct((B, S, SEQ_WIDTH), BF16),
        grid=(B, S // C),
        in_specs=[pl.BlockSpec((1, C, QK), lambda b, c: (b, c, 0)),
                  pl.BlockSpec((1, C, QK), lambda b, c: (b, c, 1)),
                  pl.BlockSpec((1, C, SEQ_WIDTH), lambda b, c: (b, c, 1)),
                  pl.BlockSpec((1, C, SEQ_WIDTH), lambda b, c: (b, c, 2)),
                  pl.BlockSpec((1, C, RET_DK), lambda b, c: (b, c, 0)),
                  pl.BlockSpec((1, C, RET_DK), lambda b, c: (b, c, 0)),
                  pl.BlockSpec((H, C, C), whole),
                  pl.BlockSpec((H, C, RET_DK), whole),
                  pl.BlockSpec((H, C, RET_DK), whole),
                  pl.BlockSpec((H, 1, RET_DV), whole),
                  pl.BlockSpec((2, SEQ_WIDTH), lambda b, c: (0, 0))],
        out_specs=pl.BlockSpec((1, C, SEQ_WIDTH), lambda b, c: (b, c, 0)),
        scratch_shapes=[pltpu.VMEM((H, RET_DK, RET_DV), F32)],
        compiler_params=_cparams("parallel", "arbitrary"),
        name="retention",
    )(p3, p3, p3, p3, cosf, sinf, intra, qd, kd, cd, gn)


RWKV_PAIRS = 12
RWKV_SUB = 2
PAIR = 2 * RWKV_HEAD
PREV_ROWS = 16

_NN = ((1,), (0,))
_NT = ((1,), (1,))
_TN = ((0,), (0,))


def _dg(a, b, dims):
    return lax.dot_general(a.astype(BF16), b.astype(BF16), (dims, ((), ())), preferred_element_type=F32)


def _rwkv_masks():
    L = RWKV_CHUNK
    row = np.arange(L)[:, None]
    lane = np.arange(PAIR)[None, :]
    m1 = np.broadcast_to(lane < RWKV_HEAD, (L, PAIR))
    small = np.stack([m1, ~m1, row > lane % L, row >= lane % L, row == lane % L]).astype(np.float32)
    same_head = (np.arange(2 * L)[:, None] // L == lane // L).astype(np.float32)
    tri = np.kron(np.eye(RWKV_SUB), row >= np.arange(L)[None, :]).astype(np.float32)
    return jnp.asarray(small), jnp.asarray(same_head, dtype=BF16), jnp.asarray(tri, dtype=BF16)


def _stacker(m1, m2):
    m1b, m2b = m1.astype(BF16), m2.astype(BF16)

    def stack(x):
        xb = x.astype(BF16)
        return jnp.concatenate([xb * m1b, xb * m2b], axis=0)

    return stack


def _rwkv_prepare(r, c, lw, k, v, kk, a, masks):
    n = len(r)
    rng = range(n)
    L = r[0].shape[0]
    assert L == RWKV_HEAD
    m1, m2, strict, incl, eye = masks
    stack = _stacker(m1, m2)
    blockdiag = stack
    strict_b = strict.astype(BF16)
    incl_b = jnp.concatenate([incl, incl], axis=1).astype(BF16)

    p_incl = [jnp.exp(c[j]) for j in rng]
    p_inv = [jnp.exp(-c[j]) for j in rng]
    at = [-kk[j] * jnp.exp(c[j] - lw[j]) for j in rng]
    rt = [r[j] * p_incl[j] for j in rng]
    lhs = [jnp.concatenate([at[j], rt[j]], axis=0).astype(BF16) for j in rng]
    rhs = [jnp.concatenate([stack(kk[j] * a[j] * p_inv[j]), stack(k[j] * p_inv[j])], axis=0)
           for j in rng]
    sc = [_dg(lhs[j], rhs[j], _NT) for j in rng]
    nmat = [sc[j][:L, :PAIR] * strict for j in rng]
    tinv = [eye + nmat[j] for j in rng]
    pw = [_dg(nmat[j], blockdiag(nmat[j]), _NN) for j in rng]
    rounds = int(np.log2(L)) - 1
    for i in range(rounds):
        if i < rounds - 1:
            both = [_dg(jnp.concatenate([pw[j], tinv[j]], axis=0), blockdiag(pw[j]), _NN) for j in rng]
            pw = [both[j][:L] for j in rng]
            tinv = [tinv[j] + both[j][L:] for j in rng]
        else:
            tinv = [tinv[j] + _dg(tinv[j], blockdiag(pw[j]), _NN) for j in rng]

    vs = [stack(v[j]) for j in rng]
    mv = [_dg(sc[j][:L, PAIR:].astype(BF16) * strict_b, vs[j], _NN) for j in rng]
    a_r = [sc[j][L:, :].astype(BF16) * incl_b for j in rng]
    return dict(lhs=lhs, rhs=rhs, vs=vs, mv=mv, a_r=a_r, tinv=[tinv[j].astype(BF16) for j in rng],
                decay=[p_incl[j][L - 1:L, :] for j in rng])


def _rwkv_apply(prep, entries, state, masks):
    m1, m2 = masks[0], masks[1]
    stack = _stacker(m1, m2)
    L = m1.shape[0]
    rng = range(len(entries))
    get = lambda name: [prep[name][e] for e in entries]
    lhs, rhs, vs, mv, a_r, tinv, decay = (get(n) for n in ("lhs", "rhs", "vs", "mv", "a_r", "tinv", "decay"))
    sg = [_dg(lhs[j], state[j], _NT) for j in rng]
    u = [_dg(tinv[j], stack(sg[j][:L] + mv[j]), _NN) for j in rng]
    uv = [jnp.concatenate([stack(u[j]), vs[j]], axis=0) for j in rng]
    ys = [_dg(a_r[j], uv[j], _NN) for j in rng]
    y = [sg[j][L:] + ys[j] for j in rng]
    new_state = [(state[j] + _dg(uv[j], rhs[j], _TN)) * decay[j] for j in rng]
    return y, new_state


def _shift_mix(x, prev_block, mu, first):
    prev_row = jnp.where(first, 0.0, prev_block[PREV_ROWS - 1:PREV_ROWS, :])
    rows = lax.broadcasted_iota(jnp.int32, x.shape, 0)
    prev = jnp.where(rows == 0, prev_row, pltpu.roll(x, 1, 0))
    return x + mu * (prev - x)


def _rwkv_mixer_kernel(*refs, with_v):
    it = iter(refs)
    r_ref, k_ref, v_ref, xg_ref, xl_ref = (next(it) for _ in range(5))
    rp_ref, kp_ref, vp_ref, xgp_ref, xlp_ref = (next(it) for _ in range(5))
    mu3_ref, mug_ref, mul_ref, vec_ref = (next(it) for _ in range(4))
    small_ref, ones_ref, tri_ref = (next(it) for _ in range(3))
    ww_ref, wa_ref = next(it), next(it)
    wv_ref = next(it) if with_v else None
    wg_ref = next(it)
    vf_ref = next(it) if with_v else None
    y_ref = next(it)
    vfo_ref = None if with_v else next(it)
    state_ref = next(it)

    first = pl.program_id(2) == 0

    @pl.when(first)
    def _():
        state_ref[...] = jnp.zeros_like(state_ref)

    L = RWKV_CHUNK
    mix = lambda x_ref, p_ref, mu: _shift_mix(x_ref[0].astype(F32), p_ref[0].astype(F32), mu, first)
    r = mix(r_ref, rp_ref, mu3_ref[0:1, :])
    k = mix(k_ref, kp_ref, mu3_ref[1:2, :])
    v = mix(v_ref, vp_ref, mu3_ref[2:3, :])
    xg = mix(xg_ref, xgp_ref, mug_ref[...])
    xl = mix(xl_ref, xlp_ref, mul_ref[...])

    w0, a0, v0 = vec_ref[0:1, :], vec_ref[1:2, :], vec_ref[2:3, :]
    k_k, k_a, r_k = vec_ref[3:4, :], vec_ref[4:5, :], vec_ref[5:6, :]
    gn_g, gn_b = vec_ref[6:7, :], vec_ref[7:8, :]

    lane = lax.broadcasted_iota(jnp.int32, xl.shape, 1)
    lin = jnp.where(lane < LORA_W, jnp.tanh(xl), xl).astype(BF16)
    w_arg = w0 + _dot(lin, ww_ref[...])
    a = jax.nn.sigmoid(a0 + _dot(lin, wa_ref[...]))
    if with_v:
        v = v + (vf_ref[0] - v) * jax.nn.sigmoid(v0 + _dot(lin, wv_ref[...]))
    else:
        vfo_ref[0] = v
    g = _dot(jax.nn.sigmoid(xg).astype(BF16), wg_ref[...])

    w_log = -(jnp.maximum(-w_arg, 0.0) + jnp.log1p(jnp.exp(-jnp.abs(w_arg)))) - 0.5
    lw = -jnp.exp(w_log)
    tri = tri_ref[...]
    h1 = lw.astype(BF16)
    h2 = (lw - h1.astype(F32)).astype(BF16)
    c = _dot(tri, h1) + _dot(tri, h2)

    kk_raw = k * k_k
    k2 = k * (1.0 + (a - 1.0) * k_a)
    ones_bd = ones_ref[...]
    masks = tuple(small_ref[i] for i in range(5))
    inv_n = 1.0 / RWKV_HEAD

    pairs = range(RWKV_PAIRS)
    cols = [slice(j * PAIR, (j + 1) * PAIR) for j in pairs]
    rows = [slice(s * L, (s + 1) * L) for s in range(RWKV_SUB)]
    where = [(rs, cs) for rs in rows for cs in cols]
    ent = range(len(where))
    part = lambda x: [x[rs, cs] for rs, cs in where]
    kkr, r_e, k_e, v_e, g_e = part(kk_raw), part(r), part(k2), part(v), part(g)
    red = [_dot(jnp.concatenate([kkr[e] * kkr[e], r_e[e] * k_e[e] * r_k[:, where[e][1]]], axis=0).astype(BF16),
                ones_bd) for e in ent]
    kks = [kkr[e] / jnp.maximum(jnp.sqrt(red[e][:L]), 1e-12) for e in ent]
    prep = _rwkv_prepare(r_e, part(c), part(lw), k_e, v_e, kks, part(a), masks)
    state = [state_ref[j] for j in pairs]
    ys = []
    for s in range(RWKV_SUB):
        y_s, state = _rwkv_apply(prep, [s * RWKV_PAIRS + j for j in pairs], state, masks)
        ys += y_s
    for j in pairs:
        state_ref[j] = state[j]
    yc = [ys[e] - _dot(ys[e].astype(BF16), ones_bd) * inv_n for e in ent]
    var = [_dot((yc[e] * yc[e]).astype(BF16), ones_bd) * inv_n for e in ent]
    for e in ent:
        rs, cs = where[e]
        yn = yc[e] * lax.rsqrt(var[e] + RWKV_GN_EPS) * gn_g[:, cs] + gn_b[:, cs]
        y_ref[0, rs, cs] = ((yn + red[e][L:] * v_e[e]) * g_e[e]).astype(y_ref.dtype)


def _rwkv_layout(w_in, mu, with_v):
    sizes = [SEQ_WIDTH] * 3 + [LORA_W, LORA_A] + ([LORA_V] if with_v else []) + [LORA_G]
    offs = np.concatenate([[0], np.cumsum(sizes)])
    seq_cols = int(offs[-1])
    g_idx = len(sizes) - 1
    pad = LORA_PACK - LORA_W - LORA_A - (LORA_V if with_v else 0)
    cols = [w_in[:, :3 * SEQ_WIDTH], w_in[:, seq_cols:], w_in[:, offs[g_idx]:offs[g_idx + 1]],
            w_in[:, offs[3]:offs[g_idx]], jnp.zeros((w_in.shape[0], pad), w_in.dtype)]
    mus = [mu[:3 * SEQ_WIDTH], jnp.zeros((MEM_WIDTH,), mu.dtype), mu[offs[g_idx]:offs[g_idx + 1]],
           mu[offs[3]:offs[g_idx]], jnp.zeros((pad,), mu.dtype)]
    return jnp.concatenate(cols, axis=1), jnp.concatenate(mus)


def _rwkv_mixer(p3, mu, lora_w, lora_a, lora_g, lora_v, vecs, v_first):
    B, S, _ = p3.shape
    L = RWKV_SUB * RWKV_CHUNK
    W = RWKV_PAIRS * PAIR
    above = lambda t: jnp.maximum(t * (L // PREV_ROWS) - 1, 0)
    with_v = lora_v is not None
    if with_v:
        w0, a0, v0, k_k, k_a, r_k, gn_g, gn_b = vecs
    else:
        w0, a0, k_k, k_a, r_k, gn_g, gn_b = vecs
        v0 = jnp.zeros_like(w0)
    vec8 = jnp.stack([w0, a0, v0, k_k, k_a, r_k, gn_g, gn_b])
    mu3 = jnp.concatenate([mu[:3 * SEQ_WIDTH].reshape(3, SEQ_WIDTH), jnp.zeros((5, SEQ_WIDTH), F32)])
    g_off = Q_MEM_OFF + MEM_WIDTH
    l_off = g_off + LORA_G
    mug = mu[g_off:g_off + LORA_G].reshape(1, LORA_G)
    mul = mu[l_off:l_off + LORA_PACK].reshape(1, LORA_PACK)
    rows = lambda w, lo: jnp.zeros((LORA_PACK, SEQ_WIDTH), F32).at[lo:lo + w.shape[0]].set(w).astype(BF16)

    def cols(width, off):
        blk = off // width
        return pl.BlockSpec((1, L, width), lambda b, g, t: (b, t, blk))

    def cols_prev(width, off):
        blk = off // width
        return pl.BlockSpec((1, PREV_ROWS, width), lambda b, g, t: (b, above(t), blk))

    def seq(off):
        blk = off // W
        return pl.BlockSpec((1, L, W), lambda b, g, t: (b, t, blk + g))

    def seq_prev(off):
        blk = off // W
        return pl.BlockSpec((1, PREV_ROWS, W), lambda b, g, t: (b, above(t), blk + g))

    per_group = lambda nrows: pl.BlockSpec((nrows, W), lambda b, g, t: (0, g))
    fixed = lambda ncols: pl.BlockSpec((1, ncols), lambda b, g, t: (0, 0))

    in_specs = [seq(0), seq(SEQ_WIDTH), seq(2 * SEQ_WIDTH), cols(LORA_G, g_off), cols(LORA_PACK, l_off),
                seq_prev(0), seq_prev(SEQ_WIDTH), seq_prev(2 * SEQ_WIDTH),
                cols_prev(LORA_G, g_off), cols_prev(LORA_PACK, l_off),
                per_group(8), fixed(LORA_G), fixed(LORA_PACK), per_group(8)]
    masks = _rwkv_masks()
    in_specs += [pl.BlockSpec(m.shape, lambda b, g, t, nd=m.ndim: (0,) * nd) for m in masks]
    in_specs += [per_group(LORA_PACK), per_group(LORA_PACK)]
    args = [p3] * 10 + [mu3, mug, mul, vec8, *masks, rows(lora_w, 0), rows(lora_a, LORA_W)]
    if with_v:
        in_specs.append(per_group(LORA_PACK))
        args.append(rows(lora_v, LORA_W + LORA_A))
    in_specs.append(per_group(LORA_G))
    args.append(lora_g.astype(BF16))
    out_spec = pl.BlockSpec((1, L, W), lambda b, g, t: (b, t, g))
    y_shape = jax.ShapeDtypeStruct((B, S, SEQ_WIDTH), BF16)
    if with_v:
        in_specs.append(out_spec)
        args.append(v_first)
        out_shape, out_specs = y_shape, out_spec
    else:
        out_shape = (y_shape, jax.ShapeDtypeStruct((B, S, SEQ_WIDTH), F32))
        out_specs = (out_spec, out_spec)

    res = pl.pallas_call(
        functools.partial(_rwkv_mixer_kernel, with_v=with_v),
        out_shape=out_shape,
        grid=(B, SEQ_WIDTH // W, S // L),
        in_specs=in_specs,
        out_specs=out_specs,
        scratch_shapes=[pltpu.VMEM((RWKV_PAIRS, PAIR, PAIR), F32)],
        compiler_params=_cparams("parallel", "parallel", "arbitrary"),
        name="rwkv_mixer",
    )(*args)
    return (res, v_first) if with_v else res


def kernel(x, mem, positions, mem_w_kv, l0_w_in, l0_mu, l0_lora_w, l0_lora_a, l0_lora_g, l0_vecs, l0_w_out, l0_norms, l0_ffn_gate, l0_ffn_up, l0_ffn_down, l1_w_in, l1_gn, l1_w_out, l1_norms, l1_router, l1_moe_gate, l1_moe_up, l1_moe_down, l2_w_in, l2_mu, l2_lora_w, l2_lora_a, l2_lora_v, l2_lora_g, l2_vecs, l2_w_out, l2_norms, l2_ffn_gate, l2_ffn_up, l2_ffn_down, l3_w_in, l3_gn, l3_w_out, l3_norms, l3_router, l3_moe_gate, l3_moe_up, l3_moe_down):
    B, S, D = x.shape
    T = B * S
    M = mem.shape[1]
    mem_kv = _matmul(mem.reshape(B * M, D), mem_w_kv.astype(BF16), tm=1024, tn=512,
                     out_dtype=BF16).reshape(B, M, 2 * MEM_WIDTH)

    rwkv_layers = {
        0: (l0_w_in, l0_mu, l0_lora_w, l0_lora_a, l0_lora_g, None, l0_vecs),
        2: (l2_w_in, l2_mu, l2_lora_w, l2_lora_a, l2_lora_g, l2_lora_v, l2_vecs),
    }
    ret_layers = {1: (l1_w_in, l1_gn), 3: (l3_w_in, l3_gn)}
    w_outs = (l0_w_out, l1_w_out, l2_w_out, l3_w_out)
    norms = (l0_norms, l1_norms, l2_norms, l3_norms)
    dense = {0: (l0_ffn_gate, l0_ffn_up, l0_ffn_down), 2: (l2_ffn_gate, l2_ffn_up, l2_ffn_down)}
    moe = {1: (l1_router, l1_moe_gate, l1_moe_up, l1_moe_down),
           3: (l3_router, l3_moe_gate, l3_moe_up, l3_moe_down)}

    xf = x.reshape(T, D)
    xin = xf
    v_first = None
    for i in range(DEPTH):
        if i in rwkv_layers:
            w_in, mu, lw_, la_, lg_, lv_, vecs = rwkv_layers[i]
            w_in, mu = _rwkv_layout(w_in, mu, lv_ is not None)
            p3 = _matmul(xin, w_in.astype(BF16), tm=512, tn=w_in.shape[1], out_dtype=BF16).reshape(B, S, -1)
            y_seq, v_first = _rwkv_mixer(p3, mu, lw_, la_, lg_, lv_, vecs, v_first)
        else:
            w_in, gn = ret_layers[i]
            p3 = _matmul(xin, w_in.astype(BF16), tm=512, tn=w_in.shape[1], out_dtype=BF16).reshape(B, S, -1)
            y_seq = _retention_mixer(p3, positions, gn)
        y_mem = _memory_attention(p3, mem_kv)
        nrm = norms[i]
        proj = _proj_ln(y_seq.reshape(T, SEQ_WIDTH), y_mem.reshape(T, MEM_WIDTH), w_outs[i], xf,
                        nrm[0], nrm[1], None if i in dense else moe[i][0])
        if i in dense:
            xf, xb = _ffn_ln(*proj, *dense[i], nrm[2], nrm[3])
        else:
            xf, xb = _moe_ln(*proj, *moe[i][1:], nrm[2], nrm[3])
        xin = xb
    return xf.reshape(B, S, D)
```
